```python
import jax, jax.numpy as jnp
from jax import lax
import numpy as np

D_MODEL = 1024
BATCH = 16
SEQ = 4096
DEPTH = 4

GRID_W = 64
CTX_LEN = 256
N_EVEN = (DEPTH + 1) // 2
N_ODD = DEPTH // 2
HEAD_DIM = 64
BLK = 128
POOL_WINDOWS = (2, 4, 8, 16)
POOL_GROUPS = 4
POOL_CH = D_MODEL // 2
POOL_GC = POOL_CH // POOL_GROUPS
B_HEADS = (D_MODEL // 2) // HEAD_DIM
B_KV = B_HEADS // 4
WINDOW = 128
AB_IN = POOL_CH + (B_HEADS + 2 * B_KV) * HEAD_DIM
C_HEADS = D_MODEL // HEAD_DIM
C_KV = C_HEADS // 4
C_IN = (C_HEADS + 2 * C_KV) * HEAD_DIM
D_FF = (8 * D_MODEL // 3) // 128 * 128
ROPE_BASE = 10000.0
EPS = 1e-6
NEG = -1e30

kernel_name = "hybrid_pool_window_global_convffn_dit"


def rms_norm(x, g):
    xf = x.astype(jnp.float32)
    y = xf * lax.rsqrt(jnp.mean(xf * xf, axis=-1, keepdims=True) + EPS)
    return (y * g.astype(jnp.float32)).astype(x.dtype)


def rope_tables(T):
    rows = T // GRID_W
    row = jnp.repeat(jnp.arange(rows), GRID_W).astype(jnp.float32)
    col = jnp.tile(jnp.arange(GRID_W), rows).astype(jnp.float32)
    n_freq = HEAD_DIM // 4
    freqs = ROPE_BASE ** (-jnp.arange(n_freq, dtype=jnp.float32) / n_freq)
    ang = jnp.concatenate([row[:, None] * freqs, col[:, None] * freqs], axis=-1)
    return jnp.cos(ang), jnp.sin(ang)


def apply_rope(x, cos, sin):
    half = HEAD_DIM // 2
    T = x.shape[1]
    shp = (1, T) + (1,) * (x.ndim - 3) + (half,)
    c, s = cos.reshape(shp), sin.reshape(shp)
    xf = x.astype(jnp.float32)
    x1, x2 = xf[..., :half], xf[..., half:]
    return jnp.concatenate([x1 * c - x2 * s, x2 * c + x1 * s], axis=-1).astype(x.dtype)


def attend(q, k, v, mask=None, sink=None):
    s = jnp.einsum('bqhgd,bkhd->bhgqk', q, k).astype(jnp.float32) * (HEAD_DIM ** -0.5)
    if mask is not None:
        s = jnp.where(mask, s, NEG)
    m = jnp.max(s, axis=-1, keepdims=True)
    if sink is not None:
        sk = sink.astype(jnp.float32)[None, :, :, None, None]
        m = jnp.maximum(m, sk)
    e = jnp.exp(s - m)
    den = jnp.sum(e, axis=-1, keepdims=True)
    if sink is not None:
        den = den + jnp.exp(sk - m)
    p = (e / den).astype(v.dtype)
    return jnp.einsum('bhgqk,bkhd->bqhgd', p, v)


def windowed_latent_attention(q, k, v, kc, vc, sink):
    B, T = q.shape[:2]
    nb = T // BLK
    pad = ((0, 0), (BLK, BLK), (0, 0), (0, 0))
    kp, vp = jnp.pad(k, pad), jnp.pad(v, pad)
    qi = jnp.arange(BLK)[:, None]
    ki = jnp.arange(3 * BLK)[None, :]
    ctx_ok = jnp.ones((BLK, kc.shape[1]), dtype=bool)

    def block(n):
        start = n * BLK
        qb = lax.dynamic_slice_in_dim(q, start, BLK, axis=1)
        kb = jnp.concatenate([lax.dynamic_slice_in_dim(kp, start, 3 * BLK, axis=1), kc], axis=1)
        vb = jnp.concatenate([lax.dynamic_slice_in_dim(vp, start, 3 * BLK, axis=1), vc], axis=1)
        qpos = start + qi
        kpos = start - BLK + ki
        band = (jnp.abs(qpos - kpos) <= WINDOW) & (kpos >= 0) & (kpos < T)
        mask = jnp.concatenate([band, ctx_ok], axis=1)
        return attend(qb, kb, vb, mask, sink)

    out = lax.map(block, jnp.arange(nb))
    return jnp.moveaxis(out, 0, 1).reshape(B, T, -1)


def global_latent_attention(q, k_all, v_all):
    B, T = q.shape[:2]
    nb = T // BLK

    def block(n):
        qb = lax.dynamic_slice_in_dim(q, n * BLK, BLK, axis=1)
        return attend(qb, k_all, v_all)

    out = lax.map(block, jnp.arange(nb))
    return jnp.moveaxis(out, 0, 1).reshape(B, T, -1)


def pool_mix(u, w, scale):
    B, T = u.shape[:2]
    uf = u.astype(jnp.float32)
    cs = jnp.concatenate([jnp.zeros_like(uf[:, :1]), jnp.cumsum(uf, axis=1)], axis=1)
    t = jnp.arange(T)
    means = []
    for g, win in enumerate(POOL_WINDOWS):
        lo = jnp.clip(t - win // 2, 0, T)
        hi = jnp.clip(t + win // 2, 0, T)
        cnt = (hi - lo).astype(jnp.float32)[None, :, None]
        means.append((cs[:, hi, g] - cs[:, lo, g]) / cnt)
    pooled = jnp.stack(means, axis=2)
    d = (pooled - uf).astype(u.dtype)
    y = jnp.einsum('btgc,gce->btge', d, w).reshape(B, T, POOL_CH)
    return y * scale


def split_ab(p):
    B, T = p.shape[:2]
    o = POOL_CH
    u = p[..., :o].reshape(B, T, POOL_GROUPS, POOL_GC)
    q = p[..., o:o + B_HEADS * HEAD_DIM].reshape(B, T, B_KV, B_HEADS // B_KV, HEAD_DIM)
    o += B_HEADS * HEAD_DIM
    k = p[..., o:o + B_KV * HEAD_DIM].reshape(B, T, B_KV, HEAD_DIM)
    o += B_KV * HEAD_DIM
    v = p[..., o:o + B_KV * HEAD_DIM].reshape(B, T, B_KV, HEAD_DIM)
    return u, q, k, v


def mixer_ab(h, hc, w_in, w_out, pool_w, pool_scale, sink, cos, sin, need_ctx):
    u, q, k, v = split_ab(h @ w_in)
    uc, qc, kc, vc = split_ab(hc @ w_in)
    q, k = apply_rope(q, cos, sin), apply_rope(k, cos, sin)
    a = windowed_latent_attention(q, k, v, kc, vc, sink)
    y = jnp.concatenate([pool_mix(u, pool_w, pool_scale), a], axis=-1) @ w_out
    yc = None
    if need_ctx:
        B, L = hc.shape[:2]
        ac = attend(qc, kc, vc, None, sink).reshape(B, L, -1)
        yc = jnp.concatenate([pool_mix(uc, pool_w, pool_scale), ac], axis=-1) @ w_out
    return y, yc


def split_c(p, q_g, k_g):
    B, T = p.shape[:2]
    o = C_HEADS * HEAD_DIM
    q = p[..., :o].reshape(B, T, C_KV, C_HEADS // C_KV, HEAD_DIM)
    k = p[..., o:o + C_KV * HEAD_DIM].reshape(B, T, C_KV, HEAD_DIM)
    o += C_KV * HEAD_DIM
    v = p[..., o:o + C_KV * HEAD_DIM].reshape(B, T, C_KV, HEAD_DIM)
    return rms_norm(q, q_g), rms_norm(k, k_g), v


def mixer_c(h, hc, w_qkv, w_out, q_g, k_g, cos, sin, need_ctx):
    q, k, v = split_c(h @ w_qkv, q_g, k_g)
    qc, kc, vc = split_c(hc @ w_qkv, q_g, k_g)
    q, k = apply_rope(q, cos, sin), apply_rope(k, cos, sin)
    k_all = jnp.concatenate([k, kc], axis=1)
    v_all = jnp.concatenate([v, vc], axis=1)
    y = global_latent_attention(q, k_all, v_all) @ w_out
    yc = None
    if need_ctx:
        B, L = hc.shape[:2]
        yc = attend(qc, kc, vc).reshape(B, L, -1) @ w_out
    return y, yc


def conv_ffn(h, w_up, conv_w, conv_b, w_down):
    u = h @ w_up
    up = jnp.pad(u, ((0, 0), (1, 1), (0, 0)))
    u = up[:, :-2] * conv_w[0] + up[:, 1:-1] * conv_w[1] + up[:, 2:] * conv_w[2] + conv_b
    a, b = jnp.split(u, 2, axis=-1)
    return (jax.nn.gelu(a) * b) @ w_down


def ada_params(cvec, w, b):
    return jnp.split(jax.nn.silu(cvec) @ w + b, 6, axis=-1)


def pre_mod(x, g, shift, scale):
    return rms_norm(x, g) * (1 + scale) + shift


def post_add(x, y, g, gate):
    return x + gate * rms_norm(y, g)


def setup_inputs(seed: int = 0) -> dict:
    key = jax.random.key(seed)
    ks = jax.random.split(key, 24)
    f32 = jnp.float32
    D = D_MODEL

    def nrm(k, shape, s):
        return jax.random.normal(k, shape, f32) * s

    return {
        "x": nrm(ks[0], (BATCH, SEQ, D), 1.0),
        "c": nrm(ks[1], (BATCH, D), 1.0),
        "ctx": nrm(ks[2], (BATCH, CTX_LEN, D), 1.0),
        "c_ctx": nrm(ks[3], (D,), 1.0),
        "ada_w": nrm(ks[4], (DEPTH, D, 6 * D), 0.5 * D ** -0.5),
        "ada_b": nrm(ks[5], (DEPTH, 6 * D), 0.02),
        "mix_pre_g": 1.0 + nrm(ks[6], (DEPTH, D), 0.02),
        "mix_post_g": 1.0 + nrm(ks[7], (DEPTH, D), 0.02),
        "ffn_pre_g": 1.0 + nrm(ks[8], (DEPTH, D), 0.02),
        "ffn_post_g": 1.0 + nrm(ks[9], (DEPTH, D), 0.02),
        "ab_w_in": nrm(ks[10], (N_EVEN, D, AB_IN), D ** -0.5),
        "ab_w_out": nrm(ks[11], (N_EVEN, POOL_CH + B_HEADS * HEAD_DIM, D), D ** -0.5),
        "pool_w": nrm(ks[12], (N_EVEN, POOL_GROUPS, POOL_GC, POOL_GC), POOL_GC ** -0.5),
        "pool_scale": 1.0 + nrm(ks[13], (N_EVEN, POOL_CH), 0.1),
        "sink_logit": nrm(ks[14], (N_EVEN, B_KV, B_HEADS // B_KV), 0.5),
        "c_w_qkv": nrm(ks[15], (N_ODD, D, C_IN), D ** -0.5),
        "c_w_out": nrm(ks[16], (N_ODD, C_HEADS * HEAD_DIM, D), D ** -0.5),
        "c_q_g": 1.0 + nrm(ks[17], (N_ODD, HEAD_DIM), 0.02),
        "c_k_g": 1.0 + nrm(ks[18], (N_ODD, HEAD_DIM), 0.02),
        "ffn_w_up": nrm(ks[19], (DEPTH, D, 2 * D_FF), D ** -0.5),
        "ffn_conv_w": nrm(ks[20], (DEPTH, 3, 2 * D_FF), 3 ** -0.5),
        "ffn_conv_b": nrm(ks[21], (DEPTH, 2 * D_FF), 0.02),
        "ffn_w_down": nrm(ks[22], (DEPTH, D_FF, D), D_FF ** -0.5),
    }


def reference(x, c, ctx, c_ctx, ada_w, ada_b, mix_pre_g, mix_post_g, ffn_pre_g, ffn_post_g,
              ab_w_in, ab_w_out, pool_w, pool_scale, sink_logit,
              c_w_qkv, c_w_out, c_q_g, c_k_g,
              ffn_w_up, ffn_conv_w, ffn_conv_b, ffn_w_down):
    T = x.shape[1]
    cos, sin = rope_tables(T)
    xc = ctx
    for l in range(DEPTH):
        need_ctx = l < DEPTH - 1
        sh1, sc1, g1, sh2, sc2, g2 = [m[:, None, :] for m in ada_params(c, ada_w[l], ada_b[l])]
        csh1, csc1, cg1, csh2, csc2, cg2 = ada_params(c_ctx, ada_w[l], ada_b[l])
        h = pre_mod(x, mix_pre_g[l], sh1, sc1)
        hc = pre_mod(xc, mix_pre_g[l], csh1, csc1)
        i = l // 2
        if l % 2 == 0:
            y, yc = mixer_ab(h, hc, ab_w_in[i], ab_w_out[i], pool_w[i], pool_scale[i], sink_logit[i],
                             cos, sin, need_ctx)
        else:
            y, yc = mixer_c(h, hc, c_w_qkv[i], c_w_out[i], c_q_g[i], c_k_g[i], cos, sin, need_ctx)
        x = post_add(x, y, mix_post_g[l], g1)
        f = conv_ffn(pre_mod(x, ffn_pre_g[l], sh2, sc2), ffn_w_up[l], ffn_conv_w[l], ffn_conv_b[l], ffn_w_down[l])
        x = post_add(x, f, ffn_post_g[l], g2)
        if need_ctx:
            xc = post_add(xc, yc, mix_post_g[l], cg1)
            fc = conv_ffn(pre_mod(xc, ffn_pre_g[l], csh2, csc2), ffn_w_up[l], ffn_conv_w[l], ffn_conv_b[l], ffn_w_down[l])
            xc = post_add(xc, fc, ffn_post_g[l], cg2)
    return x
```

```python
import functools

import jax
import jax.numpy as jnp
from jax import lax
from jax.experimental import pallas as pl
from jax.experimental.pallas import tpu as pltpu

F32 = jnp.float32
BF16 = jnp.bfloat16

HEAD_DIM = 64
GQA_GROUP = 4
POOL_WINDOWS = (2, 4, 8, 16)
POOL_GC = 128
WINDOW = 128
GRID_W = 64
ROPE_BASE = 10000.0
EPS = 1e-6
NEG = -1e30
N_MOD = 6

LANES = 128
SUBLANES = 8
VMEM_LIMIT_CAP = 56 * 1024 * 1024
VMEM_LIMIT_FLOOR = 32 * 1024 * 1024

PAIR = 2 * HEAD_DIM
GROUP_COLS = GQA_GROUP * HEAD_DIM
HALO = SUBLANES


def _vmem_limit(block_bytes, scratch_bytes=0, temp_bytes=0):
    need = 2 * block_bytes + scratch_bytes + temp_bytes
    return int(min(max(need + need // 4, VMEM_LIMIT_FLOOR), VMEM_LIMIT_CAP))


def _nbytes(shape, dtype):
    n = 1
    for s in shape:
        n *= s
    return n * jnp.dtype(dtype).itemsize


def _row_tile(t, want):
    tile = min(t, want)
    assert t % tile == 0
    return tile


def _ada_kernel(c_ref, w_ref, b_ref, o_ref):
    c = c_ref[...]
    a = (c * jax.nn.sigmoid(c)).astype(BF16)
    o_ref[0] = jnp.dot(a, w_ref[0].astype(BF16), preferred_element_type=F32) + b_ref[0]


def _ada_params(cc, ada_w, ada_b):
    depth, d, n = ada_w.shape
    rows = cc.shape[0]
    tn = n // N_MOD
    blocks = _nbytes((rows, d), F32) + _nbytes((d, tn), F32) + _nbytes((rows, tn), F32)
    return pl.pallas_call(
        _ada_kernel,
        grid=(depth, n // tn),
        in_specs=[
            pl.BlockSpec((rows, d), lambda l, j: (0, 0)),
            pl.BlockSpec((1, d, tn), lambda l, j: (l, 0, j)),
            pl.BlockSpec((1, 1, tn), lambda l, j: (l, 0, j)),
        ],
        out_specs=pl.BlockSpec((1, rows, tn), lambda l, j: (l, 0, j)),
        out_shape=jax.ShapeDtypeStruct((depth, rows, n), F32),
        compiler_params=pltpu.CompilerParams(
            dimension_semantics=("parallel", "parallel"),
            vmem_limit_bytes=_vmem_limit(blocks, temp_bytes=_nbytes((d, tn), BF16))),
        name="ada_params",
    )(cc, ada_w, ada_b.reshape(depth, 1, n))


def _rms_norm(x, g):
    return x * lax.rsqrt(jnp.mean(x * x, axis=-1, keepdims=True) + EPS) * g


def _pre_mod(x, g, shift, scale):
    return _rms_norm(x, g) * (1.0 + scale) + shift


def _inproj_kernel(*refs, n_u, n_q, n_kv, rope, qk_norm, tm):
    x_ref, sh_ref, sc_ref, g_ref, w_ref = refs[:5]
    pos = 5
    if rope:
        cos_ref, sin_ref = refs[pos:pos + 2]
        pos += 2
    if qk_norm:
        qg_ref, kg_ref, ones_ref = refs[pos:pos + 3]
        pos += 3
    out_refs = refs[pos:]

    h = _pre_mod(x_ref[0], g_ref[...], sh_ref[0], sc_ref[0]).astype(BF16)

    if rope:
        lane = lax.broadcasted_iota(jnp.int32, (tm, PAIR), 1)
        first_half = (lane & (HEAD_DIM - 1)) < HEAD_DIM // 2
        cos = cos_ref[...]
        sin = sin_ref[...]

    def head_epilogue(a, gain_ref, scale):
        if qk_norm:
            a2 = a * a
            hi = a2.astype(BF16)
            lo = (a2 - hi.astype(F32)).astype(BF16)
            ss = (jnp.dot(hi, ones_ref[...], preferred_element_type=F32)
                  + jnp.dot(lo, ones_ref[...], preferred_element_type=F32))
            a = a * lax.rsqrt(ss * (1.0 / HEAD_DIM) + EPS) * gain_ref[...]
        if rope:
            partner = jnp.where(first_half,
                                pltpu.roll(a, PAIR - HEAD_DIM // 2, 1),
                                pltpu.roll(a, HEAD_DIM // 2, 1))
            a = a * cos + partner * sin
        if scale != 1.0:
            a = a * scale
        return a.astype(BF16)

    col = 0
    out_idx = 0
    if n_u:
        out_refs[out_idx][0] = jnp.dot(h, w_ref[:, col:col + n_u], preferred_element_type=F32)
        col += n_u
        out_idx += 1

    acc = jnp.dot(h, w_ref[:, col:col + n_q], preferred_element_type=F32)
    for c in range(n_q // PAIR):
        out_refs[out_idx][0, :, c * PAIR:(c + 1) * PAIR] = head_epilogue(
            acc[:, c * PAIR:(c + 1) * PAIR], qg_ref if qk_norm else None, HEAD_DIM ** -0.5)
    col += n_q
    out_idx += 1

    acc = jnp.dot(h, w_ref[:, col:col + n_kv], preferred_element_type=F32)
    for c in range(n_kv // PAIR):
        out_refs[out_idx][0, :, c * PAIR:(c + 1) * PAIR] = head_epilogue(
            acc[:, c * PAIR:(c + 1) * PAIR], kg_ref if qk_norm else None, 1.0)
    col += n_kv
    out_idx += 1

    out_refs[out_idx][0] = jnp.dot(h, w_ref[:, col:col + n_kv],
                                   preferred_element_type=F32).astype(BF16)


def _inproj(x, mods, mod_row, pre_g, w, *, n_u, n_q, n_kv, rope_tabs=None, qk_gains=None):
    b, t, d = x.shape
    n = w.shape[1]
    tm = _row_tile(t, 512)
    rope = rope_tabs is not None
    qk_norm = qk_gains is not None

    in_specs = [
        pl.BlockSpec((1, tm, d), lambda bi, i: (bi, i, 0)),
        pl.BlockSpec((1, 1, d), lambda bi, i: (mod_row(bi), 0, 0)),
        pl.BlockSpec((1, 1, d), lambda bi, i: (mod_row(bi), 0, 1)),
        pl.BlockSpec((1, d), lambda bi, i: (0, 0)),
        pl.BlockSpec((d, n), lambda bi, i: (0, 0)),
    ]
    args = [x, mods, mods, pre_g, w]
    if rope:
        in_specs += [pl.BlockSpec((tm, PAIR), lambda bi, i: (i, 0))] * 2
        args += list(rope_tabs)
    if qk_norm:
        in_specs += [pl.BlockSpec((1, PAIR), lambda bi, i: (0, 0))] * 2
        in_specs += [pl.BlockSpec((PAIR, PAIR), lambda bi, i: (0, 0))]
        args += list(qk_gains)

    out_shape, out_specs = [], []
    for width, dtype in ((n_u, F32), (n_q, BF16), (n_kv, BF16), (n_kv, BF16)):
        if width:
            out_shape.append(jax.ShapeDtypeStruct((b, t, width), dtype))
            out_specs.append(pl.BlockSpec((1, tm, width), lambda bi, i: (bi, i, 0)))

    blocks = (_nbytes((tm, d), F32) + _nbytes((d, n), BF16) + _nbytes((tm, n_u), F32)
              + _nbytes((tm, n_q + 2 * n_kv), BF16) + 2 * _nbytes((tm, PAIR), F32))
    temps = _nbytes((tm, d), F32) * 2 + _nbytes((tm, max(n_q, n_kv, n_u)), F32) * 2
    return pl.pallas_call(
        functools.partial(_inproj_kernel, n_u=n_u, n_q=n_q, n_kv=n_kv, rope=rope,
                          qk_norm=qk_norm, tm=tm),
        grid=(b, t // tm),
        in_specs=in_specs,
        out_specs=out_specs,
        out_shape=out_shape,
        compiler_params=pltpu.CompilerParams(
            dimension_semantics=("parallel", "parallel"),
            vmem_limit_bytes=_vmem_limit(blocks, temp_bytes=temps)),
        name="mixer_inproj",
    )(*args)


def _stack_heads(q_ref, qs_ref, tq):
    lane = lax.broadcasted_iota(jnp.int32, (tq, PAIR), 1)
    low = lane < HEAD_DIM
    for p in range(GROUP_COLS // PAIR):
        qp = q_ref[0, :, p * PAIR:(p + 1) * PAIR].astype(F32)
        qs_ref[(2 * p) * tq:(2 * p + 1) * tq, :] = jnp.where(low, qp, 0.0).astype(BF16)
        qs_ref[(2 * p + 1) * tq:(2 * p + 2) * tq, :] = jnp.where(low, 0.0, qp).astype(BF16)


def _unstack_heads(o, o_ref, tq):
    lane = lax.broadcasted_iota(jnp.int32, (tq, PAIR), 1)
    low = lane < HEAD_DIM
    for p in range(GROUP_COLS // PAIR):
        even = o[(2 * p) * tq:(2 * p + 1) * tq, :]
        odd = o[(2 * p + 1) * tq:(2 * p + 2) * tq, :]
        o_ref[0, :, p * PAIR:(p + 1) * PAIR] = jnp.where(low, even, odd).astype(o_ref.dtype)


def _scores(qs, k):
    return lax.dot_general(qs, k, (((1,), (1,)), ((), ())), preferred_element_type=F32)


def _flash_kernel(*refs, tq, tk, n_keys, has_sink):
    if has_sink:
        q_ref, k_ref, v_ref, sink_ref, o_ref, qs_ref, m_ref, l_ref, acc_ref = refs
    else:
        q_ref, k_ref, v_ref, o_ref, qs_ref, m_ref, l_ref, acc_ref = refs
    _stack_heads(q_ref, qs_ref, tq)
    if has_sink:
        m_ref[...] = sink_ref[0]
        l_ref[...] = jnp.ones_like(l_ref)
    else:
        m_ref[...] = jnp.full_like(m_ref, NEG)
        l_ref[...] = jnp.zeros_like(l_ref)
    acc_ref[...] = jnp.zeros_like(acc_ref)

    def step(start, size):
        k = k_ref[0, pl.ds(start, size), :]
        v = v_ref[0, pl.ds(start, size), :]
        s = _scores(qs_ref[...], k)
        m_prev = m_ref[...]
        m_new = jnp.maximum(m_prev, jnp.max(s, axis=1, keepdims=True))
        alpha = jnp.exp(m_prev - m_new)
        p = jnp.exp(s - m_new)
        l_ref[...] = alpha * l_ref[...] + jnp.sum(p, axis=1, keepdims=True)
        acc_ref[...] = alpha * acc_ref[...] + jnp.dot(p.astype(BF16), v,
                                                     preferred_element_type=F32)
        m_ref[...] = m_new

    n_full = n_keys // tk
    if n_full:
        def body(i, carry):
            step(pl.multiple_of(i * tk, tk), tk)
            return carry
        lax.fori_loop(0, n_full, body, 0)
    if n_keys % tk:
        step(n_full * tk, n_keys % tk)

    _unstack_heads(acc_ref[...] / l_ref[...], o_ref, tq)


def _flash_attention(q, k2, v2, sink_col=None, *, tq_want=128, tk_want=512):
    b, t, hq = q.shape
    n_keys = k2.shape[1]
    n_groups = hq // GROUP_COLS
    tq = _row_tile(t, tq_want)
    tk = min(tk_want, n_keys)
    rows = GQA_GROUP * tq
    has_sink = sink_col is not None

    in_specs = [
        pl.BlockSpec((1, tq, GROUP_COLS), lambda bi, j, i: (bi, i, j)),
        pl.BlockSpec((1, n_keys, PAIR), lambda bi, j, i: (bi, 0, j)),
        pl.BlockSpec((1, n_keys, PAIR), lambda bi, j, i: (bi, 0, j)),
    ]
    args = [q, k2, v2]
    if has_sink:
        in_specs.append(pl.BlockSpec((1, rows, 1), lambda bi, j, i: (j, 0, 0)))
        args.append(sink_col)
    blocks = 2 * _nbytes((tq, GROUP_COLS), BF16) + 2 * _nbytes((n_keys, PAIR), BF16)
    scratch = _nbytes((rows, PAIR), BF16) + 3 * _nbytes((rows, PAIR), F32)
    temps = 3 * _nbytes((rows, tk), F32)
    return pl.pallas_call(
        functools.partial(_flash_kernel, tq=tq, tk=tk, n_keys=n_keys, has_sink=has_sink),
        grid=(b, n_groups, t // tq),
        in_specs=in_specs,
        out_specs=pl.BlockSpec((1, tq, GROUP_COLS), lambda bi, j, i: (bi, i, j)),
        out_shape=jax.ShapeDtypeStruct((b, t, hq), BF16),
        scratch_shapes=[
            pltpu.VMEM((rows, PAIR), BF16),
            pltpu.VMEM((rows, 1), F32),
            pltpu.VMEM((rows, 1), F32),
            pltpu.VMEM((rows, PAIR), F32),
        ],
        compiler_params=pltpu.CompilerParams(
            dimension_semantics=("parallel", "parallel", "parallel"),
            vmem_limit_bytes=_vmem_limit(blocks, scratch, temps)),
        name="flash_attention",
    )(*args)


def _window_kernel(q_ref, k_ref, v_ref, kc_ref, vc_ref, sink_ref, o_ref, qs_ref, *, tq, t):
    span = tq + 2 * WINDOW
    q0 = pl.program_id(2) * tq
    start = pl.multiple_of(jnp.clip(q0 - WINDOW, 0, t - span), WINDOW)
    _stack_heads(q_ref, qs_ref, tq)
    qs = qs_ref[...]

    s_band = _scores(qs, k_ref[0, pl.ds(start, span), :])
    row = lax.broadcasted_iota(jnp.int32, s_band.shape, 0)
    col = lax.broadcasted_iota(jnp.int32, s_band.shape, 1)
    delta = (q0 + (row & (tq - 1))) - (start + col)
    s_band = jnp.where((delta <= WINDOW) & (delta >= -WINDOW), s_band, NEG)
    s_ctx = _scores(qs, kc_ref[0])

    sink = sink_ref[0]
    m = jnp.maximum(jnp.maximum(jnp.max(s_band, axis=1, keepdims=True),
                                jnp.max(s_ctx, axis=1, keepdims=True)), sink)
    e_band = jnp.exp(s_band - m)
    e_ctx = jnp.exp(s_ctx - m)
    den = (jnp.sum(e_band, axis=1, keepdims=True) + jnp.sum(e_ctx, axis=1, keepdims=True)
           + jnp.exp(sink - m))
    acc = (jnp.dot(e_band.astype(BF16), v_ref[0, pl.ds(start, span), :],
                   preferred_element_type=F32)
           + jnp.dot(e_ctx.astype(BF16), vc_ref[0], preferred_element_type=F32))
    _unstack_heads(acc / den, o_ref, tq)


def _window_attention(q, k2, v2, kc2, vc2, sink_col, *, tq):
    b, t, hq = q.shape
    n_ctx = kc2.shape[1]
    n_groups = hq // GROUP_COLS
    rows = GQA_GROUP * tq
    span = tq + 2 * WINDOW
    assert t % tq == 0 and t >= span and tq & (tq - 1) == 0
    blocks = (2 * _nbytes((tq, GROUP_COLS), BF16) + 2 * _nbytes((t, PAIR), BF16)
              + 2 * _nbytes((n_ctx, PAIR), BF16))
    temps = 4 * _nbytes((rows, span + n_ctx), F32)
    return pl.pallas_call(
        functools.partial(_window_kernel, tq=tq, t=t),
        grid=(b, n_groups, t // tq),
        in_specs=[
            pl.BlockSpec((1, tq, GROUP_COLS), lambda bi, j, i: (bi, i, j)),
            pl.BlockSpec((1, t, PAIR), lambda bi, j, i: (bi, 0, j)),
            pl.BlockSpec((1, t, PAIR), lambda bi, j, i: (bi, 0, j)),
            pl.BlockSpec((1, n_ctx, PAIR), lambda bi, j, i: (bi, 0, j)),
            pl.BlockSpec((1, n_ctx, PAIR), lambda bi, j, i: (bi, 0, j)),
            pl.BlockSpec((1, rows, 1), lambda bi, j, i: (j, 0, 0)),
        ],
        out_specs=pl.BlockSpec((1, tq, GROUP_COLS), lambda bi, j, i: (bi, i, j)),
        out_shape=jax.ShapeDtypeStruct((b, t, hq), BF16),
        scratch_shapes=[pltpu.VMEM((rows, PAIR), BF16)],
        compiler_params=pltpu.CompilerParams(
            dimension_semantics=("parallel", "parallel", "parallel"),
            vmem_limit_bytes=_vmem_limit(blocks, _nbytes((rows, PAIR), BF16), temps)),
        name="window_attention",
    )(q, k2, v2, kc2, vc2, sink_col)


def _pool_kernel(u_ref, w_ref, scale_ref, o_ref, pad_ref, *, t):
    reach = max(POOL_WINDOWS) // 2
    assert reach <= HALO
    pos = lax.broadcasted_iota(jnp.int32, (t, POOL_GC), 0)
    zeros = jnp.zeros((HALO, POOL_GC), F32)
    pad_ref[0:HALO, :] = zeros
    pad_ref[HALO + t:2 * HALO + t, :] = zeros
    for g, win in enumerate(POOL_WINDOWS):
        half = win // 2
        u = u_ref[0, :, g * POOL_GC:(g + 1) * POOL_GC]
        pad_ref[HALO:HALO + t, :] = u
        total = pad_ref[pl.ds(HALO - half, t), :]
        for off in range(1 - half, half):
            total = total + pad_ref[pl.ds(HALO + off, t), :]
        cnt = (jnp.minimum(pos + half, t) - jnp.maximum(pos - half, 0)).astype(F32)
        d = (total / cnt - u).astype(BF16)
        y = jnp.dot(d, w_ref[g], preferred_element_type=F32)
        o_ref[0, :, g * POOL_GC:(g + 1) * POOL_GC] = (
            y * scale_ref[:, g * POOL_GC:(g + 1) * POOL_GC]).astype(BF16)


def _pool_mix(u, pool_w, pool_scale):
    b, t, ch = u.shape
    n_g = len(POOL_WINDOWS)
    assert ch == n_g * POOL_GC
    blocks = _nbytes((t, ch), F32) + _nbytes((t, ch), BF16) + _nbytes((n_g, POOL_GC, POOL_GC), BF16)
    scratch = _nbytes((t + 2 * HALO, POOL_GC), F32)
    return pl.pallas_call(
        functools.partial(_pool_kernel, t=t),
        grid=(b,),
        in_specs=[
            pl.BlockSpec((1, t, ch), lambda bi: (bi, 0, 0)),
            pl.BlockSpec((n_g, POOL_GC, POOL_GC), lambda bi: (0, 0, 0)),
            pl.BlockSpec((1, ch), lambda bi: (0, 0)),
        ],
        out_specs=pl.BlockSpec((1, t, ch), lambda bi: (bi, 0, 0)),
        out_shape=jax.ShapeDtypeStruct((b, t, ch), BF16),
        scratch_shapes=[pltpu.VMEM((t + 2 * HALO, POOL_GC), F32)],
        compiler_params=pltpu.CompilerParams(
            dimension_semantics=("parallel",),
            vmem_limit_bytes=_vmem_limit(blocks, scratch, 6 * _nbytes((t, POOL_GC), F32))),
        name="pool_mix",
    )(u, pool_w, pool_scale)


def _outproj_kernel(*refs, n_in):
    a_refs = refs[:n_in]
    w_refs = refs[n_in:2 * n_in]
    x_ref, gate_ref, g_ref, o_ref = refs[2 * n_in:]
    y = jnp.dot(a_refs[0][0], w_refs[0][...], preferred_element_type=F32)
    for a_ref, w_ref in zip(a_refs[1:], w_refs[1:]):
        y = y + jnp.dot(a_ref[0], w_ref[...], preferred_element_type=F32)
    o_ref[0] = x_ref[0] + gate_ref[0] * _rms_norm(y, g_ref[...])


def _outproj(acts, weights, x, mods, mod_row, post_g):
    b, t, d = x.shape
    tm = _row_tile(t, 512)
    n_in = len(acts)
    in_specs = [pl.BlockSpec((1, tm, a.shape[2]), lambda bi, i: (bi, i, 0)) for a in acts]
    in_specs += [pl.BlockSpec(w.shape, lambda bi, i: (0, 0)) for w in weights]
    in_specs += [
        pl.BlockSpec((1, tm, d), lambda bi, i: (bi, i, 0)),
        pl.BlockSpec((1, 1, d), lambda bi, i: (mod_row(bi), 0, 2)),
        pl.BlockSpec((1, d), lambda bi, i: (0, 0)),
    ]
    blocks = (sum(_nbytes((tm, a.shape[2]), BF16) for a in acts)
              + sum(_nbytes(w.shape, BF16) for w in weights) + 2 * _nbytes((tm, d), F32))
    return pl.pallas_call(
        functools.partial(_outproj_kernel, n_in=n_in),
        grid=(b, t // tm),
        in_specs=in_specs,
        out_specs=pl.BlockSpec((1, tm, d), lambda bi, i: (bi, i, 0)),
        out_shape=jax.ShapeDtypeStruct((b, t, d), F32),
        compiler_params=pltpu.CompilerParams(
            dimension_semantics=("parallel", "parallel"),
            vmem_limit_bytes=_vmem_limit(blocks, temp_bytes=3 * _nbytes((tm, d), F32))),
        name="mixer_outproj",
    )(*acts, *weights, x, mods, post_g)


def _ffn_kernel(x_ref, xp_ref, xn_ref, sh_ref, sc_ref, gate_ref, pre_g_ref, post_g_ref,
                wa_ref, wb_ref, cwa_ref, cwb_ref, cba_ref, cbb_ref, wd_ref, o_ref,
                h_ref, ua_ref, ub_ref, acc_ref, *, tm):
    i = pl.program_id(1)
    j = pl.program_id(2)

    @pl.when(j == 0)
    def _():
        g, sh, sc = pre_g_ref[...], sh_ref[0], sc_ref[0]
        h_prev = _pre_mod(xp_ref[0], g, sh, sc)
        h_next = _pre_mod(xn_ref[0], g, sh, sc)
        h_prev = jnp.where(i == 0, 0.0, h_prev)
        h_next = jnp.where(i == pl.num_programs(1) - 1, 0.0, h_next)
        h_ref[0:HALO, :] = h_prev.astype(BF16)
        h_ref[HALO:HALO + tm, :] = _pre_mod(x_ref[0], g, sh, sc).astype(BF16)
        h_ref[HALO + tm:2 * HALO + tm, :] = h_next.astype(BF16)
        acc_ref[...] = jnp.zeros_like(acc_ref)

    h = h_ref[...]
    ua_ref[...] = jnp.dot(h, wa_ref[...], preferred_element_type=F32)
    ub_ref[...] = jnp.dot(h, wb_ref[...], preferred_element_type=F32)

    def conv(u_ref, cw_ref, cb_ref):
        return (u_ref[pl.ds(HALO - 1, tm), :] * cw_ref[0:1, :]
                + u_ref[pl.ds(HALO, tm), :] * cw_ref[1:2, :]
                + u_ref[pl.ds(HALO + 1, tm), :] * cw_ref[2:3, :]
                + cb_ref[...])

    gated = (jax.nn.gelu(conv(ua_ref, cwa_ref, cba_ref)) * conv(ub_ref, cwb_ref, cbb_ref))
    acc_ref[...] += jnp.dot(gated.astype(BF16), wd_ref[...], preferred_element_type=F32)

    @pl.when(j == pl.num_programs(2) - 1)
    def _():
        o_ref[0] = x_ref[0] + gate_ref[0] * _rms_norm(acc_ref[...], post_g_ref[...])


def _conv_ffn(x, mods, mod_row, pre_g, post_g, w_up, conv_w, conv_b, w_down, *, tm_want=512,
              fc=384):
    b, t, d = x.shape
    d_ff = w_down.shape[0]
    tm = _row_tile(t, tm_want)
    assert d_ff % fc == 0 and tm % HALO == 0
    nj = d_ff // fc
    halo_blocks = tm // HALO
    last_halo = t // HALO - 1
    rows = tm + 2 * HALO

    in_specs = [
        pl.BlockSpec((1, tm, d), lambda bi, i, j: (bi, i, 0)),
        pl.BlockSpec((1, HALO, d), lambda bi, i, j: (bi, jnp.maximum(i * halo_blocks - 1, 0), 0)),
        pl.BlockSpec((1, HALO, d),
                     lambda bi, i, j: (bi, jnp.minimum((i + 1) * halo_blocks, last_halo), 0)),
        pl.BlockSpec((1, 1, d), lambda bi, i, j: (mod_row(bi), 0, 3)),
        pl.BlockSpec((1, 1, d), lambda bi, i, j: (mod_row(bi), 0, 4)),
        pl.BlockSpec((1, 1, d), lambda bi, i, j: (mod_row(bi), 0, 5)),
        pl.BlockSpec((1, d), lambda bi, i, j: (0, 0)),
        pl.BlockSpec((1, d), lambda bi, i, j: (0, 0)),
        pl.BlockSpec((d, fc), lambda bi, i, j: (0, j)),
        pl.BlockSpec((d, fc), lambda bi, i, j: (0, nj + j)),
        pl.BlockSpec((3, fc), lambda bi, i, j: (0, j)),
        pl.BlockSpec((3, fc), lambda bi, i, j: (0, nj + j)),
        pl.BlockSpec((1, fc), lambda bi, i, j: (0, j)),
        pl.BlockSpec((1, fc), lambda bi, i, j: (0, nj + j)),
        pl.BlockSpec((fc, d), lambda bi, i, j: (j, 0)),
    ]
    blocks = (2 * _nbytes((tm, d), F32) + 2 * _nbytes((HALO, d), F32)
              + 3 * _nbytes((d, fc), BF16))
    scratch = (_nbytes((rows, d), BF16) + 2 * _nbytes((rows, fc), F32) + _nbytes((tm, d), F32))
    temps = 4 * _nbytes((rows, fc), F32) + _nbytes((tm, d), F32)
    return pl.pallas_call(
        functools.partial(_ffn_kernel, tm=tm),
        grid=(b, t // tm, nj),
        in_specs=in_specs,
        out_specs=pl.BlockSpec((1, tm, d), lambda bi, i, j: (bi, i, 0)),
        out_shape=jax.ShapeDtypeStruct((b, t, d), F32),
        scratch_shapes=[
            pltpu.VMEM((rows, d), BF16),
            pltpu.VMEM((rows, fc), F32),
            pltpu.VMEM((rows, fc), F32),
            pltpu.VMEM((tm, d), F32),
        ],
        compiler_params=pltpu.CompilerParams(
            dimension_semantics=("parallel", "parallel", "arbitrary"),
            vmem_limit_bytes=_vmem_limit(blocks, scratch, temps)),
        name="conv_ffn",
    )(x, x, x, mods, mods, mods, pre_g, post_g, w_up, w_up, conv_w, conv_w, conv_b, conv_b, w_down)


def _rope_tables(t):
    rows = t // GRID_W
    row = jnp.repeat(jnp.arange(rows), GRID_W).astype(F32)
    col = jnp.tile(jnp.arange(GRID_W), rows).astype(F32)
    n_freq = HEAD_DIM // 4
    freqs = ROPE_BASE ** (-jnp.arange(n_freq, dtype=F32) / n_freq)
    ang = jnp.concatenate([row[:, None] * freqs, col[:, None] * freqs], axis=-1)
    cos, sin = jnp.cos(ang), jnp.sin(ang)
    return (jnp.concatenate([cos, cos, cos, cos], axis=-1),
            jnp.concatenate([-sin, sin, -sin, sin], axis=-1))


def _dup_heads(w, n_heads):
    d = w.shape[0]
    w = w.reshape(d, n_heads, 1, HEAD_DIM)
    return jnp.broadcast_to(w, (d, n_heads, 2, HEAD_DIM)).reshape(d, n_heads * PAIR)


def _sink_column(sink, tq):
    return jnp.repeat(sink.astype(F32), tq, axis=1)[..., None]


def kernel(x, c, ctx, c_ctx, ada_w, ada_b, mix_pre_g, mix_post_g, ffn_pre_g, ffn_post_g,
           ab_w_in, ab_w_out, pool_w, pool_scale, sink_logit,
           c_w_qkv, c_w_out, c_q_g, c_k_g,
           ffn_w_up, ffn_conv_w, ffn_conv_b, ffn_w_down):
    batch, t, d = x.shape
    n_ctx = ctx.shape[1]
    depth = ada_w.shape[0]
    pool_ch = pool_scale.shape[1]
    b_kv = sink_logit.shape[1]
    b_q = b_kv * GQA_GROUP * HEAD_DIM
    c_q = c_w_out.shape[1]
    c_kv = c_q // HEAD_DIM // GQA_GROUP

    ctx_row = batch
    n_rows = -(-(batch + 1) // SUBLANES) * SUBLANES
    cc = jnp.zeros((n_rows, d), F32).at[:batch].set(c).at[ctx_row].set(c_ctx)
    mods_all = _ada_params(cc, ada_w, ada_b).reshape(depth, n_rows, 1, N_MOD * d)

    rope_tabs = _rope_tables(t)
    ones_bd = jnp.kron(jnp.eye(2, dtype=F32), jnp.ones((HEAD_DIM, HEAD_DIM), F32)).astype(BF16)
    latent_row = lambda bi: bi
    context_row = lambda bi: ctx_row
    win_tq = min(128, t)

    xc = ctx
    for l in range(depth):
        need_ctx = l < depth - 1
        i = l // 2
        mods = mods_all[l]
        pre_g, post_g = mix_pre_g[l][None], mix_post_g[l][None]
        if l % 2 == 0:
            w = ab_w_in[i]
            o = pool_ch
            w_in = jnp.concatenate([
                w[:, :o + b_q],
                _dup_heads(w[:, o + b_q:o + b_q + b_kv * HEAD_DIM], b_kv),
                _dup_heads(w[:, o + b_q + b_kv * HEAD_DIM:], b_kv)], axis=1).astype(BF16)
            widths = dict(n_u=pool_ch, n_q=b_q, n_kv=b_kv * PAIR)
            u, q, k2, v2 = _inproj(x, mods, latent_row, pre_g, w_in, rope_tabs=rope_tabs, **widths)
            uc, qc, kc2, vc2 = _inproj(xc, mods, context_row, pre_g, w_in, **widths)
            w_out = ab_w_out[i].astype(BF16)
            w_outs = [w_out[:pool_ch], w_out[pool_ch:]]
            pw, ps = pool_w[i].astype(BF16), pool_scale[i][None]
            a = _window_attention(q, k2, v2, kc2, vc2, _sink_column(sink_logit[i], win_tq),
                                  tq=win_tq)
            x = _outproj([_pool_mix(u, pw, ps), a], w_outs, x, mods, latent_row, post_g)
            if need_ctx:
                tqc = min(128, n_ctx)
                ac = _flash_attention(qc, kc2, vc2, _sink_column(sink_logit[i], tqc), tq_want=tqc)
                xc = _outproj([_pool_mix(uc, pw, ps), ac], w_outs, xc, mods, context_row, post_g)
        else:
            w = c_w_qkv[i]
            w_in = jnp.concatenate([
                w[:, :c_q],
                _dup_heads(w[:, c_q:c_q + c_kv * HEAD_DIM], c_kv),
                _dup_heads(w[:, c_q + c_kv * HEAD_DIM:], c_kv)], axis=1).astype(BF16)
            gains = (jnp.tile(c_q_g[i], 2)[None], jnp.tile(c_k_g[i], 2)[None], ones_bd)
            widths = dict(n_u=0, n_q=c_q, n_kv=c_kv * PAIR)
            q, k2, v2 = _inproj(x, mods, latent_row, pre_g, w_in, rope_tabs=rope_tabs,
                                qk_gains=gains, **widths)
            qc, kc2, vc2 = _inproj(xc, mods, context_row, pre_g, w_in, qk_gains=gains, **widths)
            w_outs = [c_w_out[i].astype(BF16)]
            a = _flash_attention(q, jnp.concatenate([k2, kc2], axis=1),
                                 jnp.concatenate([v2, vc2], axis=1))
            x = _outproj([a], w_outs, x, mods, latent_row, post_g)
            if need_ctx:
                ac = _flash_attention(qc, kc2, vc2)
                xc = _outproj([ac], w_outs, xc, mods, context_row, post_g)

        ffn_args = (ffn_pre_g[l][None], ffn_post_g[l][None], ffn_w_up[l].astype(BF16),
                    ffn_conv_w[l], ffn_conv_b[l][None], ffn_w_down[l].astype(BF16))
        x = _conv_ffn(x, mods, latent_row, *ffn_args)
        if need_ctx:
            xc = _conv_ffn(xc, mods, context_row, *ffn_args)
    return x
```

```python
import functools

import jax
import jax.numpy as jnp
from jax import lax
from jax.experimental import pallas as pl
from jax.experimental.pallas import tpu as pltpu

F32 = jnp.float32
BF16 = jnp.bfloat16

HEAD_DIM = 64
GQA_GROUP = 4
POOL_WINDOWS = (2, 4, 8, 16)
POOL_GC = 128
WINDOW = 128
GRID_W = 64
ROPE_BASE = 10000.0
EPS = 1e-6
NEG = -1e30
N_MOD = 6

LANES = 128
SUBLANES = 8
VMEM_LIMIT_CAP = 56 * 1024 * 1024
VMEM_LIMIT_FLOOR = 32 * 1024 * 1024

PAIR = 2 * HEAD_DIM
GROUP_COLS = GQA_GROUP * HEAD_DIM
HALO = SUBLANES
KEY_CHUNK = 256


def _vmem_limit(block_bytes, scratch_bytes=0, temp_bytes=0):
    need = 2 * block_bytes + scratch_bytes + temp_bytes
    return int(min(max(need + need // 4, VMEM_LIMIT_FLOOR), VMEM_LIMIT_CAP))


def _nbytes(shape, dtype):
    n = 1
    for s in shape:
        n *= s
    return n * jnp.dtype(dtype).itemsize


def _row_tile(t, want):
    tile = min(t, want)
    assert t % tile == 0
    return tile


def _ada_kernel(c_ref, w_ref, b_ref, o_ref):
    c = c_ref[...]
    a = (c * jax.nn.sigmoid(c)).astype(BF16)
    o_ref[0] = jnp.dot(a, w_ref[0].astype(BF16), preferred_element_type=F32) + b_ref[0]


def _ada_params(cc, ada_w, ada_b):
    depth, d, n = ada_w.shape
    rows = cc.shape[0]
    tn = n // N_MOD
    blocks = _nbytes((rows, d), F32) + _nbytes((d, tn), F32) + _nbytes((rows, tn), F32)
    return pl.pallas_call(
        _ada_kernel,
        grid=(depth, n // tn),
        in_specs=[
            pl.BlockSpec((rows, d), lambda l, j: (0, 0)),
            pl.BlockSpec((1, d, tn), lambda l, j: (l, 0, j)),
            pl.BlockSpec((1, 1, tn), lambda l, j: (l, 0, j)),
        ],
        out_specs=pl.BlockSpec((1, rows, tn), lambda l, j: (l, 0, j)),
        out_shape=jax.ShapeDtypeStruct((depth, rows, n), F32),
        compiler_params=pltpu.CompilerParams(
            dimension_semantics=("parallel", "parallel"),
            vmem_limit_bytes=_vmem_limit(blocks, temp_bytes=_nbytes((d, tn), BF16))),
        name="ada_params",
    )(cc, ada_w, ada_b.reshape(depth, 1, n))


def _rms_norm(x, g):
    return x * lax.rsqrt(jnp.mean(x * x, axis=-1, keepdims=True) + EPS) * g


def _pre_mod(x, g, shift, scale):
    return _rms_norm(x, g) * (1.0 + scale) + shift


def _inproj_kernel(*refs, n_u, n_q, n_kv, n_v, v_rows, v_chunks, v_dup, rope, qk_norm, tm):
    x_ref, sh_ref, sc_ref, g_ref, w_ref = refs[:5]
    pos = 5
    if rope:
        cos_ref, sin_ref = refs[pos:pos + 2]
        pos += 2
    if qk_norm:
        qg_ref, kg_ref, ones_ref = refs[pos:pos + 3]
        pos += 3
    out_refs = refs[pos:]

    h = _pre_mod(x_ref[0], g_ref[...], sh_ref[0], sc_ref[0]).astype(BF16)

    if rope:
        lane = lax.broadcasted_iota(jnp.int32, (tm, PAIR), 1)
        first_half = (lane & (HEAD_DIM - 1)) < HEAD_DIM // 2
        cos = cos_ref[...]
        sin = sin_ref[...]

    def head_epilogue(a, gain_ref, scale):
        if qk_norm:
            a2 = a * a
            hi = a2.astype(BF16)
            lo = (a2 - hi.astype(F32)).astype(BF16)
            ss = (jnp.dot(hi, ones_ref[...], preferred_element_type=F32)
                  + jnp.dot(lo, ones_ref[...], preferred_element_type=F32))
            a = a * lax.rsqrt(ss * (1.0 / HEAD_DIM) + EPS) * gain_ref[...]
        if rope:
            partner = jnp.where(first_half,
                                pltpu.roll(a, PAIR - HEAD_DIM // 2, 1),
                                pltpu.roll(a, HEAD_DIM // 2, 1))
            a = a * cos + partner * sin
        if scale != 1.0:
            a = a * scale
        return a.astype(BF16)

    col = 0
    out_idx = 0
    if n_u:
        out_refs[out_idx][0] = jnp.dot(h, w_ref[:, col:col + n_u], preferred_element_type=F32)
        col += n_u
        out_idx += 1

    acc = jnp.dot(h, w_ref[:, col:col + n_q], preferred_element_type=F32)
    for c in range(n_q // PAIR):
        out_refs[out_idx][0, :, c * PAIR:(c + 1) * PAIR] = head_epilogue(
            acc[:, c * PAIR:(c + 1) * PAIR], qg_ref if qk_norm else None, HEAD_DIM ** -0.5)
    col += n_q
    out_idx += 1

    acc = jnp.dot(h, w_ref[:, col:col + n_kv], preferred_element_type=F32)
    for c in range(n_kv // PAIR):
        out_refs[out_idx][0, :, c * PAIR:(c + 1) * PAIR] = head_epilogue(
            acc[:, c * PAIR:(c + 1) * PAIR], kg_ref if qk_norm else None, 1.0)
    col += n_kv
    out_idx += 1

    acc = jnp.dot(h, w_ref[:, col:col + n_v], preferred_element_type=F32)
    if v_rows:
        out_refs[out_idx][0] = acc.astype(BF16)
        out_idx += 1
    if v_chunks:
        for cc in range(tm // KEY_CHUNK):
            at = acc[cc * KEY_CHUNK:(cc + 1) * KEY_CHUNK, :].T
            if v_dup:
                at = jnp.concatenate([at[j * PAIR:j * PAIR + HEAD_DIM]
                                      for j in range(n_v // PAIR)], axis=0)
            out_refs[out_idx][0, cc] = at.astype(BF16)


def _inproj(x, mods, mod_row, pre_g, w, *, n_u, n_q, n_kv, n_v, v_rows, v_chunks,
            rope_tabs=None, qk_gains=None):
    b, t, d = x.shape
    n = w.shape[1]
    tm = _row_tile(t, 512)
    rope = rope_tabs is not None
    qk_norm = qk_gains is not None
    v_dup = v_rows
    vt_rows = n_v // 2 if v_dup else n_v
    assert tm % KEY_CHUNK == 0

    in_specs = [
        pl.BlockSpec((1, tm, d), lambda bi, i: (bi, i, 0)),
        pl.BlockSpec((1, 1, d), lambda bi, i: (mod_row(bi), 0, 0)),
        pl.BlockSpec((1, 1, d), lambda bi, i: (mod_row(bi), 0, 1)),
        pl.BlockSpec((1, d), lambda bi, i: (0, 0)),
        pl.BlockSpec((d, n), lambda bi, i: (0, 0)),
    ]
    args = [x, mods, mods, pre_g, w]
    if rope:
        in_specs += [pl.BlockSpec((tm, PAIR), lambda bi, i: (i, 0))] * 2
        args += list(rope_tabs)
    if qk_norm:
        in_specs += [pl.BlockSpec((1, PAIR), lambda bi, i: (0, 0))] * 2
        in_specs += [pl.BlockSpec((PAIR, PAIR), lambda bi, i: (0, 0))]
        args += list(qk_gains)

    out_shape, out_specs = [], []
    for width, dtype in ((n_u, F32), (n_q, BF16), (n_kv, BF16), (n_v if v_rows else 0, BF16)):
        if width:
            out_shape.append(jax.ShapeDtypeStruct((b, t, width), dtype))
            out_specs.append(pl.BlockSpec((1, tm, width), lambda bi, i: (bi, i, 0)))
    if v_chunks:
        out_shape.append(jax.ShapeDtypeStruct((b, t // KEY_CHUNK, vt_rows, KEY_CHUNK), BF16))
        out_specs.append(pl.BlockSpec((1, tm // KEY_CHUNK, vt_rows, KEY_CHUNK),
                                      lambda bi, i: (bi, i, 0, 0)))

    blocks = (_nbytes((tm, d), F32) + _nbytes((d, n), BF16) + _nbytes((tm, n_u), F32)
              + _nbytes((tm, n_q + n_kv + 2 * n_v), BF16) + 2 * _nbytes((tm, PAIR), F32))
    temps = _nbytes((tm, d), F32) * 2 + _nbytes((tm, max(n_q, n_kv, n_u)), F32) * 2
    return pl.pallas_call(
        functools.partial(_inproj_kernel, n_u=n_u, n_q=n_q, n_kv=n_kv, n_v=n_v, v_rows=v_rows,
                          v_chunks=v_chunks, v_dup=v_dup, rope=rope, qk_norm=qk_norm, tm=tm),
        grid=(b, t // tm),
        in_specs=in_specs,
        out_specs=out_specs,
        out_shape=out_shape,
        compiler_params=pltpu.CompilerParams(
            dimension_semantics=("parallel", "parallel"),
            vmem_limit_bytes=_vmem_limit(blocks, temp_bytes=temps)),
        name="mixer_inproj",
    )(*args)


def _stack_heads(q_ref, qs_ref, tq):
    lane = lax.broadcasted_iota(jnp.int32, (tq, PAIR), 1)
    low = lane < HEAD_DIM
    for p in range(GROUP_COLS // PAIR):
        qp = q_ref[0, :, p * PAIR:(p + 1) * PAIR].astype(F32)
        qs_ref[(2 * p) * tq:(2 * p + 1) * tq, :] = jnp.where(low, qp, 0.0).astype(BF16)
        qs_ref[(2 * p + 1) * tq:(2 * p + 2) * tq, :] = jnp.where(low, 0.0, qp).astype(BF16)


def _unstack_heads(o, o_ref, tq):
    lane = lax.broadcasted_iota(jnp.int32, (tq, PAIR), 1)
    low = lane < HEAD_DIM
    for p in range(GROUP_COLS // PAIR):
        even = o[(2 * p) * tq:(2 * p + 1) * tq, :]
        odd = o[(2 * p + 1) * tq:(2 * p + 2) * tq, :]
        o_ref[0, :, p * PAIR:(p + 1) * PAIR] = jnp.where(low, even, odd).astype(o_ref.dtype)


def _scores(qs, k):
    return lax.dot_general(qs, k, (((1,), (1,)), ((), ())), preferred_element_type=F32)


def _flash_kernel(*refs, tq, n_chunks, has_sink):
    if has_sink:
        q_ref, k_ref, vt_ref, sink_ref, o_ref, qt_ref, s_ref, m_ref, l_ref, acc_ref = refs
    else:
        q_ref, k_ref, vt_ref, o_ref, qt_ref, s_ref, m_ref, l_ref, acc_ref = refs

    sub = lax.broadcasted_iota(jnp.int32, (PAIR, tq), 0)
    top = sub < HEAD_DIM
    for p in range(GROUP_COLS // PAIR):
        pair_t = q_ref[0, :, p * PAIR:(p + 1) * PAIR].astype(F32).T
        qt_ref[:, (2 * p) * tq:(2 * p + 1) * tq] = jnp.where(top, pair_t, 0.0).astype(BF16)
        qt_ref[:, (2 * p + 1) * tq:(2 * p + 2) * tq] = jnp.where(top, 0.0, pair_t).astype(BF16)
    if has_sink:
        m_ref[...] = sink_ref[0]
        l_ref[...] = jnp.ones_like(l_ref)
    else:
        m_ref[...] = jnp.full_like(m_ref, NEG)
        l_ref[...] = jnp.zeros_like(l_ref)
    acc_ref[...] = jnp.zeros_like(acc_ref)

    def scores(c):
        start = pl.multiple_of(c * KEY_CHUNK, KEY_CHUNK)
        return jnp.dot(k_ref[0, pl.ds(start, KEY_CHUNK), :], qt_ref[...],
                       preferred_element_type=F32)

    def consume(slot, c):
        s = s_ref[slot]
        m_prev = m_ref[...]
        m_new = jnp.maximum(m_prev, jnp.max(s, axis=0, keepdims=True))
        alpha = jnp.exp(m_prev - m_new)
        p = jnp.exp(s - m_new)
        l_ref[...] = alpha * l_ref[...] + jnp.sum(p, axis=0, keepdims=True)
        acc_ref[...] = alpha * acc_ref[...] + jnp.dot(vt_ref[0, c], p.astype(BF16),
                                                     preferred_element_type=F32)
        m_ref[...] = m_new

    s_ref[0] = scores(0)

    def body(pair, carry):
        c = 2 * pair
        s_ref[1] = scores(c + 1)
        consume(0, c)
        s_ref[0] = scores(c + 2)
        consume(1, c + 1)
        return carry
    lax.fori_loop(0, n_chunks // 2, body, 0)
    consume(0, n_chunks - 1)

    o_t = acc_ref[...] / l_ref[...]
    for p in range(GROUP_COLS // PAIR):
        pair_t = jnp.concatenate([o_t[:, (2 * p) * tq:(2 * p + 1) * tq],
                                  o_t[:, (2 * p + 1) * tq:(2 * p + 2) * tq]], axis=0)
        o_ref[0, :, p * PAIR:(p + 1) * PAIR] = pair_t.T.astype(o_ref.dtype)


def _flash_attention(q, k2, vt, sink_row=None, *, tq_want=256):
    b, t, hq = q.shape
    n_chunks = vt.shape[1]
    n_groups = hq // GROUP_COLS
    tq = _row_tile(t, tq_want)
    cols = GQA_GROUP * tq
    has_sink = sink_row is not None
    assert n_chunks % 2 == 1 and k2.shape[1] == n_chunks * KEY_CHUNK

    in_specs = [
        pl.BlockSpec((1, tq, GROUP_COLS), lambda bi, j, i: (bi, i, j)),
        pl.BlockSpec((1, n_chunks * KEY_CHUNK, PAIR), lambda bi, j, i: (bi, 0, j)),
        pl.BlockSpec((1, n_chunks, HEAD_DIM, KEY_CHUNK), lambda bi, j, i: (bi, 0, j, 0)),
    ]
    args = [q, k2, vt]
    if has_sink:
        in_specs.append(pl.BlockSpec((1, 1, cols), lambda bi, j, i: (j, 0, 0)))
        args.append(sink_row)
    blocks = (2 * _nbytes((tq, GROUP_COLS), BF16)
              + _nbytes((n_chunks * KEY_CHUNK, PAIR + HEAD_DIM), BF16))
    scratch = (_nbytes((PAIR, cols), BF16) + _nbytes((2, KEY_CHUNK, cols), F32)
               + _nbytes((HEAD_DIM + 2 * SUBLANES, cols), F32))
    temps = 2 * _nbytes((KEY_CHUNK, cols), F32)
    return pl.pallas_call(
        functools.partial(_flash_kernel, tq=tq, n_chunks=n_chunks, has_sink=has_sink),
        grid=(b, n_groups, t // tq),
        in_specs=in_specs,
        out_specs=pl.BlockSpec((1, tq, GROUP_COLS), lambda bi, j, i: (bi, i, j)),
        out_shape=jax.ShapeDtypeStruct((b, t, hq), BF16),
        scratch_shapes=[
            pltpu.VMEM((PAIR, cols), BF16),
            pltpu.VMEM((2, KEY_CHUNK, cols), F32),
            pltpu.VMEM((1, cols), F32),
            pltpu.VMEM((1, cols), F32),
            pltpu.VMEM((HEAD_DIM, cols), F32),
        ],
        compiler_params=pltpu.CompilerParams(
            dimension_semantics=("parallel", "parallel", "parallel"),
            vmem_limit_bytes=_vmem_limit(blocks, scratch, temps)),
        name="flash_attention",
    )(*args)


def _window_kernel(q_ref, k_ref, v_ref, kc_ref, vc_ref, sink_ref, o_ref, qs_ref, *, tq, t):
    span = tq + 2 * WINDOW
    q0 = pl.program_id(2) * tq
    start = pl.multiple_of(jnp.clip(q0 - WINDOW, 0, t - span), WINDOW)
    _stack_heads(q_ref, qs_ref, tq)
    qs = qs_ref[...]

    s_band = _scores(qs, k_ref[0, pl.ds(start, span), :])
    row = lax.broadcasted_iota(jnp.int32, s_band.shape, 0)
    col = lax.broadcasted_iota(jnp.int32, s_band.shape, 1)
    delta = (q0 + (row & (tq - 1))) - (start + col)
    s_band = jnp.where((delta <= WINDOW) & (delta >= -WINDOW), s_band, NEG)
    s_ctx = _scores(qs, kc_ref[0])

    sink = sink_ref[0]
    m = jnp.maximum(jnp.maximum(jnp.max(s_band, axis=1, keepdims=True),
                                jnp.max(s_ctx, axis=1, keepdims=True)), sink)
    e_band = jnp.exp(s_band - m)
    e_ctx = jnp.exp(s_ctx - m)
    den = (jnp.sum(e_band, axis=1, keepdims=True) + jnp.sum(e_ctx, axis=1, keepdims=True)
           + jnp.exp(sink - m))
    acc = (jnp.dot(e_band.astype(BF16), v_ref[0, pl.ds(start, span), :],
                   preferred_element_type=F32)
           + jnp.dot(e_ctx.astype(BF16), vc_ref[0], preferred_element_type=F32))
    _unstack_heads(acc / den, o_ref, tq)


def _window_attention(q, k2, v2, kc2, vc2, sink_col, *, tq):
    b, t, hq = q.shape
    n_ctx = kc2.shape[1]
    n_groups = hq // GROUP_COLS
    rows = GQA_GROUP * tq
    span = tq + 2 * WINDOW
    assert t % tq == 0 and t >= span and tq & (tq - 1) == 0
    blocks = (2 * _nbytes((tq, GROUP_COLS), BF16) + 2 * _nbytes((t, PAIR), BF16)
              + 2 * _nbytes((n_ctx, PAIR), BF16))
    temps = 4 * _nbytes((rows, span + n_ctx), F32)
    return pl.pallas_call(
        functools.partial(_window_kernel, tq=tq, t=t),
        grid=(b, n_groups, t // tq),
        in_specs=[
            pl.BlockSpec((1, tq, GROUP_COLS), lambda bi, j, i: (bi, i, j)),
            pl.BlockSpec((1, t, PAIR), lambda bi, j, i: (bi, 0, j)),
            pl.BlockSpec((1, t, PAIR), lambda bi, j, i: (bi, 0, j)),
            pl.BlockSpec((1, n_ctx, PAIR), lambda bi, j, i: (bi, 0, j)),
            pl.BlockSpec((1, n_ctx, PAIR), lambda bi, j, i: (bi, 0, j)),
            pl.BlockSpec((1, rows, 1), lambda bi, j, i: (j, 0, 0)),
        ],
        out_specs=pl.BlockSpec((1, tq, GROUP_COLS), lambda bi, j, i: (bi, i, j)),
        out_shape=jax.ShapeDtypeStruct((b, t, hq), BF16),
        scratch_shapes=[pltpu.VMEM((rows, PAIR), BF16)],
        compiler_params=pltpu.CompilerParams(
            dimension_semantics=("parallel", "parallel", "parallel"),
            vmem_limit_bytes=_vmem_limit(blocks, _nbytes((rows, PAIR), BF16), temps)),
        name="window_attention",
    )(q, k2, v2, kc2, vc2, sink_col)


def _pool_kernel(u_ref, w_ref, scale_ref, o_ref, pad_ref, *, t):
    reach = max(POOL_WINDOWS) // 2
    assert reach <= HALO
    pos = lax.broadcasted_iota(jnp.int32, (t, POOL_GC), 0)
    zeros = jnp.zeros((HALO, POOL_GC), F32)
    pad_ref[0:HALO, :] = zeros
    pad_ref[HALO + t:2 * HALO + t, :] = zeros
    for g, win in enumerate(POOL_WINDOWS):
        half = win // 2
        u = u_ref[0, :, g * POOL_GC:(g + 1) * POOL_GC]
        pad_ref[HALO:HALO + t, :] = u
        total = pad_ref[pl.ds(HALO - half, t), :]
        for off in range(1 - half, half):
            total = total + pad_ref[pl.ds(HALO + off, t), :]
        cnt = (jnp.minimum(pos + half, t) - jnp.maximum(pos - half, 0)).astype(F32)
        d = (total / cnt - u).astype(BF16)
        y = jnp.dot(d, w_ref[g], preferred_element_type=F32)
        o_ref[0, :, g * POOL_GC:(g + 1) * POOL_GC] = (
            y * scale_ref[:, g * POOL_GC:(g + 1) * POOL_GC]).astype(BF16)


def _pool_mix(u, pool_w, pool_scale):
    b, t, ch = u.shape
    n_g = len(POOL_WINDOWS)
    assert ch == n_g * POOL_GC
    blocks = _nbytes((t, ch), F32) + _nbytes((t, ch), BF16) + _nbytes((n_g, POOL_GC, POOL_GC), BF16)
    scratch = _nbytes((t + 2 * HALO, POOL_GC), F32)
    return pl.pallas_call(
        functools.partial(_pool_kernel, t=t),
        grid=(b,),
        in_specs=[
            pl.BlockSpec((1, t, ch), lambda bi: (bi, 0, 0)),
            pl.BlockSpec((n_g, POOL_GC, POOL_GC), lambda bi: (0, 0, 0)),
            pl.BlockSpec((1, ch), lambda bi: (0, 0)),
        ],
        out_specs=pl.BlockSpec((1, t, ch), lambda bi: (bi, 0, 0)),
        out_shape=jax.ShapeDtypeStruct((b, t, ch), BF16),
        scratch_shapes=[pltpu.VMEM((t + 2 * HALO, POOL_GC), F32)],
        compiler_params=pltpu.CompilerParams(
            dimension_semantics=("parallel",),
            vmem_limit_bytes=_vmem_limit(blocks, scratch, 6 * _nbytes((t, POOL_GC), F32))),
        name="pool_mix",
    )(u, pool_w, pool_scale)


def _outproj_kernel(*refs, n_in):
    a_refs = refs[:n_in]
    w_refs = refs[n_in:2 * n_in]
    x_ref, gate_ref, g_ref, o_ref = refs[2 * n_in:]
    y = jnp.dot(a_refs[0][0], w_refs[0][...], preferred_element_type=F32)
    for a_ref, w_ref in zip(a_refs[1:], w_refs[1:]):
        y = y + jnp.dot(a_ref[0], w_ref[...], preferred_element_type=F32)
    o_ref[0] = x_ref[0] + gate_ref[0] * _rms_norm(y, g_ref[...])


def _outproj(acts, weights, x, mods, mod_row, post_g):
    b, t, d = x.shape
    tm = _row_tile(t, 512)
    n_in = len(acts)
    in_specs = [pl.BlockSpec((1, tm, a.shape[2]), lambda bi, i: (bi, i, 0)) for a in acts]
    in_specs += [pl.BlockSpec(w.shape, lambda bi, i: (0, 0)) for w in weights]
    in_specs += [
        pl.BlockSpec((1, tm, d), lambda bi, i: (bi, i, 0)),
        pl.BlockSpec((1, 1, d), lambda bi, i: (mod_row(bi), 0, 2)),
        pl.BlockSpec((1, d), lambda bi, i: (0, 0)),
    ]
    blocks = (sum(_nbytes((tm, a.shape[2]), BF16) for a in acts)
              + sum(_nbytes(w.shape, BF16) for w in weights) + 2 * _nbytes((tm, d), F32))
    return pl.pallas_call(
        functools.partial(_outproj_kernel, n_in=n_in),
        grid=(b, t // tm),
        in_specs=in_specs,
        out_specs=pl.BlockSpec((1, tm, d), lambda bi, i: (bi, i, 0)),
        out_shape=jax.ShapeDtypeStruct((b, t, d), F32),
        compiler_params=pltpu.CompilerParams(
            dimension_semantics=("parallel", "parallel"),
            vmem_limit_bytes=_vmem_limit(blocks, temp_bytes=3 * _nbytes((tm, d), F32))),
        name="mixer_outproj",
    )(*acts, *weights, x, mods, post_g)


def _ffn_kernel(x_ref, xp_ref, xn_ref, sh_ref, sc_ref, gate_ref, pre_g_ref, post_g_ref,
                wa_ref, wb_ref, cwa_ref, cwb_ref, cba_ref, cbb_ref, wd_ref, o_ref,
                h_ref, ua_ref, ub_ref, acc_ref, *, tm):
    i = pl.program_id(1)
    j = pl.program_id(2)

    @pl.when(j == 0)
    def _():
        g, sh, sc = pre_g_ref[...], sh_ref[0], sc_ref[0]
        h_prev = _pre_mod(xp_ref[0], g, sh, sc)
        h_next = _pre_mod(xn_ref[0], g, sh, sc)
        h_prev = jnp.where(i == 0, 0.0, h_prev)
        h_next = jnp.where(i == pl.num_programs(1) - 1, 0.0, h_next)
        h_ref[0:HALO, :] = h_prev.astype(BF16)
        h_ref[HALO:HALO + tm, :] = _pre_mod(x_ref[0], g, sh, sc).astype(BF16)
        h_ref[HALO + tm:2 * HALO + tm, :] = h_next.astype(BF16)
        acc_ref[...] = jnp.zeros_like(acc_ref)

    h = h_ref[...]
    ua_ref[...] = jnp.dot(h, wa_ref[...], preferred_element_type=F32)
    ub_ref[...] = jnp.dot(h, wb_ref[...], preferred_element_type=F32)

    def conv(u_ref, cw_ref, cb_ref):
        return (u_ref[pl.ds(HALO - 1, tm), :] * cw_ref[0:1, :]
                + u_ref[pl.ds(HALO, tm), :] * cw_ref[1:2, :]
                + u_ref[pl.ds(HALO + 1, tm), :] * cw_ref[2:3, :]
                + cb_ref[...])

    gated = (jax.nn.gelu(conv(ua_ref, cwa_ref, cba_ref)) * conv(ub_ref, cwb_ref, cbb_ref))
    acc_ref[...] += jnp.dot(gated.astype(BF16), wd_ref[...], preferred_element_type=F32)

    @pl.when(j == pl.num_programs(2) - 1)
    def _():
        o_ref[0] = x_ref[0] + gate_ref[0] * _rms_norm(acc_ref[...], post_g_ref[...])


def _conv_ffn(x, mods, mod_row, pre_g, post_g, w_up, conv_w, conv_b, w_down, *, tm_want=512,
              fc=384):
    b, t, d = x.shape
    d_ff = w_down.shape[0]
    tm = _row_tile(t, tm_want)
    assert d_ff % fc == 0 and tm % HALO == 0
    nj = d_ff // fc
    halo_blocks = tm // HALO
    last_halo = t // HALO - 1
    rows = tm + 2 * HALO

    in_specs = [
        pl.BlockSpec((1, tm, d), lambda bi, i, j: (bi, i, 0)),
        pl.BlockSpec((1, HALO, d), lambda bi, i, j: (bi, jnp.maximum(i * halo_blocks - 1, 0), 0)),
        pl.BlockSpec((1, HALO, d),
                     lambda bi, i, j: (bi, jnp.minimum((i + 1) * halo_blocks, last_halo), 0)),
        pl.BlockSpec((1, 1, d), lambda bi, i, j: (mod_row(bi), 0, 3)),
        pl.BlockSpec((1, 1, d), lambda bi, i, j: (mod_row(bi), 0, 4)),
        pl.BlockSpec((1, 1, d), lambda bi, i, j: (mod_row(bi), 0, 5)),
        pl.BlockSpec((1, d), lambda bi, i, j: (0, 0)),
        pl.BlockSpec((1, d), lambda bi, i, j: (0, 0)),
        pl.BlockSpec((d, fc), lambda bi, i, j: (0, j)),
        pl.BlockSpec((d, fc), lambda bi, i, j: (0, nj + j)),
        pl.BlockSpec((3, fc), lambda bi, i, j: (0, j)),
        pl.BlockSpec((3, fc), lambda bi, i, j: (0, nj + j)),
        pl.BlockSpec((1, fc), lambda bi, i, j: (0, j)),
        pl.BlockSpec((1, fc), lambda bi, i, j: (0, nj + j)),
        pl.BlockSpec((fc, d), lambda bi, i, j: (j, 0)),
    ]
    blocks = (2 * _nbytes((tm, d), F32) + 2 * _nbytes((HALO, d), F32)
              + 3 * _nbytes((d, fc), BF16))
    scratch = (_nbytes((rows, d), BF16) + 2 * _nbytes((rows, fc), F32) + _nbytes((tm, d), F32))
    temps = 4 * _nbytes((rows, fc), F32) + _nbytes((tm, d), F32)
    return pl.pallas_call(
        functools.partial(_ffn_kernel, tm=tm),
        grid=(b, t // tm, nj),
        in_specs=in_specs,
        out_specs=pl.BlockSpec((1, tm, d), lambda bi, i, j: (bi, i, 0)),
        out_shape=jax.ShapeDtypeStruct((b, t, d), F32),
        scratch_shapes=[
            pltpu.VMEM((rows, d), BF16),
            pltpu.VMEM((rows, fc), F32),
            pltpu.VMEM((rows, fc), F32),
            pltpu.VMEM((tm, d), F32),
        ],
        compiler_params=pltpu.CompilerParams(
            dimension_semantics=("parallel", "parallel", "arbitrary"),
            vmem_limit_bytes=_vmem_limit(blocks, scratch, temps)),
        name="conv_ffn",
    )(x, x, x, mods, mods, mods, pre_g, post_g, w_up, w_up, conv_w, conv_w, conv_b, conv_b, w_down)


def _rope_tables(t):
    rows = t // GRID_W
    row = jnp.repeat(jnp.arange(rows), GRID_W).astype(F32)
    col = jnp.tile(jnp.arange(GRID_W), rows).astype(F32)
    n_freq = HEAD_DIM // 4
    freqs = ROPE_BASE ** (-jnp.arange(n_freq, dtype=F32) / n_freq)
    ang = jnp.concatenate([row[:, None] * freqs, col[:, None] * freqs], axis=-1)
    cos, sin = jnp.cos(ang), jnp.sin(ang)
    return (jnp.concatenate([cos, cos, cos, cos], axis=-1),
            jnp.concatenate([-sin, sin, -sin, sin], axis=-1))


def _dup_heads(w, n_heads):
    d = w.shape[0]
    w = w.reshape(d, n_heads, 1, HEAD_DIM)
    return jnp.broadcast_to(w, (d, n_heads, 2, HEAD_DIM)).reshape(d, n_heads * PAIR)


def _sink_column(sink, tq):
    return jnp.repeat(sink.astype(F32), tq, axis=1)[..., None]


def _sink_row(sink, tq):
    return jnp.repeat(sink.astype(F32), tq, axis=1)[:, None, :]


def kernel(x, c, ctx, c_ctx, ada_w, ada_b, mix_pre_g, mix_post_g, ffn_pre_g, ffn_post_g,
           ab_w_in, ab_w_out, pool_w, pool_scale, sink_logit,
           c_w_qkv, c_w_out, c_q_g, c_k_g,
           ffn_w_up, ffn_conv_w, ffn_conv_b, ffn_w_down):
    batch, t, d = x.shape
    n_ctx = ctx.shape[1]
    depth = ada_w.shape[0]
    pool_ch = pool_scale.shape[1]
    b_kv = sink_logit.shape[1]
    b_q = b_kv * GQA_GROUP * HEAD_DIM
    c_q = c_w_out.shape[1]
    c_kv = c_q // HEAD_DIM // GQA_GROUP

    ctx_row = batch
    n_rows = -(-(batch + 1) // SUBLANES) * SUBLANES
    cc = jnp.zeros((n_rows, d), F32).at[:batch].set(c).at[ctx_row].set(c_ctx)
    mods_all = _ada_params(cc, ada_w, ada_b).reshape(depth, n_rows, 1, N_MOD * d)

    rope_tabs = _rope_tables(t)
    ones_bd = jnp.kron(jnp.eye(2, dtype=F32), jnp.ones((HEAD_DIM, HEAD_DIM), F32)).astype(BF16)
    latent_row = lambda bi: bi
    context_row = lambda bi: ctx_row
    win_tq = min(128, t)

    xc = ctx
    for l in range(depth):
        need_ctx = l < depth - 1
        i = l // 2
        mods = mods_all[l]
        pre_g, post_g = mix_pre_g[l][None], mix_post_g[l][None]
        if l % 2 == 0:
            w = ab_w_in[i]
            o = pool_ch
            w_in = jnp.concatenate([
                w[:, :o + b_q],
                _dup_heads(w[:, o + b_q:o + b_q + b_kv * HEAD_DIM], b_kv),
                _dup_heads(w[:, o + b_q + b_kv * HEAD_DIM:], b_kv)], axis=1).astype(BF16)
            widths = dict(n_u=pool_ch, n_q=b_q, n_kv=b_kv * PAIR, n_v=b_kv * PAIR, v_rows=True)
            u, q, k2, v2 = _inproj(x, mods, latent_row, pre_g, w_in, rope_tabs=rope_tabs,
                                   v_chunks=False, **widths)
            uc, qc, kc2, vc2, *vtc = _inproj(xc, mods, context_row, pre_g, w_in,
                                             v_chunks=need_ctx, **widths)
            w_out = ab_w_out[i].astype(BF16)
            w_outs = [w_out[:pool_ch], w_out[pool_ch:]]
            pw, ps = pool_w[i].astype(BF16), pool_scale[i][None]
            a = _window_attention(q, k2, v2, kc2, vc2, _sink_column(sink_logit[i], win_tq),
                                  tq=win_tq)
            x = _outproj([_pool_mix(u, pw, ps), a], w_outs, x, mods, latent_row, post_g)
            if need_ctx:
                tqc = _row_tile(n_ctx, 256)
                ac = _flash_attention(qc, kc2, vtc[0], _sink_row(sink_logit[i], tqc), tq_want=tqc)
                xc = _outproj([_pool_mix(uc, pw, ps), ac], w_outs, xc, mods, context_row, post_g)
        else:
            w = c_w_qkv[i]
            w_in = jnp.concatenate([
                w[:, :c_q],
                _dup_heads(w[:, c_q:c_q + c_kv * HEAD_DIM], c_kv),
                w[:, c_q + c_kv * HEAD_DIM:]], axis=1).astype(BF16)
            gains = (jnp.tile(c_q_g[i], 2)[None], jnp.tile(c_k_g[i], 2)[None], ones_bd)
            widths = dict(n_u=0, n_q=c_q, n_kv=c_kv * PAIR, n_v=c_kv * HEAD_DIM, v_rows=False,
                          v_chunks=True)
            q, k2, vt = _inproj(x, mods, latent_row, pre_g, w_in, rope_tabs=rope_tabs,
                                qk_gains=gains, **widths)
            qc, kc2, vtc = _inproj(xc, mods, context_row, pre_g, w_in, qk_gains=gains, **widths)
            w_outs = [c_w_out[i].astype(BF16)]
            a = _flash_attention(q, jnp.concatenate([k2, kc2], axis=1),
                                 jnp.concatenate([vt, vtc], axis=1))
            x = _outproj([a], w_outs, x, mods, latent_row, post_g)
            if need_ctx:
                ac = _flash_attention(qc, kc2, vtc)
                xc = _outproj([ac], w_outs, xc, mods, context_row, post_g)

        ffn_args = (ffn_pre_g[l][None], ffn_post_g[l][None], ffn_w_up[l].astype(BF16),
                    ffn_conv_w[l], ffn_conv_b[l][None], ffn_w_down[l].astype(BF16))
        x = _conv_ffn(x, mods, latent_row, *ffn_args)
        if need_ctx:
            xc = _conv_ffn(xc, mods, context_row, *ffn_args)
    return x
```

```python
import functools

import jax
import jax.numpy as jnp
from jax import lax
from jax.experimental import pallas as pl
from jax.experimental.pallas import tpu as pltpu

F32 = jnp.float32
BF16 = jnp.bfloat16

HEAD_DIM = 64
GQA_GROUP = 4
POOL_WINDOWS = (2, 4, 8, 16)
POOL_GC = 128
WINDOW = 128
GRID_W = 64
ROPE_BASE = 10000.0
EPS = 1e-6
NEG = -1e30
N_MOD = 6

LANES = 128
SUBLANES = 8
VMEM_LIMIT_CAP = 56 * 1024 * 1024
VMEM_LIMIT_FLOOR = 32 * 1024 * 1024

PAIR = 2 * HEAD_DIM
GROUP_COLS = GQA_GROUP * HEAD_DIM
HALO = SUBLANES
KEY_CHUNK = 256
LOG2E = 1.4426950408889634
Q_SCALE = HEAD_DIM ** -0.5 * LOG2E
ONES_ROWS = 16
VT_ROWS = HEAD_DIM + ONES_ROWS
FFN_CHUNK = 896

def _vmem_limit(block_bytes, scratch_bytes=0, temp_bytes=0):
    need = 2 * block_bytes + scratch_bytes + temp_bytes
    return int(min(max(need + need // 4, VMEM_LIMIT_FLOOR), VMEM_LIMIT_CAP))


def _nbytes(shape, dtype):
    n = 1
    for s in shape:
        n *= s
    return n * jnp.dtype(dtype).itemsize


def _row_tile(t, want):
    tile = min(t, want)
    assert t % tile == 0
    return tile


def _ada_kernel(c_ref, w_ref, b_ref, o_ref):
    c = c_ref[...]
    a = (c * jax.nn.sigmoid(c)).astype(BF16)
    o_ref[0] = jnp.dot(a, w_ref[0].astype(BF16), preferred_element_type=F32) + b_ref[0]


def _ada_params(cc, ada_w, ada_b):
    depth, d, n = ada_w.shape
    rows = cc.shape[0]
    tn = n // N_MOD
    blocks = _nbytes((rows, d), F32) + _nbytes((d, tn), F32) + _nbytes((rows, tn), F32)
    return pl.pallas_call(
        _ada_kernel,
        grid=(depth, n // tn),
        in_specs=[
            pl.BlockSpec((rows, d), lambda l, j: (0, 0)),
            pl.BlockSpec((1, d, tn), lambda l, j: (l, 0, j)),
            pl.BlockSpec((1, 1, tn), lambda l, j: (l, 0, j)),
        ],
        out_specs=pl.BlockSpec((1, rows, tn), lambda l, j: (l, 0, j)),
        out_shape=jax.ShapeDtypeStruct((depth, rows, n), F32),
        compiler_params=pltpu.CompilerParams(
            dimension_semantics=("parallel", "parallel"),
            vmem_limit_bytes=_vmem_limit(blocks, temp_bytes=_nbytes((d, tn), BF16))),
        name="ada_params",
    )(cc, ada_w, ada_b.reshape(depth, 1, n))


def _rms_norm(x, g):
    return x * lax.rsqrt(jnp.mean(x * x, axis=-1, keepdims=True) + EPS) * g


def _pre_mod(x, g, shift, scale):
    return _rms_norm(x, g) * (1.0 + scale) + shift


def _inproj_kernel(*refs, n_u, n_q, n_kv, n_v, v_rows, v_chunks, v_dup, rope, qk_norm, tm):
    x_ref, sh_ref, sc_ref, g_ref, w_ref = refs[:5]
    pos = 5
    if rope:
        cos_ref, sin_ref = refs[pos:pos + 2]
        pos += 2
    if qk_norm:
        qg_ref, kg_ref, ones_ref = refs[pos:pos + 3]
        pos += 3
    out_refs = refs[pos:]

    h = _pre_mod(x_ref[0], g_ref[...], sh_ref[0], sc_ref[0]).astype(BF16)

    if rope:
        lane = lax.broadcasted_iota(jnp.int32, (tm, PAIR), 1)
        first_half = (lane & (HEAD_DIM - 1)) < HEAD_DIM // 2
        cos = cos_ref[...]
        sin = sin_ref[...]

    def head_epilogue(a, gain_ref, scale):
        if qk_norm:
            a2 = a * a
            hi = a2.astype(BF16)
            lo = (a2 - hi.astype(F32)).astype(BF16)
            ss = (jnp.dot(hi, ones_ref[...], preferred_element_type=F32)
                  + jnp.dot(lo, ones_ref[...], preferred_element_type=F32))
            a = a * lax.rsqrt(ss * (1.0 / HEAD_DIM) + EPS) * gain_ref[...]
        if rope:
            partner = jnp.where(first_half,
                                pltpu.roll(a, PAIR - HEAD_DIM // 2, 1),
                                pltpu.roll(a, HEAD_DIM // 2, 1))
            a = a * cos + partner * sin
        if scale != 1.0:
            a = a * scale
        return a.astype(BF16)

    col = 0
    out_idx = 0
    if n_u:
        out_refs[out_idx][0] = jnp.dot(h, w_ref[:, col:col + n_u], preferred_element_type=F32)
        col += n_u
        out_idx += 1

    acc = jnp.dot(h, w_ref[:, col:col + n_q], preferred_element_type=F32)
    for c in range(n_q // PAIR):
        out_refs[out_idx][0, :, c * PAIR:(c + 1) * PAIR] = head_epilogue(
            acc[:, c * PAIR:(c + 1) * PAIR], qg_ref if qk_norm else None, Q_SCALE)
    col += n_q
    out_idx += 1

    acc = jnp.dot(h, w_ref[:, col:col + n_kv], preferred_element_type=F32)
    for c in range(n_kv // PAIR):
        out_refs[out_idx][0, :, c * PAIR:(c + 1) * PAIR] = head_epilogue(
            acc[:, c * PAIR:(c + 1) * PAIR], kg_ref if qk_norm else None, 1.0)
    col += n_kv
    out_idx += 1

    acc = jnp.dot(h, w_ref[:, col:col + n_v], preferred_element_type=F32)
    if v_rows:
        out_refs[out_idx][0] = acc.astype(BF16)
        out_idx += 1
    if v_chunks:
        stride = PAIR if v_dup else HEAD_DIM
        ones = jnp.ones((ONES_ROWS, KEY_CHUNK), F32)
        for cc in range(tm // KEY_CHUNK):
            at = acc[cc * KEY_CHUNK:(cc + 1) * KEY_CHUNK, :].T
            pieces = []
            for j in range(n_v // stride):
                pieces += [at[j * stride:j * stride + HEAD_DIM], ones]
            out_refs[out_idx][0, cc] = jnp.concatenate(pieces, axis=0).astype(BF16)


def _inproj(x, mods, mod_row, pre_g, w, *, n_u, n_q, n_kv, n_v, v_rows, v_chunks,
            rope_tabs=None, qk_gains=None):
    b, t, d = x.shape
    n = w.shape[1]
    tm = _row_tile(t, 512)
    rope = rope_tabs is not None
    qk_norm = qk_gains is not None
    v_dup = v_rows
    vt_rows = n_v // (PAIR if v_dup else HEAD_DIM) * VT_ROWS
    assert tm % KEY_CHUNK == 0

    in_specs = [
        pl.BlockSpec((1, tm, d), lambda bi, i: (bi, i, 0)),
        pl.BlockSpec((1, 1, d), lambda bi, i: (mod_row(bi), 0, 0)),
        pl.BlockSpec((1, 1, d), lambda bi, i: (mod_row(bi), 0, 1)),
        pl.BlockSpec((1, d), lambda bi, i: (0, 0)),
        pl.BlockSpec((d, n), lambda bi, i: (0, 0)),
    ]
    args = [x, mods, mods, pre_g, w]
    if rope:
        in_specs += [pl.BlockSpec((tm, PAIR), lambda bi, i: (i, 0))] * 2
        args += list(rope_tabs)
    if qk_norm:
        in_specs += [pl.BlockSpec((1, PAIR), lambda bi, i: (0, 0))] * 2
        in_specs += [pl.BlockSpec((PAIR, PAIR), lambda bi, i: (0, 0))]
        args += list(qk_gains)

    out_shape, out_specs = [], []
    for width, dtype in ((n_u, F32), (n_q, BF16), (n_kv, BF16), (n_v if v_rows else 0, BF16)):
        if width:
            out_shape.append(jax.ShapeDtypeStruct((b, t, width), dtype))
            out_specs.append(pl.BlockSpec((1, tm, width), lambda bi, i: (bi, i, 0)))
    if v_chunks:
        out_shape.append(jax.ShapeDtypeStruct((b, t // KEY_CHUNK, vt_rows, KEY_CHUNK), BF16))
        out_specs.append(pl.BlockSpec((1, tm // KEY_CHUNK, vt_rows, KEY_CHUNK),
                                      lambda bi, i: (bi, i, 0, 0)))

    blocks = (_nbytes((tm, d), F32) + _nbytes((d, n), BF16) + _nbytes((tm, n_u), F32)
              + _nbytes((tm, n_q + n_kv + 2 * n_v), BF16) + 2 * _nbytes((tm, PAIR), F32))
    temps = _nbytes((tm, d), F32) * 2 + _nbytes((tm, max(n_q, n_kv, n_u)), F32) * 2
    return pl.pallas_call(
        functools.partial(_inproj_kernel, n_u=n_u, n_q=n_q, n_kv=n_kv, n_v=n_v, v_rows=v_rows,
                          v_chunks=v_chunks, v_dup=v_dup, rope=rope, qk_norm=qk_norm, tm=tm),
        grid=(b, t // tm),
        in_specs=in_specs,
        out_specs=out_specs,
        out_shape=out_shape,
        compiler_params=pltpu.CompilerParams(
            dimension_semantics=("parallel", "parallel"),
            vmem_limit_bytes=_vmem_limit(blocks, temp_bytes=temps)),
        name="mixer_inproj",
    )(*args)


def _stack_heads(q_ref, qs_ref, tq):
    lane = lax.broadcasted_iota(jnp.int32, (tq, PAIR), 1)
    low = lane < HEAD_DIM
    for p in range(GROUP_COLS // PAIR):
        qp = q_ref[0, :, p * PAIR:(p + 1) * PAIR].astype(F32)
        qs_ref[(2 * p) * tq:(2 * p + 1) * tq, :] = jnp.where(low, qp, 0.0).astype(BF16)
        qs_ref[(2 * p + 1) * tq:(2 * p + 2) * tq, :] = jnp.where(low, 0.0, qp).astype(BF16)


def _unstack_heads(o, o_ref, tq):
    lane = lax.broadcasted_iota(jnp.int32, (tq, PAIR), 1)
    low = lane < HEAD_DIM
    for p in range(GROUP_COLS // PAIR):
        even = o[(2 * p) * tq:(2 * p + 1) * tq, :]
        odd = o[(2 * p + 1) * tq:(2 * p + 2) * tq, :]
        o_ref[0, :, p * PAIR:(p + 1) * PAIR] = jnp.where(low, even, odd).astype(o_ref.dtype)


def _scores(qs, k):
    return lax.dot_general(qs, k, (((1,), (1,)), ((), ())), preferred_element_type=F32)


def _flash_kernel(*refs, tq, n_chunks, has_sink):
    if has_sink:
        q_ref, k_ref, vt_ref, sink_ref, o_ref, qt_ref, s_ref, m_ref, acc_ref = refs
    else:
        q_ref, k_ref, vt_ref, o_ref, qt_ref, s_ref, m_ref, acc_ref = refs

    sub = lax.broadcasted_iota(jnp.int32, (PAIR, tq), 0)
    top = sub < HEAD_DIM
    for p in range(GROUP_COLS // PAIR):
        pair_t = q_ref[0, :, p * PAIR:(p + 1) * PAIR].astype(F32).T
        qt_ref[:, (2 * p) * tq:(2 * p + 1) * tq] = jnp.where(top, pair_t, 0.0).astype(BF16)
        qt_ref[:, (2 * p + 1) * tq:(2 * p + 2) * tq] = jnp.where(top, 0.0, pair_t).astype(BF16)
    if has_sink:
        m_ref[...] = sink_ref[0]
        den_rows = lax.broadcasted_iota(jnp.int32, acc_ref.shape, 0) >= HEAD_DIM
        acc_ref[...] = jnp.where(den_rows, 1.0, 0.0)
    else:
        m_ref[...] = jnp.full_like(m_ref, NEG)
        acc_ref[...] = jnp.zeros_like(acc_ref)

    def scores(c):
        start = pl.multiple_of(c * KEY_CHUNK, KEY_CHUNK)
        return jnp.dot(k_ref[0, pl.ds(start, KEY_CHUNK), :], qt_ref[...],
                       preferred_element_type=F32)

    def consume(slot, c):
        s = s_ref[slot]
        m_prev = m_ref[...]
        m_new = jnp.maximum(m_prev, jnp.max(s, axis=0, keepdims=True))
        alpha = jnp.exp2(m_prev - m_new)
        p = jnp.exp2(s - m_new)
        acc_ref[...] = alpha * acc_ref[...] + jnp.dot(vt_ref[0, c], p.astype(BF16),
                                                     preferred_element_type=F32)
        m_ref[...] = m_new

    s_ref[0] = scores(0)

    def body(pair, carry):
        c = 2 * pair
        s_ref[1] = scores(c + 1)
        consume(0, c)
        s_ref[0] = scores(c + 2)
        consume(1, c + 1)
        return carry
    n_pairs = n_chunks // 2
    lax.fori_loop(0, n_pairs, body, 0, unroll=2 if n_pairs % 2 == 0 else 1)
    consume(0, n_chunks - 1)

    o_t = acc_ref[0:HEAD_DIM, :] / acc_ref[HEAD_DIM:HEAD_DIM + 1, :]
    for p in range(GROUP_COLS // PAIR):
        pair_t = jnp.concatenate([o_t[:, (2 * p) * tq:(2 * p + 1) * tq],
                                  o_t[:, (2 * p + 1) * tq:(2 * p + 2) * tq]], axis=0)
        o_ref[0, :, p * PAIR:(p + 1) * PAIR] = pair_t.T.astype(o_ref.dtype)


def _flash_attention(q, k2, vt, sink_row=None, *, tq_want=256):
    b, t, hq = q.shape
    n_chunks = vt.shape[1]
    n_groups = hq // GROUP_COLS
    tq = _row_tile(t, tq_want)
    cols = GQA_GROUP * tq
    has_sink = sink_row is not None
    assert n_chunks % 2 == 1 and k2.shape[1] == n_chunks * KEY_CHUNK

    in_specs = [
        pl.BlockSpec((1, tq, GROUP_COLS), lambda bi, j, i: (bi, i, j)),
        pl.BlockSpec((1, n_chunks * KEY_CHUNK, PAIR), lambda bi, j, i: (bi, 0, j)),
        pl.BlockSpec((1, n_chunks, VT_ROWS, KEY_CHUNK), lambda bi, j, i: (bi, 0, j, 0)),
    ]
    args = [q, k2, vt]
    if has_sink:
        in_specs.append(pl.BlockSpec((1, 1, cols), lambda bi, j, i: (j, 0, 0)))
        args.append(sink_row)
    blocks = (2 * _nbytes((tq, GROUP_COLS), BF16)
              + _nbytes((n_chunks * KEY_CHUNK, PAIR + VT_ROWS), BF16))
    scratch = (_nbytes((PAIR, cols), BF16) + _nbytes((2, KEY_CHUNK, cols), F32)
               + _nbytes((VT_ROWS + SUBLANES, cols), F32))
    temps = 2 * _nbytes((KEY_CHUNK, cols), F32)
    return pl.pallas_call(
        functools.partial(_flash_kernel, tq=tq, n_chunks=n_chunks, has_sink=has_sink),
        grid=(b, n_groups, t // tq),
        in_specs=in_specs,
        out_specs=pl.BlockSpec((1, tq, GROUP_COLS), lambda bi, j, i: (bi, i, j)),
        out_shape=jax.ShapeDtypeStruct((b, t, hq), BF16),
        scratch_shapes=[
            pltpu.VMEM((PAIR, cols), BF16),
            pltpu.VMEM((2, KEY_CHUNK, cols), F32),
            pltpu.VMEM((1, cols), F32),
            pltpu.VMEM((VT_ROWS, cols), F32),
        ],
        compiler_params=pltpu.CompilerParams(
            dimension_semantics=("parallel", "parallel", "parallel"),
            vmem_limit_bytes=_vmem_limit(blocks, scratch, temps)),
        name="flash_attention",
    )(*args)


def _window_kernel(q_ref, k_ref, v_ref, kc_ref, vc_ref, sink_ref, o_ref, qs_ref, *, tq, t):
    span = tq + 2 * WINDOW
    q0 = pl.program_id(2) * tq
    start = pl.multiple_of(jnp.clip(q0 - WINDOW, 0, t - span), WINDOW)
    _stack_heads(q_ref, qs_ref, tq)
    qs = qs_ref[...]

    s_band = _scores(qs, k_ref[0, pl.ds(start, span), :])
    row = lax.broadcasted_iota(jnp.int32, s_band.shape, 0)
    col = lax.broadcasted_iota(jnp.int32, s_band.shape, 1)
    delta = (q0 + (row & (tq - 1))) - (start + col)
    s_band = jnp.where((delta <= WINDOW) & (delta >= -WINDOW), s_band, NEG)
    s_ctx = _scores(qs, kc_ref[0])

    sink = sink_ref[0]
    m = jnp.maximum(jnp.maximum(jnp.max(s_band, axis=1, keepdims=True),
                                jnp.max(s_ctx, axis=1, keepdims=True)), sink)
    e_band = jnp.exp2(s_band - m)
    e_ctx = jnp.exp2(s_ctx - m)
    den = (jnp.sum(e_band, axis=1, keepdims=True) + jnp.sum(e_ctx, axis=1, keepdims=True)
           + jnp.exp2(sink - m))
    acc = (jnp.dot(e_band.astype(BF16), v_ref[0, pl.ds(start, span), :],
                   preferred_element_type=F32)
           + jnp.dot(e_ctx.astype(BF16), vc_ref[0], preferred_element_type=F32))
    _unstack_heads(acc / den, o_ref, tq)


def _window_attention(q, k2, v2, kc2, vc2, sink_col, *, tq):
    b, t, hq = q.shape
    n_ctx = kc2.shape[1]
    n_groups = hq // GROUP_COLS
    rows = GQA_GROUP * tq
    span = tq + 2 * WINDOW
    assert t % tq == 0 and t >= span and tq & (tq - 1) == 0
    blocks = (2 * _nbytes((tq, GROUP_COLS), BF16) + 2 * _nbytes((t, PAIR), BF16)
              + 2 * _nbytes((n_ctx, PAIR), BF16))
    temps = 4 * _nbytes((rows, span + n_ctx), F32)
    return pl.pallas_call(
        functools.partial(_window_kernel, tq=tq, t=t),
        grid=(b, n_groups, t // tq),
        in_specs=[
            pl.BlockSpec((1, tq, GROUP_COLS), lambda bi, j, i: (bi, i, j)),
            pl.BlockSpec((1, t, PAIR), lambda bi, j, i: (bi, 0, j)),
            pl.BlockSpec((1, t, PAIR), lambda bi, j, i: (bi, 0, j)),
            pl.BlockSpec((1, n_ctx, PAIR), lambda bi, j, i: (bi, 0, j)),
            pl.BlockSpec((1, n_ctx, PAIR), lambda bi, j, i: (bi, 0, j)),
            pl.BlockSpec((1, rows, 1), lambda bi, j, i: (j, 0, 0)),
        ],
        out_specs=pl.BlockSpec((1, tq, GROUP_COLS), lambda bi, j, i: (bi, i, j)),
        out_shape=jax.ShapeDtypeStruct((b, t, hq), BF16),
        scratch_shapes=[pltpu.VMEM((rows, PAIR), BF16)],
        compiler_params=pltpu.CompilerParams(
            dimension_semantics=("parallel", "parallel", "parallel"),
            vmem_limit_bytes=_vmem_limit(blocks, _nbytes((rows, PAIR), BF16), temps)),
        name="window_attention",
    )(q, k2, v2, kc2, vc2, sink_col)


def _pool_kernel(u_ref, w_ref, scale_ref, o_ref, pad_ref, *, t):
    reach = max(POOL_WINDOWS) // 2
    assert reach <= HALO
    pos = lax.broadcasted_iota(jnp.int32, (t, POOL_GC), 0)
    zeros = jnp.zeros((HALO, POOL_GC), F32)
    pad_ref[0:HALO, :] = zeros
    pad_ref[HALO + t:2 * HALO + t, :] = zeros
    for g, win in enumerate(POOL_WINDOWS):
        half = win // 2
        u = u_ref[0, :, g * POOL_GC:(g + 1) * POOL_GC]
        pad_ref[HALO:HALO + t, :] = u
        total = pad_ref[pl.ds(HALO - half, t), :]
        for off in range(1 - half, half):
            total = total + pad_ref[pl.ds(HALO + off, t), :]
        cnt = (jnp.minimum(pos + half, t) - jnp.maximum(pos - half, 0)).astype(F32)
        d = (total / cnt - u).astype(BF16)
        y = jnp.dot(d, w_ref[g], preferred_element_type=F32)
        o_ref[0, :, g * POOL_GC:(g + 1) * POOL_GC] = (
            y * scale_ref[:, g * POOL_GC:(g + 1) * POOL_GC]).astype(BF16)


def _pool_mix(u, pool_w, pool_scale):
    b, t, ch = u.shape
    n_g = len(POOL_WINDOWS)
    assert ch == n_g * POOL_GC
    blocks = _nbytes((t, ch), F32) + _nbytes((t, ch), BF16) + _nbytes((n_g, POOL_GC, POOL_GC), BF16)
    scratch = _nbytes((t + 2 * HALO, POOL_GC), F32)
    return pl.pallas_call(
        functools.partial(_pool_kernel, t=t),
        grid=(b,),
        in_specs=[
            pl.BlockSpec((1, t, ch), lambda bi: (bi, 0, 0)),
            pl.BlockSpec((n_g, POOL_GC, POOL_GC), lambda bi: (0, 0, 0)),
            pl.BlockSpec((1, ch), lambda bi: (0, 0)),
        ],
        out_specs=pl.BlockSpec((1, t, ch), lambda bi: (bi, 0, 0)),
        out_shape=jax.ShapeDtypeStruct((b, t, ch), BF16),
        scratch_shapes=[pltpu.VMEM((t + 2 * HALO, POOL_GC), F32)],
        compiler_params=pltpu.CompilerParams(
            dimension_semantics=("parallel",),
            vmem_limit_bytes=_vmem_limit(blocks, scratch, 6 * _nbytes((t, POOL_GC), F32))),
        name="pool_mix",
    )(u, pool_w, pool_scale)


def _outproj_kernel(*refs, n_in):
    a_refs = refs[:n_in]
    w_refs = refs[n_in:2 * n_in]
    x_ref, gate_ref, g_ref, o_ref = refs[2 * n_in:]
    y = jnp.dot(a_refs[0][0], w_refs[0][...], preferred_element_type=F32)
    for a_ref, w_ref in zip(a_refs[1:], w_refs[1:]):
        y = y + jnp.dot(a_ref[0], w_ref[...], preferred_element_type=F32)
    o_ref[0] = x_ref[0] + gate_ref[0] * _rms_norm(y, g_ref[...])


def _outproj(acts, weights, x, mods, mod_row, post_g):
    b, t, d = x.shape
    tm = _row_tile(t, 512)
    n_in = len(acts)
    in_specs = [pl.BlockSpec((1, tm, a.shape[2]), lambda bi, i: (bi, i, 0)) for a in acts]
    in_specs += [pl.BlockSpec(w.shape, lambda bi, i: (0, 0)) for w in weights]
    in_specs += [
        pl.BlockSpec((1, tm, d), lambda bi, i: (bi, i, 0)),
        pl.BlockSpec((1, 1, d), lambda bi, i: (mod_row(bi), 0, 2)),
        pl.BlockSpec((1, d), lambda bi, i: (0, 0)),
    ]
    blocks = (sum(_nbytes((tm, a.shape[2]), BF16) for a in acts)
              + sum(_nbytes(w.shape, BF16) for w in weights) + 2 * _nbytes((tm, d), F32))
    return pl.pallas_call(
        functools.partial(_outproj_kernel, n_in=n_in),
        grid=(b, t // tm),
        in_specs=in_specs,
        out_specs=pl.BlockSpec((1, tm, d), lambda bi, i: (bi, i, 0)),
        out_shape=jax.ShapeDtypeStruct((b, t, d), F32),
        compiler_params=pltpu.CompilerParams(
            dimension_semantics=("parallel", "parallel"),
            vmem_limit_bytes=_vmem_limit(blocks, temp_bytes=3 * _nbytes((tm, d), F32))),
        name="mixer_outproj",
    )(*acts, *weights, x, mods, post_g)


def _ffn_kernel(x_ref, xp_ref, xn_ref, sh_ref, sc_ref, gate_ref, pre_g_ref, post_g_ref,
                wu_ref, cw_ref, cb_ref, wd_ref, o_ref, h_ref, acc_ref, u_ref, *, tm):
    i = pl.program_id(1)
    j = pl.program_id(2)

    @pl.when(j == 0)
    def _():
        g, sh, sc = pre_g_ref[...], sh_ref[0], sc_ref[0]
        h_prev = _pre_mod(xp_ref[0], g, sh, sc)
        h_next = _pre_mod(xn_ref[0], g, sh, sc)
        h_prev = jnp.where(i == 0, 0.0, h_prev)
        h_next = jnp.where(i == pl.num_programs(1) - 1, 0.0, h_next)
        h_ref[0:HALO, :] = h_prev.astype(BF16)
        h_ref[HALO:HALO + tm, :] = _pre_mod(x_ref[0], g, sh, sc).astype(BF16)
        h_ref[HALO + tm:2 * HALO + tm, :] = h_next.astype(BF16)
        acc_ref[...] = jnp.zeros_like(acc_ref)

    fc = FFN_CHUNK
    u_ref[...] = jnp.dot(h_ref[...], wu_ref[...], preferred_element_type=F32)
    conv = (u_ref[pl.ds(HALO - 1, tm), :] * cw_ref[0:1, :]
            + u_ref[pl.ds(HALO, tm), :] * cw_ref[1:2, :]
            + u_ref[pl.ds(HALO + 1, tm), :] * cw_ref[2:3, :]
            + cb_ref[...])
    gated = jax.nn.gelu(conv[:, :fc]) * conv[:, fc:]
    acc_ref[...] += jnp.dot(gated.astype(BF16), wd_ref[...], preferred_element_type=F32)

    @pl.when(j == pl.num_programs(2) - 1)
    def _():
        o_ref[0] = x_ref[0] + gate_ref[0] * _rms_norm(acc_ref[...], post_g_ref[...])


def _interleave_ffn_chunks(a, d_ff):
    lead = a.shape[:-1]
    a = a.reshape(*lead, 2, d_ff // FFN_CHUNK, FFN_CHUNK)
    return jnp.swapaxes(a, -3, -2).reshape(*lead, 2 * d_ff)


def _conv_ffn(x, mods, mod_row, pre_g, post_g, w_up, conv_w, conv_b, w_down, *, tm_want=512):
    b, t, d = x.shape
    d_ff = w_down.shape[0]
    tm = _row_tile(t, tm_want)
    fc = FFN_CHUNK
    assert d_ff % fc == 0 and tm % HALO == 0
    nj = d_ff // fc
    halo_blocks = tm // HALO
    last_halo = t // HALO - 1
    rows = tm + 2 * HALO

    in_specs = [
        pl.BlockSpec((1, tm, d), lambda bi, i, j: (bi, i, 0)),
        pl.BlockSpec((1, HALO, d), lambda bi, i, j: (bi, jnp.maximum(i * halo_blocks - 1, 0), 0)),
        pl.BlockSpec((1, HALO, d),
                     lambda bi, i, j: (bi, jnp.minimum((i + 1) * halo_blocks, last_halo), 0)),
        pl.BlockSpec((1, 1, d), lambda bi, i, j: (mod_row(bi), 0, 3)),
        pl.BlockSpec((1, 1, d), lambda bi, i, j: (mod_row(bi), 0, 4)),
        pl.BlockSpec((1, 1, d), lambda bi, i, j: (mod_row(bi), 0, 5)),
        pl.BlockSpec((1, d), lambda bi, i, j: (0, 0)),
        pl.BlockSpec((1, d), lambda bi, i, j: (0, 0)),
        pl.BlockSpec((d, 2 * fc), lambda bi, i, j: (0, j)),
        pl.BlockSpec((3, 2 * fc), lambda bi, i, j: (0, j)),
        pl.BlockSpec((1, 2 * fc), lambda bi, i, j: (0, j)),
        pl.BlockSpec((fc, d), lambda bi, i, j: (j, 0)),
    ]
    blocks = (2 * _nbytes((tm, d), F32) + 2 * _nbytes((HALO, d), F32)
              + 3 * _nbytes((d, fc), BF16))
    scratch = (_nbytes((rows, d), BF16) + 2 * _nbytes((rows, fc), F32) + _nbytes((tm, d), F32))
    temps = 3 * _nbytes((tm, fc), F32) + _nbytes((tm, d), F32)
    return pl.pallas_call(
        functools.partial(_ffn_kernel, tm=tm),
        grid=(b, t // tm, nj),
        in_specs=in_specs,
        out_specs=pl.BlockSpec((1, tm, d), lambda bi, i, j: (bi, i, 0)),
        out_shape=jax.ShapeDtypeStruct((b, t, d), F32),
        scratch_shapes=[
            pltpu.VMEM((rows, d), BF16),
            pltpu.VMEM((tm, d), F32),
            pltpu.VMEM((rows, 2 * fc), F32),
        ],
        compiler_params=pltpu.CompilerParams(
            dimension_semantics=("parallel", "parallel", "arbitrary"),
            vmem_limit_bytes=_vmem_limit(blocks, scratch, temps)),
        name="conv_ffn",
    )(x, x, x, mods, mods, mods, pre_g, post_g, w_up, conv_w, conv_b, w_down)


def _rope_tables(t):
    rows = t // GRID_W
    row = jnp.repeat(jnp.arange(rows), GRID_W).astype(F32)
    col = jnp.tile(jnp.arange(GRID_W), rows).astype(F32)
    n_freq = HEAD_DIM // 4
    freqs = ROPE_BASE ** (-jnp.arange(n_freq, dtype=F32) / n_freq)
    ang = jnp.concatenate([row[:, None] * freqs, col[:, None] * freqs], axis=-1)
    cos, sin = jnp.cos(ang), jnp.sin(ang)
    return (jnp.concatenate([cos, cos, cos, cos], axis=-1),
            jnp.concatenate([-sin, sin, -sin, sin], axis=-1))


def _dup_heads(w, n_heads):
    d = w.shape[0]
    w = w.reshape(d, n_heads, 1, HEAD_DIM)
    return jnp.broadcast_to(w, (d, n_heads, 2, HEAD_DIM)).reshape(d, n_heads * PAIR)


def _sink_column(sink, tq):
    return jnp.repeat(sink.astype(F32) * LOG2E, tq, axis=1)[..., None]


def _sink_row(sink, tq):
    return jnp.repeat(sink.astype(F32) * LOG2E, tq, axis=1)[:, None, :]


def kernel(x, c, ctx, c_ctx, ada_w, ada_b, mix_pre_g, mix_post_g, ffn_pre_g, ffn_post_g,
           ab_w_in, ab_w_out, pool_w, pool_scale, sink_logit,
           c_w_qkv, c_w_out, c_q_g, c_k_g,
           ffn_w_up, ffn_conv_w, ffn_conv_b, ffn_w_down):
    batch, t, d = x.shape
    n_ctx = ctx.shape[1]
    depth = ada_w.shape[0]
    pool_ch = pool_scale.shape[1]
    b_kv = sink_logit.shape[1]
    b_q = b_kv * GQA_GROUP * HEAD_DIM
    c_q = c_w_out.shape[1]
    c_kv = c_q // HEAD_DIM // GQA_GROUP

    ctx_row = batch
    n_rows = -(-(batch + 1) // SUBLANES) * SUBLANES
    cc = jnp.zeros((n_rows, d), F32).at[:batch].set(c).at[ctx_row].set(c_ctx)
    mods_all = _ada_params(cc, ada_w, ada_b).reshape(depth, n_rows, 1, N_MOD * d)

    rope_tabs = _rope_tables(t)
    ones_bd = jnp.kron(jnp.eye(2, dtype=F32), jnp.ones((HEAD_DIM, HEAD_DIM), F32)).astype(BF16)
    latent_row = lambda bi: bi
    context_row = lambda bi: ctx_row
    win_tq = min(128, t)

    xc = ctx
    for l in range(depth):
        need_ctx = l < depth - 1
        i = l // 2
        mods = mods_all[l]
        pre_g, post_g = mix_pre_g[l][None], mix_post_g[l][None]
        if l % 2 == 0:
            w = ab_w_in[i]
            o = pool_ch
            w_in = jnp.concatenate([
                w[:, :o + b_q],
                _dup_heads(w[:, o + b_q:o + b_q + b_kv * HEAD_DIM], b_kv),
                _dup_heads(w[:, o + b_q + b_kv * HEAD_DIM:], b_kv)], axis=1).astype(BF16)
            widths = dict(n_u=pool_ch, n_q=b_q, n_kv=b_kv * PAIR, n_v=b_kv * PAIR, v_rows=True)
            u, q, k2, v2 = _inproj(x, mods, latent_row, pre_g, w_in, rope_tabs=rope_tabs,
                                   v_chunks=False, **widths)
            uc, qc, kc2, vc2, *vtc = _inproj(xc, mods, context_row, pre_g, w_in,
                                             v_chunks=need_ctx, **widths)
            w_out = ab_w_out[i].astype(BF16)
            w_outs = [w_out[:pool_ch], w_out[pool_ch:]]
            pw, ps = pool_w[i].astype(BF16), pool_scale[i][None]
            a = _window_attention(q, k2, v2, kc2, vc2, _sink_column(sink_logit[i], win_tq),
                                  tq=win_tq)
            x = _outproj([_pool_mix(u, pw, ps), a], w_outs, x, mods, latent_row, post_g)
            if need_ctx:
                tqc = _row_tile(n_ctx, 256)
                ac = _flash_attention(qc, kc2, vtc[0], _sink_row(sink_logit[i], tqc), tq_want=tqc)
                xc = _outproj([_pool_mix(uc, pw, ps), ac], w_outs, xc, mods, context_row, post_g)
        else:
            w = c_w_qkv[i]
            w_in = jnp.concatenate([
                w[:, :c_q],
                _dup_heads(w[:, c_q:c_q + c_kv * HEAD_DIM], c_kv),
                w[:, c_q + c_kv * HEAD_DIM:]], axis=1).astype(BF16)
            gains = (jnp.tile(c_q_g[i], 2)[None], jnp.tile(c_k_g[i], 2)[None], ones_bd)
            widths = dict(n_u=0, n_q=c_q, n_kv=c_kv * PAIR, n_v=c_kv * HEAD_DIM, v_rows=False,
                          v_chunks=True)
            q, k2, vt = _inproj(x, mods, latent_row, pre_g, w_in, rope_tabs=rope_tabs,
                                qk_gains=gains, **widths)
            qc, kc2, vtc = _inproj(xc, mods, context_row, pre_g, w_in, qk_gains=gains, **widths)
            w_outs = [c_w_out[i].astype(BF16)]
            a = _flash_attention(q, jnp.concatenate([k2, kc2], axis=1),
                                 jnp.concatenate([vt, vtc], axis=1))
            x = _outproj([a], w_outs, x, mods, latent_row, post_g)
            if need_ctx:
                ac = _flash_attention(qc, kc2, vtc)
                xc = _outproj([ac], w_outs, xc, mods, context_row, post_g)

        d_ff = ffn_w_down.shape[1]
        ffn_args = (ffn_pre_g[l][None], ffn_post_g[l][None],
                    _interleave_ffn_chunks(ffn_w_up[l], d_ff).astype(BF16),
                    _interleave_ffn_chunks(ffn_conv_w[l], d_ff),
                    _interleave_ffn_chunks(ffn_conv_b[l][None], d_ff),
                    ffn_w_down[l].astype(BF16))
        x = _conv_ffn(x, mods, latent_row, *ffn_args)
        if need_ctx:
            xc = _conv_ffn(xc, mods, context_row, *ffn_args)
    return x
```

```python
import functools

import jax
import jax.numpy as jnp
from jax import lax
from jax.experimental import pallas as pl
from jax.experimental.pallas import tpu as pltpu

F32 = jnp.float32
BF16 = jnp.bfloat16

HEAD_DIM = 64
GQA_GROUP = 4
POOL_WINDOWS = (2, 4, 8, 16)
POOL_GC = 128
WINDOW = 128
GRID_W = 64
ROPE_BASE = 10000.0
EPS = 1e-6
NEG = -1e30
N_MOD = 6

LANES = 128
SUBLANES = 8
VMEM_LIMIT_CAP = 56 * 1024 * 1024
VMEM_LIMIT_FLOOR = 32 * 1024 * 1024

PAIR = 2 * HEAD_DIM
GROUP_COLS = GQA_GROUP * HEAD_DIM
HALO = SUBLANES
KEY_CHUNK = 256
BAND_CHUNK = WINDOW
LOG2E = 1.4426950408889634
Q_SCALE = HEAD_DIM ** -0.5 * LOG2E
ONES_ROWS = 16
VT_ROWS = HEAD_DIM + ONES_ROWS
FFN_CHUNK = 896


def _vmem_limit(block_bytes, scratch_bytes=0, temp_bytes=0):
    need = 2 * block_bytes + scratch_bytes + temp_bytes
    return int(min(max(need + need // 4, VMEM_LIMIT_FLOOR), VMEM_LIMIT_CAP))


def _nbytes(shape, dtype):
    n = 1
    for s in shape:
        n *= s
    return n * jnp.dtype(dtype).itemsize


def _row_tile(t, want):
    tile = min(t, want)
    assert t % tile == 0
    return tile


def _ada_kernel(c_ref, w_ref, b_ref, o_ref):
    c = c_ref[...]
    a = (c * jax.nn.sigmoid(c)).astype(BF16)
    o_ref[0] = jnp.dot(a, w_ref[0].astype(BF16), preferred_element_type=F32) + b_ref[0]


def _ada_params(cc, ada_w, ada_b):
    depth, d, n = ada_w.shape
    rows = cc.shape[0]
    tn = n // N_MOD
    blocks = _nbytes((rows, d), F32) + _nbytes((d, tn), F32) + _nbytes((rows, tn), F32)
    return pl.pallas_call(
        _ada_kernel,
        grid=(depth, n // tn),
        in_specs=[
            pl.BlockSpec((rows, d), lambda l, j: (0, 0)),
            pl.BlockSpec((1, d, tn), lambda l, j: (l, 0, j)),
            pl.BlockSpec((1, 1, tn), lambda l, j: (l, 0, j)),
        ],
        out_specs=pl.BlockSpec((1, rows, tn), lambda l, j: (l, 0, j)),
        out_shape=jax.ShapeDtypeStruct((depth, rows, n), F32),
        compiler_params=pltpu.CompilerParams(
            dimension_semantics=("parallel", "parallel"),
            vmem_limit_bytes=_vmem_limit(blocks, temp_bytes=_nbytes((d, tn), BF16))),
        name="ada_params",
    )(cc, ada_w, ada_b.reshape(depth, 1, n))


def _rms_norm(x, g):
    return x * lax.rsqrt(jnp.mean(x * x, axis=-1, keepdims=True) + EPS) * g


def _pre_mod(x, g, shift, scale):
    return _rms_norm(x, g) * (1.0 + scale) + shift


def _inproj_kernel(*refs, n_u, n_q, n_kv, n_v, key_chunk, rope, qk_norm, tm):
    x_ref, sh_ref, sc_ref, g_ref, w_ref = refs[:5]
    pos = 5
    if rope:
        cos_ref, sin_ref = refs[pos:pos + 2]
        pos += 2
    if qk_norm:
        qg_ref, kg_ref, ones_ref = refs[pos:pos + 3]
        pos += 3
    out_refs = refs[pos:]

    h = _pre_mod(x_ref[0], g_ref[...], sh_ref[0], sc_ref[0]).astype(BF16)

    if rope:
        lane = lax.broadcasted_iota(jnp.int32, (tm, PAIR), 1)
        first_half = (lane & (HEAD_DIM - 1)) < HEAD_DIM // 2
        cos = cos_ref[...]
        sin = sin_ref[...]

    def head_epilogue(a, gain_ref, scale):
        if qk_norm:
            a2 = a * a
            hi = a2.astype(BF16)
            lo = (a2 - hi.astype(F32)).astype(BF16)
            ss = (jnp.dot(hi, ones_ref[...], preferred_element_type=F32)
                  + jnp.dot(lo, ones_ref[...], preferred_element_type=F32))
            a = a * lax.rsqrt(ss * (1.0 / HEAD_DIM) + EPS) * gain_ref[...]
        if rope:
            partner = jnp.where(first_half,
                                pltpu.roll(a, PAIR - HEAD_DIM // 2, 1),
                                pltpu.roll(a, HEAD_DIM // 2, 1))
            a = a * cos + partner * sin
        if scale != 1.0:
            a = a * scale
        return a.astype(BF16)

    col = 0
    out_idx = 0
    if n_u:
        out_refs[out_idx][0] = jnp.dot(h, w_ref[:, col:col + n_u], preferred_element_type=F32)
        col += n_u
        out_idx += 1

    acc = jnp.dot(h, w_ref[:, col:col + n_q], preferred_element_type=F32)
    for c in range(n_q // PAIR):
        out_refs[out_idx][0, :, c * PAIR:(c + 1) * PAIR] = head_epilogue(
            acc[:, c * PAIR:(c + 1) * PAIR], qg_ref if qk_norm else None, Q_SCALE)
    col += n_q
    out_idx += 1

    acc = jnp.dot(h, w_ref[:, col:col + n_kv], preferred_element_type=F32)
    for c in range(n_kv // PAIR):
        out_refs[out_idx][0, :, c * PAIR:(c + 1) * PAIR] = head_epilogue(
            acc[:, c * PAIR:(c + 1) * PAIR], kg_ref if qk_norm else None, 1.0)
    col += n_kv
    out_idx += 1

    acc = jnp.dot(h, w_ref[:, col:col + n_v], preferred_element_type=F32)
    ones = jnp.ones((ONES_ROWS, key_chunk), F32)
    for cc in range(tm // key_chunk):
        at = acc[cc * key_chunk:(cc + 1) * key_chunk, :].T
        pieces = []
        for j in range(n_v // HEAD_DIM):
            pieces += [at[j * HEAD_DIM:(j + 1) * HEAD_DIM], ones]
        out_refs[out_idx][0, cc] = jnp.concatenate(pieces, axis=0).astype(BF16)


def _inproj(x, mods, mod_row, pre_g, w, *, n_u, n_q, n_kv, n_v, key_chunk=KEY_CHUNK,
            rope_tabs=None, qk_gains=None):
    b, t, d = x.shape
    n = w.shape[1]
    tm = _row_tile(t, 512)
    rope = rope_tabs is not None
    qk_norm = qk_gains is not None
    vt_rows = n_v // HEAD_DIM * VT_ROWS
    assert tm % key_chunk == 0

    in_specs = [
        pl.BlockSpec((1, tm, d), lambda bi, i: (bi, i, 0)),
        pl.BlockSpec((1, 1, d), lambda bi, i: (mod_row(bi), 0, 0)),
        pl.BlockSpec((1, 1, d), lambda bi, i: (mod_row(bi), 0, 1)),
        pl.BlockSpec((1, d), lambda bi, i: (0, 0)),
        pl.BlockSpec((d, n), lambda bi, i: (0, 0)),
    ]
    args = [x, mods, mods, pre_g, w]
    if rope:
        in_specs += [pl.BlockSpec((tm, PAIR), lambda bi, i: (i, 0))] * 2
        args += list(rope_tabs)
    if qk_norm:
        in_specs += [pl.BlockSpec((1, PAIR), lambda bi, i: (0, 0))] * 2
        in_specs += [pl.BlockSpec((PAIR, PAIR), lambda bi, i: (0, 0))]
        args += list(qk_gains)

    out_shape, out_specs = [], []
    for width, dtype in ((n_u, F32), (n_q, BF16), (n_kv, BF16)):
        if width:
            out_shape.append(jax.ShapeDtypeStruct((b, t, width), dtype))
            out_specs.append(pl.BlockSpec((1, tm, width), lambda bi, i: (bi, i, 0)))
    out_shape.append(jax.ShapeDtypeStruct((b, t // key_chunk, vt_rows, key_chunk), BF16))
    out_specs.append(pl.BlockSpec((1, tm // key_chunk, vt_rows, key_chunk),
                                  lambda bi, i: (bi, i, 0, 0)))

    blocks = (_nbytes((tm, d), F32) + _nbytes((d, n), BF16) + _nbytes((tm, n_u), F32)
              + _nbytes((tm, n_q + n_kv + 2 * n_v), BF16) + 2 * _nbytes((tm, PAIR), F32))
    temps = _nbytes((tm, d), F32) * 2 + _nbytes((tm, max(n_q, n_kv, n_u)), F32) * 2
    return pl.pallas_call(
        functools.partial(_inproj_kernel, n_u=n_u, n_q=n_q, n_kv=n_kv, n_v=n_v,
                          key_chunk=key_chunk, rope=rope, qk_norm=qk_norm, tm=tm),
        grid=(b, t // tm),
        in_specs=in_specs,
        out_specs=out_specs,
        out_shape=out_shape,
        compiler_params=pltpu.CompilerParams(
            dimension_semantics=("parallel", "parallel"),
            vmem_limit_bytes=_vmem_limit(blocks, temp_bytes=temps)),
        name="mixer_inproj",
    )(*args)


def _stack_heads_transposed(q_ref, qt_ref, tq):
    sub = lax.broadcasted_iota(jnp.int32, (PAIR, tq), 0)
    top = sub < HEAD_DIM
    for p in range(GROUP_COLS // PAIR):
        pair_t = q_ref[0, :, p * PAIR:(p + 1) * PAIR].astype(F32).T
        qt_ref[:, (2 * p) * tq:(2 * p + 1) * tq] = jnp.where(top, pair_t, 0.0).astype(BF16)
        qt_ref[:, (2 * p + 1) * tq:(2 * p + 2) * tq] = jnp.where(top, 0.0, pair_t).astype(BF16)


def _store_heads(o_t, o_ref, tq):
    for p in range(GROUP_COLS // PAIR):
        pair_t = jnp.concatenate([o_t[:, (2 * p) * tq:(2 * p + 1) * tq],
                                  o_t[:, (2 * p + 1) * tq:(2 * p + 2) * tq]], axis=0)
        o_ref[0, :, p * PAIR:(p + 1) * PAIR] = pair_t.T.astype(o_ref.dtype)


def _flash_kernel(*refs, tq, n_chunks, has_sink):
    if has_sink:
        q_ref, k_ref, vt_ref, sink_ref, o_ref, qt_ref, s_ref, m_ref, acc_ref = refs
    else:
        q_ref, k_ref, vt_ref, o_ref, qt_ref, s_ref, m_ref, acc_ref = refs

    _stack_heads_transposed(q_ref, qt_ref, tq)
    if has_sink:
        m_ref[...] = sink_ref[0]
        den_rows = lax.broadcasted_iota(jnp.int32, acc_ref.shape, 0) >= HEAD_DIM
        acc_ref[...] = jnp.where(den_rows, 1.0, 0.0)
    else:
        m_ref[...] = jnp.full_like(m_ref, NEG)
        acc_ref[...] = jnp.zeros_like(acc_ref)

    def scores(c):
        start = pl.multiple_of(c * KEY_CHUNK, KEY_CHUNK)
        return jnp.dot(k_ref[0, pl.ds(start, KEY_CHUNK), :], qt_ref[...],
                       preferred_element_type=F32)

    def consume(slot, c):
        s = s_ref[slot]
        m_prev = m_ref[...]
        m_new = jnp.maximum(m_prev, jnp.max(s, axis=0, keepdims=True))
        alpha = jnp.exp2(m_prev - m_new)
        p = jnp.exp2(s - m_new)
        acc_ref[...] = alpha * acc_ref[...] + jnp.dot(vt_ref[0, c], p.astype(BF16),
                                                     preferred_element_type=F32)
        m_ref[...] = m_new

    s_ref[0] = scores(0)

    def body(pair, carry):
        c = 2 * pair
        s_ref[1] = scores(c + 1)
        consume(0, c)
        s_ref[0] = scores(c + 2)
        consume(1, c + 1)
        return carry
    n_pairs = n_chunks // 2
    lax.fori_loop(0, n_pairs, body, 0, unroll=2 if n_pairs % 2 == 0 else 1)
    consume(0, n_chunks - 1)

    _store_heads(acc_ref[0:HEAD_DIM, :] / acc_ref[HEAD_DIM:HEAD_DIM + 1, :], o_ref, tq)


def _flash_attention(q, k2, vt, sink_row=None, *, tq_want=256):
    b, t, hq = q.shape
    n_chunks = vt.shape[1]
    n_groups = hq // GROUP_COLS
    tq = _row_tile(t, tq_want)
    cols = GQA_GROUP * tq
    has_sink = sink_row is not None
    assert n_chunks % 2 == 1 and k2.shape[1] == n_chunks * KEY_CHUNK

    in_specs = [
        pl.BlockSpec((1, tq, GROUP_COLS), lambda bi, j, i: (bi, i, j)),
        pl.BlockSpec((1, n_chunks * KEY_CHUNK, PAIR), lambda bi, j, i: (bi, 0, j)),
        pl.BlockSpec((1, n_chunks, VT_ROWS, KEY_CHUNK), lambda bi, j, i: (bi, 0, j, 0)),
    ]
    args = [q, k2, vt]
    if has_sink:
        in_specs.append(pl.BlockSpec((1, 1, cols), lambda bi, j, i: (j, 0, 0)))
        args.append(sink_row)
    blocks = (2 * _nbytes((tq, GROUP_COLS), BF16)
              + _nbytes((n_chunks * KEY_CHUNK, PAIR + VT_ROWS), BF16))
    scratch = (_nbytes((PAIR, cols), BF16) + _nbytes((2, KEY_CHUNK, cols), F32)
               + _nbytes((VT_ROWS + SUBLANES, cols), F32))
    temps = 2 * _nbytes((KEY_CHUNK, cols), F32)
    return pl.pallas_call(
        functools.partial(_flash_kernel, tq=tq, n_chunks=n_chunks, has_sink=has_sink),
        grid=(b, n_groups, t // tq),
        in_specs=in_specs,
        out_specs=pl.BlockSpec((1, tq, GROUP_COLS), lambda bi, j, i: (bi, i, j)),
        out_shape=jax.ShapeDtypeStruct((b, t, hq), BF16),
        scratch_shapes=[
            pltpu.VMEM((PAIR, cols), BF16),
            pltpu.VMEM((2, KEY_CHUNK, cols), F32),
            pltpu.VMEM((1, cols), F32),
            pltpu.VMEM((VT_ROWS, cols), F32),
        ],
        compiler_params=pltpu.CompilerParams(
            dimension_semantics=("parallel", "parallel", "parallel"),
            vmem_limit_bytes=_vmem_limit(blocks, scratch, temps)),
        name="flash_attention",
    )(*args)


def _window_kernel(q_ref, k_ref, vt_ref, kc_ref, vtc_ref, bias_ref, sink_ref, o_ref, qt_ref,
                   *, tq, t):
    span = tq + 2 * WINDOW
    q0 = pl.program_id(2) * tq
    start = pl.multiple_of(jnp.clip(q0 - WINDOW, 0, t - span), WINDOW)
    bias = bias_ref[(q0 - start) // WINDOW]
    _stack_heads_transposed(q_ref, qt_ref, tq)
    qt = qt_ref[...]

    s_band = jnp.dot(k_ref[0, pl.ds(start, span), :], qt, preferred_element_type=F32)
    s_band = s_band + jnp.concatenate([bias] * GQA_GROUP, axis=1)
    s_ctx = jnp.dot(kc_ref[0], qt, preferred_element_type=F32)

    sink = sink_ref[0]
    m = jnp.maximum(jnp.maximum(jnp.max(s_band, axis=0, keepdims=True),
                                jnp.max(s_ctx, axis=0, keepdims=True)), sink)
    p_band = jnp.exp2(s_band - m).astype(BF16)
    p_ctx = jnp.exp2(s_ctx - m).astype(BF16)
    first = start // BAND_CHUNK
    vt_band = jnp.concatenate([vt_ref[0, first + c] for c in range(span // BAND_CHUNK)], axis=1)
    acc = (jnp.dot(vt_band, p_band, preferred_element_type=F32)
           + jnp.dot(vtc_ref[0, 0], p_ctx, preferred_element_type=F32))
    den = acc[HEAD_DIM:HEAD_DIM + 1, :] + jnp.exp2(sink - m)
    _store_heads(acc[0:HEAD_DIM, :] / den, o_ref, tq)


def _band_bias(tq):
    span = tq + 2 * WINDOW
    key = jnp.arange(span)[None, :, None]
    qry = jnp.arange(tq)[None, None, :]
    back = (jnp.arange(3) * WINDOW)[:, None, None]
    delta = qry + back - key
    return jnp.where(jnp.abs(delta) <= WINDOW, 0.0, NEG).astype(F32)


def _window_attention(q, k2, vt, kc2, vtc, sink_row, *, tq):
    b, t, hq = q.shape
    n_ctx = kc2.shape[1]
    n_groups = hq // GROUP_COLS
    cols = GQA_GROUP * tq
    span = tq + 2 * WINDOW
    n_band = t // BAND_CHUNK
    assert t % tq == 0 and t >= span and tq % WINDOW == 0 and vtc.shape[1] == 1
    blocks = (2 * _nbytes((tq, GROUP_COLS), BF16) + _nbytes((t, PAIR + VT_ROWS), BF16)
              + _nbytes((n_ctx, PAIR + VT_ROWS), BF16) + _nbytes((3, span, tq), F32))
    temps = 3 * _nbytes((span + n_ctx, cols), F32)
    return pl.pallas_call(
        functools.partial(_window_kernel, tq=tq, t=t),
        grid=(b, n_groups, t // tq),
        in_specs=[
            pl.BlockSpec((1, tq, GROUP_COLS), lambda bi, j, i: (bi, i, j)),
            pl.BlockSpec((1, t, PAIR), lambda bi, j, i: (bi, 0, j)),
            pl.BlockSpec((1, n_band, VT_ROWS, BAND_CHUNK), lambda bi, j, i: (bi, 0, j, 0)),
            pl.BlockSpec((1, n_ctx, PAIR), lambda bi, j, i: (bi, 0, j)),
            pl.BlockSpec((1, 1, VT_ROWS, n_ctx), lambda bi, j, i: (bi, 0, j, 0)),
            pl.BlockSpec((3, span, tq), lambda bi, j, i: (0, 0, 0)),
            pl.BlockSpec((1, 1, cols), lambda bi, j, i: (j, 0, 0)),
        ],
        out_specs=pl.BlockSpec((1, tq, GROUP_COLS), lambda bi, j, i: (bi, i, j)),
        out_shape=jax.ShapeDtypeStruct((b, t, hq), BF16),
        scratch_shapes=[pltpu.VMEM((PAIR, cols), BF16)],
        compiler_params=pltpu.CompilerParams(
            dimension_semantics=("parallel", "parallel", "parallel"),
            vmem_limit_bytes=_vmem_limit(blocks, _nbytes((PAIR, cols), BF16), temps)),
        name="window_attention",
    )(q, k2, vt, kc2, vtc, _band_bias(tq), sink_row)


def _pool_kernel(u_ref, w_ref, scale_ref, o_ref, pad_ref, *, t):
    reach = max(POOL_WINDOWS) // 2
    assert reach <= HALO
    pos = lax.broadcasted_iota(jnp.int32, (t, POOL_GC), 0)
    zeros = jnp.zeros((HALO, POOL_GC), F32)
    pad_ref[0:HALO, :] = zeros
    pad_ref[HALO + t:2 * HALO + t, :] = zeros
    for g, win in enumerate(POOL_WINDOWS):
        half = win // 2
        u = u_ref[0, :, g * POOL_GC:(g + 1) * POOL_GC]
        pad_ref[HALO:HALO + t, :] = u
        total = pad_ref[pl.ds(HALO - half, t), :]
        for off in range(1 - half, half):
            total = total + pad_ref[pl.ds(HALO + off, t), :]
        cnt = (jnp.minimum(pos + half, t) - jnp.maximum(pos - half, 0)).astype(F32)
        d = (total / cnt - u).astype(BF16)
        y = jnp.dot(d, w_ref[g], preferred_element_type=F32)
        o_ref[0, :, g * POOL_GC:(g + 1) * POOL_GC] = (
            y * scale_ref[:, g * POOL_GC:(g + 1) * POOL_GC]).astype(BF16)


def _pool_mix(u, pool_w, pool_scale):
    b, t, ch = u.shape
    n_g = len(POOL_WINDOWS)
    assert ch == n_g * POOL_GC
    blocks = _nbytes((t, ch), F32) + _nbytes((t, ch), BF16) + _nbytes((n_g, POOL_GC, POOL_GC), BF16)
    scratch = _nbytes((t + 2 * HALO, POOL_GC), F32)
    return pl.pallas_call(
        functools.partial(_pool_kernel, t=t),
        grid=(b,),
        in_specs=[
            pl.BlockSpec((1, t, ch), lambda bi: (bi, 0, 0)),
            pl.BlockSpec((n_g, POOL_GC, POOL_GC), lambda bi: (0, 0, 0)),
            pl.BlockSpec((1, ch), lambda bi: (0, 0)),
        ],
        out_specs=pl.BlockSpec((1, t, ch), lambda bi: (bi, 0, 0)),
        out_shape=jax.ShapeDtypeStruct((b, t, ch), BF16),
        scratch_shapes=[pltpu.VMEM((t + 2 * HALO, POOL_GC), F32)],
        compiler_params=pltpu.CompilerParams(
            dimension_semantics=("parallel",),
            vmem_limit_bytes=_vmem_limit(blocks, scratch, 6 * _nbytes((t, POOL_GC), F32))),
        name="pool_mix",
    )(u, pool_w, pool_scale)


def _outproj_kernel(*refs, n_in):
    a_refs = refs[:n_in]
    w_refs = refs[n_in:2 * n_in]
    x_ref, gate_ref, g_ref, o_ref = refs[2 * n_in:]
    y = jnp.dot(a_refs[0][0], w_refs[0][...], preferred_element_type=F32)
    for a_ref, w_ref in zip(a_refs[1:], w_refs[1:]):
        y = y + jnp.dot(a_ref[0], w_ref[...], preferred_element_type=F32)
    o_ref[0] = x_ref[0] + gate_ref[0] * _rms_norm(y, g_ref[...])


def _outproj(acts, weights, x, mods, mod_row, post_g):
    b, t, d = x.shape
    tm = _row_tile(t, 512)
    n_in = len(acts)
    in_specs = [pl.BlockSpec((1, tm, a.shape[2]), lambda bi, i: (bi, i, 0)) for a in acts]
    in_specs += [pl.BlockSpec(w.shape, lambda bi, i: (0, 0)) for w in weights]
    in_specs += [
        pl.BlockSpec((1, tm, d), lambda bi, i: (bi, i, 0)),
        pl.BlockSpec((1, 1, d), lambda bi, i: (mod_row(bi), 0, 2)),
        pl.BlockSpec((1, d), lambda bi, i: (0, 0)),
    ]
    blocks = (sum(_nbytes((tm, a.shape[2]), BF16) for a in acts)
              + sum(_nbytes(w.shape, BF16) for w in weights) + 2 * _nbytes((tm, d), F32))
    return pl.pallas_call(
        functools.partial(_outproj_kernel, n_in=n_in),
        grid=(b, t // tm),
        in_specs=in_specs,
        out_specs=pl.BlockSpec((1, tm, d), lambda bi, i: (bi, i, 0)),
        out_shape=jax.ShapeDtypeStruct((b, t, d), F32),
        compiler_params=pltpu.CompilerParams(
            dimension_semantics=("parallel", "parallel"),
            vmem_limit_bytes=_vmem_limit(blocks, temp_bytes=3 * _nbytes((tm, d), F32))),
        name="mixer_outproj",
    )(*acts, *weights, x, mods, post_g)


def _ffn_kernel(x_ref, xp_ref, xn_ref, sh_ref, sc_ref, gate_ref, pre_g_ref, post_g_ref,
                wu_ref, cw_ref, cb_ref, wd_ref, o_ref, h_ref, acc_ref, u_ref, *, tm):
    i = pl.program_id(1)
    j = pl.program_id(2)

    @pl.when(j == 0)
    def _():
        g, sh, sc = pre_g_ref[...], sh_ref[0], sc_ref[0]
        h_prev = _pre_mod(xp_ref[0], g, sh, sc)
        h_next = _pre_mod(xn_ref[0], g, sh, sc)
        h_prev = jnp.where(i == 0, 0.0, h_prev)
        h_next = jnp.where(i == pl.num_programs(1) - 1, 0.0, h_next)
        h_ref[0:HALO, :] = h_prev.astype(BF16)
        h_ref[HALO:HALO + tm, :] = _pre_mod(x_ref[0], g, sh, sc).astype(BF16)
        h_ref[HALO + tm:2 * HALO + tm, :] = h_next.astype(BF16)
        acc_ref[...] = jnp.zeros_like(acc_ref)

    fc = FFN_CHUNK
    u_ref[...] = jnp.dot(h_ref[...], wu_ref[...], preferred_element_type=F32)
    conv = (u_ref[pl.ds(HALO - 1, tm), :] * cw_ref[0:1, :]
            + u_ref[pl.ds(HALO, tm), :] * cw_ref[1:2, :]
            + u_ref[pl.ds(HALO + 1, tm), :] * cw_ref[2:3, :]
            + cb_ref[...])
    gated = jax.nn.gelu(conv[:, :fc]) * conv[:, fc:]
    acc_ref[...] += jnp.dot(gated.astype(BF16), wd_ref[...], preferred_element_type=F32)

    @pl.when(j == pl.num_programs(2) - 1)
    def _():
        o_ref[0] = x_ref[0] + gate_ref[0] * _rms_norm(acc_ref[...], post_g_ref[...])


def _interleave_ffn_chunks(a, d_ff):
    lead = a.shape[:-1]
    a = a.reshape(*lead, 2, d_ff // FFN_CHUNK, FFN_CHUNK)
    return jnp.swapaxes(a, -3, -2).reshape(*lead, 2 * d_ff)


def _conv_ffn(x, mods, mod_row, pre_g, post_g, w_up, conv_w, conv_b, w_down, *, tm_want=512):
    b, t, d = x.shape
    d_ff = w_down.shape[0]
    tm = _row_tile(t, tm_want)
    fc = FFN_CHUNK
    assert d_ff % fc == 0 and tm % HALO == 0
    nj = d_ff // fc
    halo_blocks = tm // HALO
    last_halo = t // HALO - 1
    rows = tm + 2 * HALO

    in_specs = [
        pl.BlockSpec((1, tm, d), lambda bi, i, j: (bi, i, 0)),
        pl.BlockSpec((1, HALO, d), lambda bi, i, j: (bi, jnp.maximum(i * halo_blocks - 1, 0), 0)),
        pl.BlockSpec((1, HALO, d),
                     lambda bi, i, j: (bi, jnp.minimum((i + 1) * halo_blocks, last_halo), 0)),
        pl.BlockSpec((1, 1, d), lambda bi, i, j: (mod_row(bi), 0, 3)),
        pl.BlockSpec((1, 1, d), lambda bi, i, j: (mod_row(bi), 0, 4)),
        pl.BlockSpec((1, 1, d), lambda bi, i, j: (mod_row(bi), 0, 5)),
        pl.BlockSpec((1, d), lambda bi, i, j: (0, 0)),
        pl.BlockSpec((1, d), lambda bi, i, j: (0, 0)),
        pl.BlockSpec((d, 2 * fc), lambda bi, i, j: (0, j)),
        pl.BlockSpec((3, 2 * fc), lambda bi, i, j: (0, j)),
        pl.BlockSpec((1, 2 * fc), lambda bi, i, j: (0, j)),
        pl.BlockSpec((fc, d), lambda bi, i, j: (j, 0)),
    ]
    blocks = (2 * _nbytes((tm, d), F32) + 2 * _nbytes((HALO, d), F32)
              + 3 * _nbytes((d, fc), BF16))
    scratch = (_nbytes((rows, d), BF16) + 2 * _nbytes((rows, fc), F32) + _nbytes((tm, d), F32))
    temps = 3 * _nbytes((tm, fc), F32) + _nbytes((tm, d), F32)
    return pl.pallas_call(
        functools.partial(_ffn_kernel, tm=tm),
        grid=(b, t // tm, nj),
        in_specs=in_specs,
        out_specs=pl.BlockSpec((1, tm, d), lambda bi, i, j: (bi, i, 0)),
        out_shape=jax.ShapeDtypeStruct((b, t, d), F32),
        scratch_shapes=[
            pltpu.VMEM((rows, d), BF16),
            pltpu.VMEM((tm, d), F32),
            pltpu.VMEM((rows, 2 * fc), F32),
        ],
        compiler_params=pltpu.CompilerParams(
            dimension_semantics=("parallel", "parallel", "arbitrary"),
            vmem_limit_bytes=_vmem_limit(blocks, scratch, temps)),
        name="conv_ffn",
    )(x, x, x, mods, mods, mods, pre_g, post_g, w_up, conv_w, conv_b, w_down)


def _rope_tables(t):
    rows = t // GRID_W
    row = jnp.repeat(jnp.arange(rows), GRID_W).astype(F32)
    col = jnp.tile(jnp.arange(GRID_W), rows).astype(F32)
    n_freq = HEAD_DIM // 4
    freqs = ROPE_BASE ** (-jnp.arange(n_freq, dtype=F32) / n_freq)
    ang = jnp.concatenate([row[:, None] * freqs, col[:, None] * freqs], axis=-1)
    cos, sin = jnp.cos(ang), jnp.sin(ang)
    return (jnp.concatenate([cos, cos, cos, cos], axis=-1),
            jnp.concatenate([-sin, sin, -sin, sin], axis=-1))


def _dup_heads(w, n_heads):
    d = w.shape[0]
    w = w.reshape(d, n_heads, 1, HEAD_DIM)
    return jnp.broadcast_to(w, (d, n_heads, 2, HEAD_DIM)).reshape(d, n_heads * PAIR)


def _sink_row(sink, tq):
    return jnp.repeat(sink.astype(F32) * LOG2E, tq, axis=1)[:, None, :]


def kernel(x, c, ctx, c_ctx, ada_w, ada_b, mix_pre_g, mix_post_g, ffn_pre_g, ffn_post_g,
           ab_w_in, ab_w_out, pool_w, pool_scale, sink_logit,
           c_w_qkv, c_w_out, c_q_g, c_k_g,
           ffn_w_up, ffn_conv_w, ffn_conv_b, ffn_w_down):
    batch, t, d = x.shape
    n_ctx = ctx.shape[1]
    depth = ada_w.shape[0]
    pool_ch = pool_scale.shape[1]
    b_kv = sink_logit.shape[1]
    b_q = b_kv * GQA_GROUP * HEAD_DIM
    c_q = c_w_out.shape[1]
    c_kv = c_q // HEAD_DIM // GQA_GROUP
    d_ff = ffn_w_down.shape[1]

    ctx_row = batch
    n_rows = -(-(batch + 1) // SUBLANES) * SUBLANES
    cc = jnp.zeros((n_rows, d), F32).at[:batch].set(c).at[ctx_row].set(c_ctx)
    mods_all = _ada_params(cc, ada_w, ada_b).reshape(depth, n_rows, 1, N_MOD * d)

    rope_tabs = _rope_tables(t)
    ones_bd = jnp.kron(jnp.eye(2, dtype=F32), jnp.ones((HEAD_DIM, HEAD_DIM), F32)).astype(BF16)
    latent_row = lambda bi: bi
    context_row = lambda bi: ctx_row
    win_tq = _row_tile(t, 256)
    ctx_tq = _row_tile(n_ctx, 256)

    def qkv_weights(w, n_lead, n_kv_heads):
        k_end = n_lead + n_kv_heads * HEAD_DIM
        return jnp.concatenate([w[:, :n_lead], _dup_heads(w[:, n_lead:k_end], n_kv_heads),
                                w[:, k_end:]], axis=1).astype(BF16)

    xc = ctx
    for l in range(depth):
        need_ctx = l < depth - 1
        i = l // 2
        mods = mods_all[l]
        pre_g, post_g = mix_pre_g[l][None], mix_post_g[l][None]
        if l % 2 == 0:
            w_in = qkv_weights(ab_w_in[i], pool_ch + b_q, b_kv)
            widths = dict(n_u=pool_ch, n_q=b_q, n_kv=b_kv * PAIR, n_v=b_kv * HEAD_DIM)
            u, q, k2, vt = _inproj(x, mods, latent_row, pre_g, w_in, rope_tabs=rope_tabs,
                                   key_chunk=BAND_CHUNK, **widths)
            uc, qc, kc2, vtc = _inproj(xc, mods, context_row, pre_g, w_in, **widths)
            w_out = ab_w_out[i].astype(BF16)
            w_outs = [w_out[:pool_ch], w_out[pool_ch:]]
            pw, ps = pool_w[i].astype(BF16), pool_scale[i][None]
            a = _window_attention(q, k2, vt, kc2, vtc, _sink_row(sink_logit[i], win_tq),
                                  tq=win_tq)
            x = _outproj([_pool_mix(u, pw, ps), a], w_outs, x, mods, latent_row, post_g)
            if need_ctx:
                ac = _flash_attention(qc, kc2, vtc, _sink_row(sink_logit[i], ctx_tq),
                                      tq_want=ctx_tq)
                xc = _outproj([_pool_mix(uc, pw, ps), ac], w_outs, xc, mods, context_row, post_g)
        else:
            w_in = qkv_weights(c_w_qkv[i], c_q, c_kv)
            gains = (jnp.tile(c_q_g[i], 2)[None], jnp.tile(c_k_g[i], 2)[None], ones_bd)
            widths = dict(n_u=0, n_q=c_q, n_kv=c_kv * PAIR, n_v=c_kv * HEAD_DIM)
            q, k2, vt = _inproj(x, mods, latent_row, pre_g, w_in, rope_tabs=rope_tabs,
                                qk_gains=gains, **widths)
            qc, kc2, vtc = _inproj(xc, mods, context_row, pre_g, w_in, qk_gains=gains, **widths)
            w_outs = [c_w_out[i].astype(BF16)]
            a = _flash_attention(q, jnp.concatenate([k2, kc2], axis=1),
                                 jnp.concatenate([vt, vtc], axis=1))
            x = _outproj([a], w_outs, x, mods, latent_row, post_g)
            if need_ctx:
                ac = _flash_attention(qc, kc2, vtc)
                xc = _outproj([ac], w_outs, xc, mods, context_row, post_g)

        ffn_args = (ffn_pre_g[l][None], ffn_post_g[l][None],
                    _interleave_ffn_chunks(ffn_w_up[l], d_ff).astype(BF16),
                    _interleave_ffn_chunks(ffn_conv_w[l], d_ff),
                    _interleave_ffn_chunks(ffn_conv_b[l][None], d_ff),
                    ffn_w_down[l].astype(BF16))
        x = _conv_ffn(x, mods, latent_row, *ffn_args)
        if need_ctx:
            xc = _conv_ffn(xc, mods, context_row, *ffn_args)
    return x
```

```python
import functools

import jax
import jax.numpy as jnp
from jax import lax
from jax.experimental import pallas as pl
from jax.experimental.pallas import tpu as pltpu

F32 = jnp.float32
BF16 = jnp.bfloat16

HEAD_DIM = 64
GQA_GROUP = 4
POOL_WINDOWS = (2, 4, 8, 16)
POOL_GC = 128
WINDOW = 128
GRID_W = 64
ROPE_BASE = 10000.0
EPS = 1e-6
NEG = -1e30
N_MOD = 6

LANES = 128
SUBLANES = 8
VMEM_LIMIT_CAP = 56 * 1024 * 1024
VMEM_LIMIT_FLOOR = 32 * 1024 * 1024

PAIR = 2 * HEAD_DIM
GROUP_COLS = GQA_GROUP * HEAD_DIM
HALO = SUBLANES
KEY_CHUNK = 256
BAND_CHUNK = WINDOW
LOG2E = 1.4426950408889634
Q_SCALE = HEAD_DIM ** -0.5 * LOG2E
ONES_ROWS = 16
VT_ROWS = HEAD_DIM + ONES_ROWS
FFN_CHUNK = 512


def _vmem_limit(block_bytes, scratch_bytes=0, temp_bytes=0):
    need = 2 * block_bytes + scratch_bytes + temp_bytes
    return int(min(max(need + need // 4, VMEM_LIMIT_FLOOR), VMEM_LIMIT_CAP))


def _nbytes(shape, dtype):
    n = 1
    for s in shape:
        n *= s
    return n * jnp.dtype(dtype).itemsize


def _row_tile(t, want):
    tile = min(t, want)
    assert t % tile == 0
    return tile


def _ada_kernel(c_ref, w_ref, b_ref, o_ref):
    c = c_ref[...]
    a = (c * jax.nn.sigmoid(c)).astype(BF16)
    o_ref[0] = jnp.dot(a, w_ref[0].astype(BF16), preferred_element_type=F32) + b_ref[0]


def _ada_params(cc, ada_w, ada_b):
    depth, d, n = ada_w.shape
    rows = cc.shape[0]
    tn = n // N_MOD
    blocks = _nbytes((rows, d), F32) + _nbytes((d, tn), F32) + _nbytes((rows, tn), F32)
    return pl.pallas_call(
        _ada_kernel,
        grid=(depth, n // tn),
        in_specs=[
            pl.BlockSpec((rows, d), lambda l, j: (0, 0)),
            pl.BlockSpec((1, d, tn), lambda l, j: (l, 0, j)),
            pl.BlockSpec((1, 1, tn), lambda l, j: (l, 0, j)),
        ],
        out_specs=pl.BlockSpec((1, rows, tn), lambda l, j: (l, 0, j)),
        out_shape=jax.ShapeDtypeStruct((depth, rows, n), F32),
        compiler_params=pltpu.CompilerParams(
            dimension_semantics=("parallel", "parallel"),
            vmem_limit_bytes=_vmem_limit(blocks, temp_bytes=_nbytes((d, tn), BF16))),
        name="ada_params",
    )(cc, ada_w, ada_b.reshape(depth, 1, n))


def _rms_norm(x, g):
    return x * lax.rsqrt(jnp.mean(x * x, axis=-1, keepdims=True) + EPS) * g


def _pre_mod(x, g, shift, scale):
    return _rms_norm(x, g) * (1.0 + scale) + shift


def _inproj_kernel(*refs, n_u, n_q, n_kv, n_v, key_chunk, rope, qk_norm, tm):
    x_ref, sh_ref, sc_ref, g_ref, w_ref = refs[:5]
    pos = 5
    if rope:
        cos_ref, sin_ref = refs[pos:pos + 2]
        pos += 2
    if qk_norm:
        qg_ref, kg_ref, ones_ref = refs[pos:pos + 3]
        pos += 3
    out_refs = refs[pos:]

    h = _pre_mod(x_ref[0], g_ref[...], sh_ref[0], sc_ref[0]).astype(BF16)

    if rope:
        lane = lax.broadcasted_iota(jnp.int32, (tm, PAIR), 1)
        first_half = (lane & (HEAD_DIM - 1)) < HEAD_DIM // 2
        cos = cos_ref[...]
        sin = sin_ref[...]

    def head_epilogue(a, gain_ref, scale):
        if qk_norm:
            a2 = a * a
            hi = a2.astype(BF16)
            lo = (a2 - hi.astype(F32)).astype(BF16)
            ss = (jnp.dot(hi, ones_ref[...], preferred_element_type=F32)
                  + jnp.dot(lo, ones_ref[...], preferred_element_type=F32))
            a = a * lax.rsqrt(ss * (1.0 / HEAD_DIM) + EPS) * gain_ref[...]
        if rope:
            partner = jnp.where(first_half,
                                pltpu.roll(a, PAIR - HEAD_DIM // 2, 1),
                                pltpu.roll(a, HEAD_DIM // 2, 1))
            a = a * cos + partner * sin
        if scale != 1.0:
            a = a * scale
        return a.astype(BF16)

    col = 0
    out_idx = 0
    if n_u:
        out_refs[out_idx][0] = jnp.dot(h, w_ref[:, col:col + n_u], preferred_element_type=F32)
        col += n_u
        out_idx += 1

    acc = jnp.dot(h, w_ref[:, col:col + n_q], preferred_element_type=F32)
    for c in range(n_q // PAIR):
        out_refs[out_idx][0, :, c * PAIR:(c + 1) * PAIR] = head_epilogue(
            acc[:, c * PAIR:(c + 1) * PAIR], qg_ref if qk_norm else None, Q_SCALE)
    col += n_q
    out_idx += 1

    acc = jnp.dot(h, w_ref[:, col:col + n_kv], preferred_element_type=F32)
    for c in range(n_kv // PAIR):
        out_refs[out_idx][0, :, c * PAIR:(c + 1) * PAIR] = head_epilogue(
            acc[:, c * PAIR:(c + 1) * PAIR], kg_ref if qk_norm else None, 1.0)
    col += n_kv
    out_idx += 1

    acc = jnp.dot(h, w_ref[:, col:col + n_v], preferred_element_type=F32)
    ones = jnp.ones((ONES_ROWS, key_chunk), F32)
    for cc in range(tm // key_chunk):
        at = acc[cc * key_chunk:(cc + 1) * key_chunk, :].T
        pieces = []
        for j in range(n_v // HEAD_DIM):
            pieces += [at[j * HEAD_DIM:(j + 1) * HEAD_DIM], ones]
        out_refs[out_idx][0, cc] = jnp.concatenate(pieces, axis=0).astype(BF16)


def _inproj(x, mods, mod_row, pre_g, w, *, n_u, n_q, n_kv, n_v, key_chunk=KEY_CHUNK,
            rope_tabs=None, qk_gains=None):
    b, t, d = x.shape
    n = w.shape[1]
    tm = _row_tile(t, 512)
    rope = rope_tabs is not None
    qk_norm = qk_gains is not None
    vt_rows = n_v // HEAD_DIM * VT_ROWS
    assert tm % key_chunk == 0

    in_specs = [
        pl.BlockSpec((1, tm, d), lambda bi, i: (bi, i, 0)),
        pl.BlockSpec((1, 1, d), lambda bi, i: (mod_row(bi), 0, 0)),
        pl.BlockSpec((1, 1, d), lambda bi, i: (mod_row(bi), 0, 1)),
        pl.BlockSpec((1, d), lambda bi, i: (0, 0)),
        pl.BlockSpec((d, n), lambda bi, i: (0, 0)),
    ]
    args = [x, mods, mods, pre_g, w]
    if rope:
        in_specs += [pl.BlockSpec((tm, PAIR), lambda bi, i: (i, 0))] * 2
        args += list(rope_tabs)
    if qk_norm:
        in_specs += [pl.BlockSpec((1, PAIR), lambda bi, i: (0, 0))] * 2
        in_specs += [pl.BlockSpec((PAIR, PAIR), lambda bi, i: (0, 0))]
        args += list(qk_gains)

    out_shape, out_specs = [], []
    for width, dtype in ((n_u, F32), (n_q, BF16), (n_kv, BF16)):
        if width:
            out_shape.append(jax.ShapeDtypeStruct((b, t, width), dtype))
            out_specs.append(pl.BlockSpec((1, tm, width), lambda bi, i: (bi, i, 0)))
    out_shape.append(jax.ShapeDtypeStruct((b, t // key_chunk, vt_rows, key_chunk), BF16))
    out_specs.append(pl.BlockSpec((1, tm // key_chunk, vt_rows, key_chunk),
                                  lambda bi, i: (bi, i, 0, 0)))

    blocks = (_nbytes((tm, d), F32) + _nbytes((d, n), BF16) + _nbytes((tm, n_u), F32)
              + _nbytes((tm, n_q + n_kv + 2 * n_v), BF16) + 2 * _nbytes((tm, PAIR), F32))
    temps = _nbytes((tm, d), F32) * 2 + _nbytes((tm, max(n_q, n_kv, n_u)), F32) * 2
    return pl.pallas_call(
        functools.partial(_inproj_kernel, n_u=n_u, n_q=n_q, n_kv=n_kv, n_v=n_v,
                          key_chunk=key_chunk, rope=rope, qk_norm=qk_norm, tm=tm),
        grid=(b, t // tm),
        in_specs=in_specs,
        out_specs=out_specs,
        out_shape=out_shape,
        compiler_params=pltpu.CompilerParams(
            dimension_semantics=("parallel", "parallel"),
            vmem_limit_bytes=_vmem_limit(blocks, temp_bytes=temps)),
        name="mixer_inproj",
    )(*args)


def _stack_heads_transposed(q_ref, qt_ref, tq):
    sub = lax.broadcasted_iota(jnp.int32, (PAIR, tq), 0)
    top = sub < HEAD_DIM
    for p in range(GROUP_COLS // PAIR):
        pair_t = q_ref[0, :, p * PAIR:(p + 1) * PAIR].astype(F32).T
        qt_ref[:, (2 * p) * tq:(2 * p + 1) * tq] = jnp.where(top, pair_t, 0.0).astype(BF16)
        qt_ref[:, (2 * p + 1) * tq:(2 * p + 2) * tq] = jnp.where(top, 0.0, pair_t).astype(BF16)


def _store_heads(o_t, o_ref, tq):
    for p in range(GROUP_COLS // PAIR):
        pair_t = jnp.concatenate([o_t[:, (2 * p) * tq:(2 * p + 1) * tq],
                                  o_t[:, (2 * p + 1) * tq:(2 * p + 2) * tq]], axis=0)
        o_ref[0, :, p * PAIR:(p + 1) * PAIR] = pair_t.T.astype(o_ref.dtype)


def _flash_kernel(*refs, tq, n_chunks, has_sink):
    if has_sink:
        q_ref, k_ref, vt_ref, sink_ref, o_ref, qt_ref, s_ref, m_ref, acc_ref = refs
    else:
        q_ref, k_ref, vt_ref, o_ref, qt_ref, s_ref, m_ref, acc_ref = refs

    _stack_heads_transposed(q_ref, qt_ref, tq)
    if has_sink:
        m_ref[...] = sink_ref[0]
        den_rows = lax.broadcasted_iota(jnp.int32, acc_ref.shape, 0) >= HEAD_DIM
        acc_ref[...] = jnp.where(den_rows, 1.0, 0.0)
    else:
        m_ref[...] = jnp.full_like(m_ref, NEG)
        acc_ref[...] = jnp.zeros_like(acc_ref)

    def scores(c):
        start = pl.multiple_of(c * KEY_CHUNK, KEY_CHUNK)
        return jnp.dot(k_ref[0, pl.ds(start, KEY_CHUNK), :], qt_ref[...],
                       preferred_element_type=F32)

    def consume(slot, c):
        s = s_ref[slot]
        m_prev = m_ref[...]
        m_new = jnp.maximum(m_prev, jnp.max(s, axis=0, keepdims=True))
        alpha = jnp.exp2(m_prev - m_new)
        p = jnp.exp2(s - m_new)
        acc_ref[...] = alpha * acc_ref[...] + jnp.dot(vt_ref[0, c], p.astype(BF16),
                                                     preferred_element_type=F32)
        m_ref[...] = m_new

    s_ref[0] = scores(0)

    def body(pair, carry):
        c = 2 * pair
        s_ref[1] = scores(c + 1)
        consume(0, c)
        s_ref[0] = scores(c + 2)
        consume(1, c + 1)
        return carry
    n_pairs = n_chunks // 2
    lax.fori_loop(0, n_pairs, body, 0, unroll=2 if n_pairs % 2 == 0 else 1)
    consume(0, n_chunks - 1)

    _store_heads(acc_ref[0:HEAD_DIM, :] / acc_ref[HEAD_DIM:HEAD_DIM + 1, :], o_ref, tq)


def _flash_attention(q, k2, vt, sink_row=None, *, tq_want=256):
    b, t, hq = q.shape
    n_chunks = vt.shape[1]
    n_groups = hq // GROUP_COLS
    tq = _row_tile(t, tq_want)
    cols = GQA_GROUP * tq
    has_sink = sink_row is not None
    assert n_chunks % 2 == 1 and k2.shape[1] == n_chunks * KEY_CHUNK

    in_specs = [
        pl.BlockSpec((1, tq, GROUP_COLS), lambda bi, j, i: (bi, i, j)),
        pl.BlockSpec((1, n_chunks * KEY_CHUNK, PAIR), lambda bi, j, i: (bi, 0, j)),
        pl.BlockSpec((1, n_chunks, VT_ROWS, KEY_CHUNK), lambda bi, j, i: (bi, 0, j, 0)),
    ]
    args = [q, k2, vt]
    if has_sink:
        in_specs.append(pl.BlockSpec((1, 1, cols), lambda bi, j, i: (j, 0, 0)))
        args.append(sink_row)
    blocks = (2 * _nbytes((tq, GROUP_COLS), BF16)
              + _nbytes((n_chunks * KEY_CHUNK, PAIR + VT_ROWS), BF16))
    scratch = (_nbytes((PAIR, cols), BF16) + _nbytes((2, KEY_CHUNK, cols), F32)
               + _nbytes((VT_ROWS + SUBLANES, cols), F32))
    temps = 2 * _nbytes((KEY_CHUNK, cols), F32)
    return pl.pallas_call(
        functools.partial(_flash_kernel, tq=tq, n_chunks=n_chunks, has_sink=has_sink),
        grid=(b, n_groups, t // tq),
        in_specs=in_specs,
        out_specs=pl.BlockSpec((1, tq, GROUP_COLS), lambda bi, j, i: (bi, i, j)),
        out_shape=jax.ShapeDtypeStruct((b, t, hq), BF16),
        scratch_shapes=[
            pltpu.VMEM((PAIR, cols), BF16),
            pltpu.VMEM((2, KEY_CHUNK, cols), F32),
            pltpu.VMEM((1, cols), F32),
            pltpu.VMEM((VT_ROWS, cols), F32),
        ],
        compiler_params=pltpu.CompilerParams(
            dimension_semantics=("parallel", "parallel", "parallel"),
            vmem_limit_bytes=_vmem_limit(blocks, scratch, temps)),
        name="flash_attention",
    )(*args)


def _window_kernel(q_ref, k_ref, vt_ref, kc_ref, vtc_ref, bias_ref, sink_ref, o_ref, qt_ref,
                   *, tq, t):
    span = tq + 2 * WINDOW
    q0 = pl.program_id(2) * tq
    start = pl.multiple_of(jnp.clip(q0 - WINDOW, 0, t - span), WINDOW)
    bias = bias_ref[(q0 - start) // WINDOW]
    _stack_heads_transposed(q_ref, qt_ref, tq)
    qt = qt_ref[...]

    s_band = jnp.dot(k_ref[0, pl.ds(start, span), :], qt, preferred_element_type=F32)
    s_band = s_band + jnp.concatenate([bias] * GQA_GROUP, axis=1)
    s_ctx = jnp.dot(kc_ref[0], qt, preferred_element_type=F32)

    sink = sink_ref[0]
    m = jnp.maximum(jnp.maximum(jnp.max(s_band, axis=0, keepdims=True),
                                jnp.max(s_ctx, axis=0, keepdims=True)), sink)
    p_band = jnp.exp2(s_band - m).astype(BF16)
    p_ctx = jnp.exp2(s_ctx - m).astype(BF16)
    first = start // BAND_CHUNK
    vt_band = jnp.concatenate([vt_ref[0, first + c] for c in range(span // BAND_CHUNK)], axis=1)
    acc = (jnp.dot(vt_band, p_band, preferred_element_type=F32)
           + jnp.dot(vtc_ref[0, 0], p_ctx, preferred_element_type=F32))
    den = acc[HEAD_DIM:HEAD_DIM + 1, :] + jnp.exp2(sink - m)
    _store_heads(acc[0:HEAD_DIM, :] / den, o_ref, tq)


def _band_bias(tq):
    span = tq + 2 * WINDOW
    key = jnp.arange(span)[None, :, None]
    qry = jnp.arange(tq)[None, None, :]
    back = (jnp.arange(3) * WINDOW)[:, None, None]
    delta = qry + back - key
    return jnp.where(jnp.abs(delta) <= WINDOW, 0.0, NEG).astype(F32)


def _window_attention(q, k2, vt, kc2, vtc, sink_row, *, tq):
    b, t, hq = q.shape
    n_ctx = kc2.shape[1]
    n_groups = hq // GROUP_COLS
    cols = GQA_GROUP * tq
    span = tq + 2 * WINDOW
    n_band = t // BAND_CHUNK
    assert t % tq == 0 and t >= span and tq % WINDOW == 0 and vtc.shape[1] == 1
    blocks = (2 * _nbytes((tq, GROUP_COLS), BF16) + _nbytes((t, PAIR + VT_ROWS), BF16)
              + _nbytes((n_ctx, PAIR + VT_ROWS), BF16) + _nbytes((3, span, tq), F32))
    temps = 3 * _nbytes((span + n_ctx, cols), F32)
    return pl.pallas_call(
        functools.partial(_window_kernel, tq=tq, t=t),
        grid=(b, n_groups, t // tq),
        in_specs=[
            pl.BlockSpec((1, tq, GROUP_COLS), lambda bi, j, i: (bi, i, j)),
            pl.BlockSpec((1, t, PAIR), lambda bi, j, i: (bi, 0, j)),
            pl.BlockSpec((1, n_band, VT_ROWS, BAND_CHUNK), lambda bi, j, i: (bi, 0, j, 0)),
            pl.BlockSpec((1, n_ctx, PAIR), lambda bi, j, i: (bi, 0, j)),
            pl.BlockSpec((1, 1, VT_ROWS, n_ctx), lambda bi, j, i: (bi, 0, j, 0)),
            pl.BlockSpec((3, span, tq), lambda bi, j, i: (0, 0, 0)),
            pl.BlockSpec((1, 1, cols), lambda bi, j, i: (j, 0, 0)),
        ],
        out_specs=pl.BlockSpec((1, tq, GROUP_COLS), lambda bi, j, i: (bi, i, j)),
        out_shape=jax.ShapeDtypeStruct((b, t, hq), BF16),
        scratch_shapes=[pltpu.VMEM((PAIR, cols), BF16)],
        compiler_params=pltpu.CompilerParams(
            dimension_semantics=("parallel", "parallel", "parallel"),
            vmem_limit_bytes=_vmem_limit(blocks, _nbytes((PAIR, cols), BF16), temps)),
        name="window_attention",
    )(q, k2, vt, kc2, vtc, _band_bias(tq), sink_row)


def _pool_kernel(u_ref, w_ref, scale_ref, o_ref, pad_ref, *, t):
    reach = max(POOL_WINDOWS) // 2
    assert reach <= HALO
    pos = lax.broadcasted_iota(jnp.int32, (t, POOL_GC), 0)
    zeros = jnp.zeros((HALO, POOL_GC), F32)
    pad_ref[0:HALO, :] = zeros
    pad_ref[HALO + t:2 * HALO + t, :] = zeros
    for g, win in enumerate(POOL_WINDOWS):
        half = win // 2
        u = u_ref[0, :, g * POOL_GC:(g + 1) * POOL_GC]
        pad_ref[HALO:HALO + t, :] = u
        total = pad_ref[pl.ds(HALO - half, t), :]
        for off in range(1 - half, half):
            total = total + pad_ref[pl.ds(HALO + off, t), :]
        cnt = (jnp.minimum(pos + half, t) - jnp.maximum(pos - half, 0)).astype(F32)
        d = (total / cnt - u).astype(BF16)
        y = jnp.dot(d, w_ref[g], preferred_element_type=F32)
        o_ref[0, :, g * POOL_GC:(g + 1) * POOL_GC] = (
            y * scale_ref[:, g * POOL_GC:(g + 1) * POOL_GC]).astype(BF16)


def _pool_mix(u, pool_w, pool_scale):
    b, t, ch = u.shape
    n_g = len(POOL_WINDOWS)
    assert ch == n_g * POOL_GC
    blocks = _nbytes((t, ch), F32) + _nbytes((t, ch), BF16) + _nbytes((n_g, POOL_GC, POOL_GC), BF16)
    scratch = _nbytes((t + 2 * HALO, POOL_GC), F32)
    return pl.pallas_call(
        functools.partial(_pool_kernel, t=t),
        grid=(b,),
        in_specs=[
            pl.BlockSpec((1, t, ch), lambda bi: (bi, 0, 0)),
            pl.BlockSpec((n_g, POOL_GC, POOL_GC), lambda bi: (0, 0, 0)),
            pl.BlockSpec((1, ch), lambda bi: (0, 0)),
        ],
        out_specs=pl.BlockSpec((1, t, ch), lambda bi: (bi, 0, 0)),
        out_shape=jax.ShapeDtypeStruct((b, t, ch), BF16),
        scratch_shapes=[pltpu.VMEM((t + 2 * HALO, POOL_GC), F32)],
        compiler_params=pltpu.CompilerParams(
            dimension_semantics=("parallel",),
            vmem_limit_bytes=_vmem_limit(blocks, scratch, 6 * _nbytes((t, POOL_GC), F32))),
        name="pool_mix",
    )(u, pool_w, pool_scale)


def _outproj_kernel(*refs, n_in):
    a_refs = refs[:n_in]
    w_refs = refs[n_in:2 * n_in]
    x_ref, gate_ref, g_ref, o_ref = refs[2 * n_in:]
    y = jnp.dot(a_refs[0][0], w_refs[0][...], preferred_element_type=F32)
    for a_ref, w_ref in zip(a_refs[1:], w_refs[1:]):
        y = y + jnp.dot(a_ref[0], w_ref[...], preferred_element_type=F32)
    o_ref[0] = x_ref[0] + gate_ref[0] * _rms_norm(y, g_ref[...])


def _outproj(acts, weights, x, mods, mod_row, post_g):
    b, t, d = x.shape
    tm = _row_tile(t, 512)
    n_in = len(acts)
    in_specs = [pl.BlockSpec((1, tm, a.shape[2]), lambda bi, i: (bi, i, 0)) for a in acts]
    in_specs += [pl.BlockSpec(w.shape, lambda bi, i: (0, 0)) for w in weights]
    in_specs += [
        pl.BlockSpec((1, tm, d), lambda bi, i: (bi, i, 0)),
        pl.BlockSpec((1, 1, d), lambda bi, i: (mod_row(bi), 0, 2)),
        pl.BlockSpec((1, d), lambda bi, i: (0, 0)),
    ]
    blocks = (sum(_nbytes((tm, a.shape[2]), BF16) for a in acts)
              + sum(_nbytes(w.shape, BF16) for w in weights) + 2 * _nbytes((tm, d), F32))
    return pl.pallas_call(
        functools.partial(_outproj_kernel, n_in=n_in),
        grid=(b, t // tm),
        in_specs=in_specs,
        out_specs=pl.BlockSpec((1, tm, d), lambda bi, i: (bi, i, 0)),
        out_shape=jax.ShapeDtypeStruct((b, t, d), F32),
        compiler_params=pltpu.CompilerParams(
            dimension_semantics=("parallel", "parallel"),
            vmem_limit_bytes=_vmem_limit(blocks, temp_bytes=3 * _nbytes((tm, d), F32))),
        name="mixer_outproj",
    )(*acts, *weights, x, mods, post_g)


def _ffn_kernel(x_ref, xp_ref, xn_ref, sh_ref, sc_ref, gate_ref, pre_g_ref, post_g_ref,
                wu_ref, wut_ref, cw_ref, cwt_ref, cb_ref, cbt_ref, wd_ref, wdt_ref, o_ref,
                h_ref, acc_ref, u_ref, ut_ref, *, tm, n_main):
    i = pl.program_id(1)
    g, sh, sc = pre_g_ref[...], sh_ref[0], sc_ref[0]
    h_prev = _pre_mod(xp_ref[0], g, sh, sc)
    h_next = _pre_mod(xn_ref[0], g, sh, sc)
    h_prev = jnp.where(i == 0, 0.0, h_prev)
    h_next = jnp.where(i == pl.num_programs(1) - 1, 0.0, h_next)
    h_ref[0:HALO, :] = h_prev.astype(BF16)
    h_ref[HALO:HALO + tm, :] = _pre_mod(x_ref[0], g, sh, sc).astype(BF16)
    h_ref[HALO + tm:2 * HALO + tm, :] = h_next.astype(BF16)

    def up(c):
        return jnp.dot(h_ref[...], wu_ref[c], preferred_element_type=F32)

    def conv_down(u, cw, cb, wd):
        w = wd.shape[0]
        conv = (u[pl.ds(HALO - 1, tm), :] * cw[0:1, :]
                + u[pl.ds(HALO, tm), :] * cw[1:2, :]
                + u[pl.ds(HALO + 1, tm), :] * cw[2:3, :]
                + cb)
        gated = jax.nn.gelu(conv[:, :w]) * conv[:, w:]
        return jnp.dot(gated.astype(BF16), wd, preferred_element_type=F32)

    def consume(slot, c):
        return conv_down(u_ref.at[slot], cw_ref[c], cb_ref[c], wd_ref[c])

    u_ref[0] = up(0)
    acc_ref[...] = jnp.zeros_like(acc_ref)

    def body(pair, carry):
        c = 2 * pair
        u_ref[1] = up(c + 1)
        acc_ref[...] += consume(0, c)
        u_ref[0] = up(c + 2)
        acc_ref[...] += consume(1, c + 1)
        return carry
    lax.fori_loop(0, (n_main - 1) // 2, body, 0)
    last = n_main - 1
    if n_main % 2 == 0:
        u_ref[1] = up(last)
        acc_ref[...] += consume(0, last - 1)
    ut_ref[...] = jnp.dot(h_ref[...], wut_ref[...], preferred_element_type=F32)
    acc_ref[...] += consume(last % 2, last)
    y = acc_ref[...] + conv_down(ut_ref, cwt_ref[...], cbt_ref[...], wdt_ref[...])
    o_ref[0] = x_ref[0] + gate_ref[0] * _rms_norm(y, post_g_ref[...])


def _ffn_weights(w_up, conv_w, conv_b, w_down):
    d_ff = w_down.shape[0]
    n_main = d_ff // FFN_CHUNK
    split = n_main * FFN_CHUNK

    def pair_cols(a):
        r = a.shape[0]
        gelu, gate = a[:, :d_ff], a[:, d_ff:]
        main = jnp.concatenate([gelu[:, :split].reshape(r, n_main, FFN_CHUNK),
                                gate[:, :split].reshape(r, n_main, FFN_CHUNK)], axis=2)
        tail = jnp.concatenate([gelu[:, split:], gate[:, split:]], axis=1)
        return jnp.swapaxes(main, 0, 1), tail

    wu, wut = pair_cols(w_up.astype(BF16))
    cw, cwt = pair_cols(conv_w)
    cb, cbt = pair_cols(conv_b[None])
    wd = w_down.astype(BF16)
    return (wu, wut, cw, cwt, cb, cbt, wd[:split].reshape(n_main, FFN_CHUNK, -1), wd[split:])


def _conv_ffn(x, mods, mod_row, pre_g, post_g, weights, *, tm_want=512):
    b, t, d = x.shape
    tm = _row_tile(t, tm_want)
    n_main = weights[0].shape[0]
    tail = weights[-1].shape[0]
    assert tm % HALO == 0 and n_main >= 2 and tail > 0
    halo_blocks = tm // HALO
    last_halo = t // HALO - 1
    rows = tm + 2 * HALO

    def resident(a):
        zeros = (0,) * a.ndim
        return pl.BlockSpec(a.shape, lambda bi, i: zeros, pipeline_mode=pl.Buffered(1))

    in_specs = [
        pl.BlockSpec((1, tm, d), lambda bi, i: (bi, i, 0)),
        pl.BlockSpec((1, HALO, d), lambda bi, i: (bi, jnp.maximum(i * halo_blocks - 1, 0), 0)),
        pl.BlockSpec((1, HALO, d),
                     lambda bi, i: (bi, jnp.minimum((i + 1) * halo_blocks, last_halo), 0)),
        pl.BlockSpec((1, 1, d), lambda bi, i: (mod_row(bi), 0, 3)),
        pl.BlockSpec((1, 1, d), lambda bi, i: (mod_row(bi), 0, 4)),
        pl.BlockSpec((1, 1, d), lambda bi, i: (mod_row(bi), 0, 5)),
        pl.BlockSpec((1, d), lambda bi, i: (0, 0)),
        pl.BlockSpec((1, d), lambda bi, i: (0, 0)),
    ] + [resident(a) for a in weights]
    weight_bytes = sum(_nbytes(a.shape, a.dtype) for a in weights)
    blocks = 2 * _nbytes((tm, d), F32) + 2 * _nbytes((HALO, d), F32)
    scratch = (_nbytes((rows, d), BF16) + _nbytes((tm, d), F32)
               + _nbytes((2, rows, 2 * FFN_CHUNK), F32) + _nbytes((rows, 2 * tail), F32)
               + weight_bytes)
    temps = 4 * _nbytes((tm, 2 * FFN_CHUNK), F32) + _nbytes((tm, d), F32)
    return pl.pallas_call(
        functools.partial(_ffn_kernel, tm=tm, n_main=n_main),
        grid=(b, t // tm),
        in_specs=in_specs,
        out_specs=pl.BlockSpec((1, tm, d), lambda bi, i: (bi, i, 0)),
        out_shape=jax.ShapeDtypeStruct((b, t, d), F32),
        scratch_shapes=[
            pltpu.VMEM((rows, d), BF16),
            pltpu.VMEM((tm, d), F32),
            pltpu.VMEM((2, rows, 2 * FFN_CHUNK), F32),
            pltpu.VMEM((rows, 2 * tail), F32),
        ],
        compiler_params=pltpu.CompilerParams(
            dimension_semantics=("parallel", "parallel"),
            vmem_limit_bytes=_vmem_limit(blocks, scratch, temps)),
        name="conv_ffn",
    )(x, x, x, mods, mods, mods, pre_g, post_g, *weights)


def _rope_tables(t):
    rows = t // GRID_W
    row = jnp.repeat(jnp.arange(rows), GRID_W).astype(F32)
    col = jnp.tile(jnp.arange(GRID_W), rows).astype(F32)
    n_freq = HEAD_DIM // 4
    freqs = ROPE_BASE ** (-jnp.arange(n_freq, dtype=F32) / n_freq)
    ang = jnp.concatenate([row[:, None] * freqs, col[:, None] * freqs], axis=-1)
    cos, sin = jnp.cos(ang), jnp.sin(ang)
    return (jnp.concatenate([cos, cos, cos, cos], axis=-1),
            jnp.concatenate([-sin, sin, -sin, sin], axis=-1))


def _dup_heads(w, n_heads):
    d = w.shape[0]
    w = w.reshape(d, n_heads, 1, HEAD_DIM)
    return jnp.broadcast_to(w, (d, n_heads, 2, HEAD_DIM)).reshape(d, n_heads * PAIR)


def _sink_row(sink, tq):
    return jnp.repeat(sink.astype(F32) * LOG2E, tq, axis=1)[:, None, :]


def kernel(x, c, ctx, c_ctx, ada_w, ada_b, mix_pre_g, mix_post_g, ffn_pre_g, ffn_post_g,
           ab_w_in, ab_w_out, pool_w, pool_scale, sink_logit,
           c_w_qkv, c_w_out, c_q_g, c_k_g,
           ffn_w_up, ffn_conv_w, ffn_conv_b, ffn_w_down):
    batch, t, d = x.shape
    n_ctx = ctx.shape[1]
    depth = ada_w.shape[0]
    pool_ch = pool_scale.shape[1]
    b_kv = sink_logit.shape[1]
    b_q = b_kv * GQA_GROUP * HEAD_DIM
    c_q = c_w_out.shape[1]
    c_kv = c_q // HEAD_DIM // GQA_GROUP
    d_ff = ffn_w_down.shape[1]

    ctx_row = batch
    n_rows = -(-(batch + 1) // SUBLANES) * SUBLANES
    cc = jnp.zeros((n_rows, d), F32).at[:batch].set(c).at[ctx_row].set(c_ctx)
    mods_all = _ada_params(cc, ada_w, ada_b).reshape(depth, n_rows, 1, N_MOD * d)

    rope_tabs = _rope_tables(t)
    ones_bd = jnp.kron(jnp.eye(2, dtype=F32), jnp.ones((HEAD_DIM, HEAD_DIM), F32)).astype(BF16)
    latent_row = lambda bi: bi
    context_row = lambda bi: ctx_row
    win_tq = _row_tile(t, 256)
    ctx_tq = _row_tile(n_ctx, 256)

    def qkv_weights(w, n_lead, n_kv_heads):
        k_end = n_lead + n_kv_heads * HEAD_DIM
        return jnp.concatenate([w[:, :n_lead], _dup_heads(w[:, n_lead:k_end], n_kv_heads),
                                w[:, k_end:]], axis=1).astype(BF16)

    xc = ctx
    for l in range(depth):
        need_ctx = l < depth - 1
        i = l // 2
        mods = mods_all[l]
        pre_g, post_g = mix_pre_g[l][None], mix_post_g[l][None]
        if l % 2 == 0:
            w_in = qkv_weights(ab_w_in[i], pool_ch + b_q, b_kv)
            widths = dict(n_u=pool_ch, n_q=b_q, n_kv=b_kv * PAIR, n_v=b_kv * HEAD_DIM)
            u, q, k2, vt = _inproj(x, mods, latent_row, pre_g, w_in, rope_tabs=rope_tabs,
                                   key_chunk=BAND_CHUNK, **widths)
            uc, qc, kc2, vtc = _inproj(xc, mods, context_row, pre_g, w_in, **widths)
            w_out = ab_w_out[i].astype(BF16)
            w_outs = [w_out[:pool_ch], w_out[pool_ch:]]
            pw, ps = pool_w[i].astype(BF16), pool_scale[i][None]
            a = _window_attention(q, k2, vt, kc2, vtc, _sink_row(sink_logit[i], win_tq),
                                  tq=win_tq)
            x = _outproj([_pool_mix(u, pw, ps), a], w_outs, x, mods, latent_row, post_g)
            if need_ctx:
                ac = _flash_attention(qc, kc2, vtc, _sink_row(sink_logit[i], ctx_tq),
                                      tq_want=ctx_tq)
                xc = _outproj([_pool_mix(uc, pw, ps), ac], w_outs, xc, mods, context_row, post_g)
        else:
            w_in = qkv_weights(c_w_qkv[i], c_q, c_kv)
            gains = (jnp.tile(c_q_g[i], 2)[None], jnp.tile(c_k_g[i], 2)[None], ones_bd)
            widths = dict(n_u=0, n_q=c_q, n_kv=c_kv * PAIR, n_v=c_kv * HEAD_DIM)
            q, k2, vt = _inproj(x, mods, latent_row, pre_g, w_in, rope_tabs=rope_tabs,
                                qk_gains=gains, **widths)
            qc, kc2, vtc = _inproj(xc, mods, context_row, pre_g, w_in, qk_gains=gains, **widths)
            w_outs = [c_w_out[i].astype(BF16)]
            a = _flash_attention(q, jnp.concatenate([k2, kc2], axis=1),
                                 jnp.concatenate([vt, vtc], axis=1))
            x = _outproj([a], w_outs, x, mods, latent_row, post_g)
            if need_ctx:
                ac = _flash_attention(qc, kc2, vtc)
                xc = _outproj([ac], w_outs, xc, mods, context_row, post_g)

        ffn_args = (ffn_pre_g[l][None], ffn_post_g[l][None],
                    _ffn_weights(ffn_w_up[l], ffn_conv_w[l], ffn_conv_b[l], ffn_w_down[l]))
        x = _conv_ffn(x, mods, latent_row, *ffn_args)
        if need_ctx:
            xc = _conv_ffn(xc, mods, context_row, *ffn_args)
    return x
```

```python
import functools

import jax
import jax.numpy as jnp
from jax import lax
from jax.experimental import pallas as pl
from jax.experimental.pallas import tpu as pltpu

F32 = jnp.float32
BF16 = jnp.bfloat16

HEAD_DIM = 64
GQA_GROUP = 4
POOL_WINDOWS = (2, 4, 8, 16)
POOL_GC = 128
WINDOW = 128
GRID_W = 64
ROPE_BASE = 10000.0
EPS = 1e-6
NEG = -1e30
N_MOD = 6

LANES = 128
SUBLANES = 8
VMEM_LIMIT_CAP = 56 * 1024 * 1024
VMEM_LIMIT_FLOOR = 32 * 1024 * 1024

PAIR = 2 * HEAD_DIM
GROUP_COLS = GQA_GROUP * HEAD_DIM
HALO = SUBLANES
KEY_CHUNK = 256
BAND_CHUNK = WINDOW
LOG2E = 1.4426950408889634
Q_SCALE = HEAD_DIM ** -0.5 * LOG2E
ONES_ROWS = 16
VT_ROWS = HEAD_DIM + ONES_ROWS
FFN_CHUNK = 256


def _vmem_limit(block_bytes, scratch_bytes=0, temp_bytes=0):
    need = 2 * block_bytes + scratch_bytes + temp_bytes
    return int(min(max(need + need // 4, VMEM_LIMIT_FLOOR), VMEM_LIMIT_CAP))


def _nbytes(shape, dtype):
    n = 1
    for s in shape:
        n *= s
    return n * jnp.dtype(dtype).itemsize


def _row_tile(t, want):
    tile = min(t, want)
    assert t % tile == 0
    return tile


def _ada_kernel(c_ref, w_ref, b_ref, o_ref):
    c = c_ref[...]
    a = (c * jax.nn.sigmoid(c)).astype(BF16)
    o_ref[0] = jnp.dot(a, w_ref[0].astype(BF16), preferred_element_type=F32) + b_ref[0]


def _ada_params(cc, ada_w, ada_b):
    depth, d, n = ada_w.shape
    rows = cc.shape[0]
    tn = n // N_MOD
    blocks = _nbytes((rows, d), F32) + _nbytes((d, tn), F32) + _nbytes((rows, tn), F32)
    return pl.pallas_call(
        _ada_kernel,
        grid=(depth, n // tn),
        in_specs=[
            pl.BlockSpec((rows, d), lambda l, j: (0, 0)),
            pl.BlockSpec((1, d, tn), lambda l, j: (l, 0, j)),
            pl.BlockSpec((1, 1, tn), lambda l, j: (l, 0, j)),
        ],
        out_specs=pl.BlockSpec((1, rows, tn), lambda l, j: (l, 0, j)),
        out_shape=jax.ShapeDtypeStruct((depth, rows, n), F32),
        compiler_params=pltpu.CompilerParams(
            dimension_semantics=("parallel", "parallel"),
            vmem_limit_bytes=_vmem_limit(blocks, temp_bytes=_nbytes((d, tn), BF16))),
        name="ada_params",
    )(cc, ada_w, ada_b.reshape(depth, 1, n))


def _rms_norm(x, g):
    return x * lax.rsqrt(jnp.mean(x * x, axis=-1, keepdims=True) + EPS) * g


def _pre_mod(x, g, shift, scale):
    return _rms_norm(x, g) * (1.0 + scale) + shift


def _inproj_kernel(*refs, n_u, n_q, n_kv, n_v, key_chunk, rope, qk_norm, tm):
    x_ref, sh_ref, sc_ref, g_ref, w_ref = refs[:5]
    pos = 5
    if rope:
        cos_ref, sin_ref = refs[pos:pos + 2]
        pos += 2
    if qk_norm:
        qg_ref, kg_ref, ones_ref = refs[pos:pos + 3]
        pos += 3
    out_refs = refs[pos:]

    h = _pre_mod(x_ref[0], g_ref[...], sh_ref[0], sc_ref[0]).astype(BF16)

    if rope:
        lane = lax.broadcasted_iota(jnp.int32, (tm, PAIR), 1)
        first_half = (lane & (HEAD_DIM - 1)) < HEAD_DIM // 2
        cos = cos_ref[...]
        sin = sin_ref[...]

    def head_epilogue(a, gain_ref, scale):
        if qk_norm:
            a2 = a * a
            hi = a2.astype(BF16)
            lo = (a2 - hi.astype(F32)).astype(BF16)
            ss = (jnp.dot(hi, ones_ref[...], preferred_element_type=F32)
                  + jnp.dot(lo, ones_ref[...], preferred_element_type=F32))
            a = a * lax.rsqrt(ss * (1.0 / HEAD_DIM) + EPS) * gain_ref[...]
        if rope:
            partner = jnp.where(first_half,
                                pltpu.roll(a, PAIR - HEAD_DIM // 2, 1),
                                pltpu.roll(a, HEAD_DIM // 2, 1))
            a = a * cos + partner * sin
        if scale != 1.0:
            a = a * scale
        return a.astype(BF16)

    col = 0
    out_idx = 0
    if n_u:
        out_refs[out_idx][0] = jnp.dot(h, w_ref[:, col:col + n_u], preferred_element_type=F32)
        col += n_u
        out_idx += 1

    acc = jnp.dot(h, w_ref[:, col:col + n_q], preferred_element_type=F32)
    for c in range(n_q // PAIR):
        out_refs[out_idx][0, :, c * PAIR:(c + 1) * PAIR] = head_epilogue(
            acc[:, c * PAIR:(c + 1) * PAIR], qg_ref if qk_norm else None, Q_SCALE)
    col += n_q
    out_idx += 1

    acc = jnp.dot(h, w_ref[:, col:col + n_kv], preferred_element_type=F32)
    for c in range(n_kv // PAIR):
        out_refs[out_idx][0, :, c * PAIR:(c + 1) * PAIR] = head_epilogue(
            acc[:, c * PAIR:(c + 1) * PAIR], kg_ref if qk_norm else None, 1.0)
    col += n_kv
    out_idx += 1

    acc = jnp.dot(h, w_ref[:, col:col + n_v], preferred_element_type=F32)
    ones = jnp.ones((ONES_ROWS, key_chunk), F32)
    for cc in range(tm // key_chunk):
        at = acc[cc * key_chunk:(cc + 1) * key_chunk, :].T
        pieces = []
        for j in range(n_v // HEAD_DIM):
            pieces += [at[j * HEAD_DIM:(j + 1) * HEAD_DIM], ones]
        out_refs[out_idx][0, cc] = jnp.concatenate(pieces, axis=0).astype(BF16)


def _inproj(x, mods, mod_row, pre_g, w, *, n_u, n_q, n_kv, n_v, key_chunk=KEY_CHUNK,
            rope_tabs=None, qk_gains=None):
    b, t, d = x.shape
    n = w.shape[1]
    tm = _row_tile(t, 512)
    rope = rope_tabs is not None
    qk_norm = qk_gains is not None
    vt_rows = n_v // HEAD_DIM * VT_ROWS
    assert tm % key_chunk == 0

    in_specs = [
        pl.BlockSpec((1, tm, d), lambda bi, i: (bi, i, 0)),
        pl.BlockSpec((1, 1, d), lambda bi, i: (mod_row(bi), 0, 0)),
        pl.BlockSpec((1, 1, d), lambda bi, i: (mod_row(bi), 0, 1)),
        pl.BlockSpec((1, d), lambda bi, i: (0, 0)),
        pl.BlockSpec((d, n), lambda bi, i: (0, 0)),
    ]
    args = [x, mods, mods, pre_g, w]
    if rope:
        in_specs += [pl.BlockSpec((tm, PAIR), lambda bi, i: (i, 0))] * 2
        args += list(rope_tabs)
    if qk_norm:
        in_specs += [pl.BlockSpec((1, PAIR), lambda bi, i: (0, 0))] * 2
        in_specs += [pl.BlockSpec((PAIR, PAIR), lambda bi, i: (0, 0))]
        args += list(qk_gains)

    out_shape, out_specs = [], []
    for width, dtype in ((n_u, F32), (n_q, BF16), (n_kv, BF16)):
        if width:
            out_shape.append(jax.ShapeDtypeStruct((b, t, width), dtype))
            out_specs.append(pl.BlockSpec((1, tm, width), lambda bi, i: (bi, i, 0)))
    out_shape.append(jax.ShapeDtypeStruct((b, t // key_chunk, vt_rows, key_chunk), BF16))
    out_specs.append(pl.BlockSpec((1, tm // key_chunk, vt_rows, key_chunk),
                                  lambda bi, i: (bi, i, 0, 0)))

    blocks = (_nbytes((tm, d), F32) + _nbytes((d, n), BF16) + _nbytes((tm, n_u), F32)
              + _nbytes((tm, n_q + n_kv + 2 * n_v), BF16) + 2 * _nbytes((tm, PAIR), F32))
    temps = _nbytes((tm, d), F32) * 2 + _nbytes((tm, max(n_q, n_kv, n_u)), F32) * 2
    return pl.pallas_call(
        functools.partial(_inproj_kernel, n_u=n_u, n_q=n_q, n_kv=n_kv, n_v=n_v,
                          key_chunk=key_chunk, rope=rope, qk_norm=qk_norm, tm=tm),
        grid=(b, t // tm),
        in_specs=in_specs,
        out_specs=out_specs,
        out_shape=out_shape,
        compiler_params=pltpu.CompilerParams(
            dimension_semantics=("parallel", "parallel"),
            vmem_limit_bytes=_vmem_limit(blocks, temp_bytes=temps)),
        name="mixer_inproj",
    )(*args)


def _stack_heads_transposed(q_ref, qt_ref, tq):
    sub = lax.broadcasted_iota(jnp.int32, (PAIR, tq), 0)
    top = sub < HEAD_DIM
    for p in range(GROUP_COLS // PAIR):
        pair_t = q_ref[0, :, p * PAIR:(p + 1) * PAIR].astype(F32).T
        qt_ref[:, (2 * p) * tq:(2 * p + 1) * tq] = jnp.where(top, pair_t, 0.0).astype(BF16)
        qt_ref[:, (2 * p + 1) * tq:(2 * p + 2) * tq] = jnp.where(top, 0.0, pair_t).astype(BF16)


def _store_heads(o_t, o_ref, tq):
    for p in range(GROUP_COLS // PAIR):
        pair_t = jnp.concatenate([o_t[:, (2 * p) * tq:(2 * p + 1) * tq],
                                  o_t[:, (2 * p + 1) * tq:(2 * p + 2) * tq]], axis=0)
        o_ref[0, :, p * PAIR:(p + 1) * PAIR] = pair_t.T.astype(o_ref.dtype)


def _flash_kernel(*refs, tq, n_chunks, has_sink):
    if has_sink:
        q_ref, k_ref, vt_ref, sink_ref, o_ref, qt_ref, s_ref, m_ref, acc_ref = refs
    else:
        q_ref, k_ref, vt_ref, o_ref, qt_ref, s_ref, m_ref, acc_ref = refs

    _stack_heads_transposed(q_ref, qt_ref, tq)
    if has_sink:
        m_ref[...] = sink_ref[0]
        den_rows = lax.broadcasted_iota(jnp.int32, acc_ref.shape, 0) >= HEAD_DIM
        acc_ref[...] = jnp.where(den_rows, 1.0, 0.0)
    else:
        m_ref[...] = jnp.full_like(m_ref, NEG)
        acc_ref[...] = jnp.zeros_like(acc_ref)

    def scores(c):
        start = pl.multiple_of(c * KEY_CHUNK, KEY_CHUNK)
        return jnp.dot(k_ref[0, pl.ds(start, KEY_CHUNK), :], qt_ref[...],
                       preferred_element_type=F32)

    def consume(slot, c):
        s = s_ref[slot]
        m_prev = m_ref[...]
        m_new = jnp.maximum(m_prev, jnp.max(s, axis=0, keepdims=True))
        alpha = jnp.exp2(m_prev - m_new)
        p = jnp.exp2(s - m_new)
        acc_ref[...] = alpha * acc_ref[...] + jnp.dot(vt_ref[0, c], p.astype(BF16),
                                                     preferred_element_type=F32)
        m_ref[...] = m_new

    s_ref[0] = scores(0)

    def body(pair, carry):
        c = 2 * pair
        s_ref[1] = scores(c + 1)
        consume(0, c)
        s_ref[0] = scores(c + 2)
        consume(1, c + 1)
        return carry
    n_pairs = n_chunks // 2
    lax.fori_loop(0, n_pairs, body, 0, unroll=4 if n_pairs % 4 == 0 else 1)
    consume(0, n_chunks - 1)

    _store_heads(acc_ref[0:HEAD_DIM, :] / acc_ref[HEAD_DIM:HEAD_DIM + 1, :], o_ref, tq)


def _flash_attention(q, k2, vt, sink_row=None, *, tq_want=256):
    b, t, hq = q.shape
    n_chunks = vt.shape[1]
    n_groups = hq // GROUP_COLS
    tq = _row_tile(t, tq_want)
    cols = GQA_GROUP * tq
    has_sink = sink_row is not None
    assert n_chunks % 2 == 1 and k2.shape[1] == n_chunks * KEY_CHUNK

    in_specs = [
        pl.BlockSpec((1, tq, GROUP_COLS), lambda bi, j, i: (bi, i, j)),
        pl.BlockSpec((1, n_chunks * KEY_CHUNK, PAIR), lambda bi, j, i: (bi, 0, j)),
        pl.BlockSpec((1, n_chunks, VT_ROWS, KEY_CHUNK), lambda bi, j, i: (bi, 0, j, 0)),
    ]
    args = [q, k2, vt]
    if has_sink:
        in_specs.append(pl.BlockSpec((1, 1, cols), lambda bi, j, i: (j, 0, 0)))
        args.append(sink_row)
    blocks = (2 * _nbytes((tq, GROUP_COLS), BF16)
              + _nbytes((n_chunks * KEY_CHUNK, PAIR + VT_ROWS), BF16))
    scratch = (_nbytes((PAIR, cols), BF16) + _nbytes((2, KEY_CHUNK, cols), F32)
               + _nbytes((VT_ROWS + SUBLANES, cols), F32))
    temps = 2 * _nbytes((KEY_CHUNK, cols), F32)
    return pl.pallas_call(
        functools.partial(_flash_kernel, tq=tq, n_chunks=n_chunks, has_sink=has_sink),
        grid=(b, n_groups, t // tq),
        in_specs=in_specs,
        out_specs=pl.BlockSpec((1, tq, GROUP_COLS), lambda bi, j, i: (bi, i, j)),
        out_shape=jax.ShapeDtypeStruct((b, t, hq), BF16),
        scratch_shapes=[
            pltpu.VMEM((PAIR, cols), BF16),
            pltpu.VMEM((2, KEY_CHUNK, cols), F32),
            pltpu.VMEM((1, cols), F32),
            pltpu.VMEM((VT_ROWS, cols), F32),
        ],
        compiler_params=pltpu.CompilerParams(
            dimension_semantics=("parallel", "parallel", "parallel"),
            vmem_limit_bytes=_vmem_limit(blocks, scratch, temps)),
        name="flash_attention",
    )(*args)


def _window_kernel(q_ref, k_ref, vt_ref, kc_ref, vtc_ref, bias_ref, sink_ref, o_ref, qt_ref,
                   *, tq, t):
    span = tq + 2 * WINDOW
    q0 = pl.program_id(2) * tq
    start = pl.multiple_of(jnp.clip(q0 - WINDOW, 0, t - span), WINDOW)
    bias = bias_ref[(q0 - start) // WINDOW]
    _stack_heads_transposed(q_ref, qt_ref, tq)
    qt = qt_ref[...]

    s_band = jnp.dot(k_ref[0, pl.ds(start, span), :], qt, preferred_element_type=F32)
    s_band = s_band + jnp.concatenate([bias] * GQA_GROUP, axis=1)
    s_ctx = jnp.dot(kc_ref[0], qt, preferred_element_type=F32)

    sink = sink_ref[0]
    m = jnp.maximum(jnp.maximum(jnp.max(s_band, axis=0, keepdims=True),
                                jnp.max(s_ctx, axis=0, keepdims=True)), sink)
    p_band = jnp.exp2(s_band - m).astype(BF16)
    p_ctx = jnp.exp2(s_ctx - m).astype(BF16)
    first = start // BAND_CHUNK
    vt_band = jnp.concatenate([vt_ref[0, first + c] for c in range(span // BAND_CHUNK)], axis=1)
    acc = (jnp.dot(vt_band, p_band, preferred_element_type=F32)
           + jnp.dot(vtc_ref[0, 0], p_ctx, preferred_element_type=F32))
    den = acc[HEAD_DIM:HEAD_DIM + 1, :] + jnp.exp2(sink - m)
    _store_heads(acc[0:HEAD_DIM, :] / den, o_ref, tq)


def _band_bias(tq):
    span = tq + 2 * WINDOW
    key = jnp.arange(span)[None, :, None]
    qry = jnp.arange(tq)[None, None, :]
    back = (jnp.arange(3) * WINDOW)[:, None, None]
    delta = qry + back - key
    return jnp.where(jnp.abs(delta) <= WINDOW, 0.0, NEG).astype(F32)


def _window_attention(q, k2, vt, kc2, vtc, sink_row, *, tq):
    b, t, hq = q.shape
    n_ctx = kc2.shape[1]
    n_groups = hq // GROUP_COLS
    cols = GQA_GROUP * tq
    span = tq + 2 * WINDOW
    n_band = t // BAND_CHUNK
    assert t % tq == 0 and t >= span and tq % WINDOW == 0 and vtc.shape[1] == 1
    blocks = (2 * _nbytes((tq, GROUP_COLS), BF16) + _nbytes((t, PAIR + VT_ROWS), BF16)
              + _nbytes((n_ctx, PAIR + VT_ROWS), BF16) + _nbytes((3, span, tq), F32))
    temps = 3 * _nbytes((span + n_ctx, cols), F32)
    return pl.pallas_call(
        functools.partial(_window_kernel, tq=tq, t=t),
        grid=(b, n_groups, t // tq),
        in_specs=[
            pl.BlockSpec((1, tq, GROUP_COLS), lambda bi, j, i: (bi, i, j)),
            pl.BlockSpec((1, t, PAIR), lambda bi, j, i: (bi, 0, j)),
            pl.BlockSpec((1, n_band, VT_ROWS, BAND_CHUNK), lambda bi, j, i: (bi, 0, j, 0)),
            pl.BlockSpec((1, n_ctx, PAIR), lambda bi, j, i: (bi, 0, j)),
            pl.BlockSpec((1, 1, VT_ROWS, n_ctx), lambda bi, j, i: (bi, 0, j, 0)),
            pl.BlockSpec((3, span, tq), lambda bi, j, i: (0, 0, 0)),
            pl.BlockSpec((1, 1, cols), lambda bi, j, i: (j, 0, 0)),
        ],
        out_specs=pl.BlockSpec((1, tq, GROUP_COLS), lambda bi, j, i: (bi, i, j)),
        out_shape=jax.ShapeDtypeStruct((b, t, hq), BF16),
        scratch_shapes=[pltpu.VMEM((PAIR, cols), BF16)],
        compiler_params=pltpu.CompilerParams(
            dimension_semantics=("parallel", "parallel", "parallel"),
            vmem_limit_bytes=_vmem_limit(blocks, _nbytes((PAIR, cols), BF16), temps)),
        name="window_attention",
    )(q, k2, vt, kc2, vtc, _band_bias(tq), sink_row)


def _pool_kernel(u_ref, w_ref, scale_ref, o_ref, pad_ref, *, t):
    reach = max(POOL_WINDOWS) // 2
    assert reach <= HALO
    pos = lax.broadcasted_iota(jnp.int32, (t, POOL_GC), 0)
    zeros = jnp.zeros((HALO, POOL_GC), F32)
    pad_ref[0:HALO, :] = zeros
    pad_ref[HALO + t:2 * HALO + t, :] = zeros
    for g, win in enumerate(POOL_WINDOWS):
        half = win // 2
        u = u_ref[0, :, g * POOL_GC:(g + 1) * POOL_GC]
        pad_ref[HALO:HALO + t, :] = u
        total = pad_ref[pl.ds(HALO - half, t), :]
        for off in range(1 - half, half):
            total = total + pad_ref[pl.ds(HALO + off, t), :]
        cnt = (jnp.minimum(pos + half, t) - jnp.maximum(pos - half, 0)).astype(F32)
        d = (total / cnt - u).astype(BF16)
        y = jnp.dot(d, w_ref[g], preferred_element_type=F32)
        o_ref[0, :, g * POOL_GC:(g + 1) * POOL_GC] = (
            y * scale_ref[:, g * POOL_GC:(g + 1) * POOL_GC]).astype(BF16)


def _pool_mix(u, pool_w, pool_scale):
    b, t, ch = u.shape
    n_g = len(POOL_WINDOWS)
    assert ch == n_g * POOL_GC
    blocks = _nbytes((t, ch), F32) + _nbytes((t, ch), BF16) + _nbytes((n_g, POOL_GC, POOL_GC), BF16)
    scratch = _nbytes((t + 2 * HALO, POOL_GC), F32)
    return pl.pallas_call(
        functools.partial(_pool_kernel, t=t),
        grid=(b,),
        in_specs=[
            pl.BlockSpec((1, t, ch), lambda bi: (bi, 0, 0)),
            pl.BlockSpec((n_g, POOL_GC, POOL_GC), lambda bi: (0, 0, 0)),
            pl.BlockSpec((1, ch), lambda bi: (0, 0)),
        ],
        out_specs=pl.BlockSpec((1, t, ch), lambda bi: (bi, 0, 0)),
        out_shape=jax.ShapeDtypeStruct((b, t, ch), BF16),
        scratch_shapes=[pltpu.VMEM((t + 2 * HALO, POOL_GC), F32)],
        compiler_params=pltpu.CompilerParams(
            dimension_semantics=("parallel",),
            vmem_limit_bytes=_vmem_limit(blocks, scratch, 6 * _nbytes((t, POOL_GC), F32))),
        name="pool_mix",
    )(u, pool_w, pool_scale)


def _outproj_kernel(*refs, n_in):
    a_refs = refs[:n_in]
    w_refs = refs[n_in:2 * n_in]
    x_ref, gate_ref, g_ref, o_ref = refs[2 * n_in:]
    y = jnp.dot(a_refs[0][0], w_refs[0][...], preferred_element_type=F32)
    for a_ref, w_ref in zip(a_refs[1:], w_refs[1:]):
        y = y + jnp.dot(a_ref[0], w_ref[...], preferred_element_type=F32)
    o_ref[0] = x_ref[0] + gate_ref[0] * _rms_norm(y, g_ref[...])


def _outproj(acts, weights, x, mods, mod_row, post_g):
    b, t, d = x.shape
    tm = _row_tile(t, 512)
    n_in = len(acts)
    in_specs = [pl.BlockSpec((1, tm, a.shape[2]), lambda bi, i: (bi, i, 0)) for a in acts]
    in_specs += [pl.BlockSpec(w.shape, lambda bi, i: (0, 0)) for w in weights]
    in_specs += [
        pl.BlockSpec((1, tm, d), lambda bi, i: (bi, i, 0)),
        pl.BlockSpec((1, 1, d), lambda bi, i: (mod_row(bi), 0, 2)),
        pl.BlockSpec((1, d), lambda bi, i: (0, 0)),
    ]
    blocks = (sum(_nbytes((tm, a.shape[2]), BF16) for a in acts)
              + sum(_nbytes(w.shape, BF16) for w in weights) + 2 * _nbytes((tm, d), F32))
    return pl.pallas_call(
        functools.partial(_outproj_kernel, n_in=n_in),
        grid=(b, t // tm),
        in_specs=in_specs,
        out_specs=pl.BlockSpec((1, tm, d), lambda bi, i: (bi, i, 0)),
        out_shape=jax.ShapeDtypeStruct((b, t, d), F32),
        compiler_params=pltpu.CompilerParams(
            dimension_semantics=("parallel", "parallel"),
            vmem_limit_bytes=_vmem_limit(blocks, temp_bytes=3 * _nbytes((tm, d), F32))),
        name="mixer_outproj",
    )(*acts, *weights, x, mods, post_g)


def _ffn_kernel(x_ref, xp_ref, xn_ref, sh_ref, sc_ref, gate_ref, pre_g_ref, post_g_ref,
                wu_ref, wut_ref, cw_ref, cwt_ref, cb_ref, cbt_ref, wd_ref, wdt_ref, o_ref,
                h_ref, acc_ref, u_ref, ut_ref, *, tm, n_main):
    i = pl.program_id(1)
    g, sh, sc = pre_g_ref[...], sh_ref[0], sc_ref[0]
    h_prev = _pre_mod(xp_ref[0], g, sh, sc)
    h_next = _pre_mod(xn_ref[0], g, sh, sc)
    h_prev = jnp.where(i == 0, 0.0, h_prev)
    h_next = jnp.where(i == pl.num_programs(1) - 1, 0.0, h_next)
    h_ref[0:HALO, :] = h_prev.astype(BF16)
    h_ref[HALO:HALO + tm, :] = _pre_mod(x_ref[0], g, sh, sc).astype(BF16)
    h_ref[HALO + tm:2 * HALO + tm, :] = h_next.astype(BF16)

    def up(c):
        return jnp.dot(h_ref[...], wu_ref[c], preferred_element_type=F32)

    def conv_down(u, cw, cb, wd):
        w = wd.shape[0]
        conv = (u[pl.ds(HALO - 1, tm), :] * cw[0:1, :]
                + u[pl.ds(HALO, tm), :] * cw[1:2, :]
                + u[pl.ds(HALO + 1, tm), :] * cw[2:3, :]
                + cb)
        gated = jax.nn.gelu(conv[:, :w]) * conv[:, w:]
        return jnp.dot(gated.astype(BF16), wd, preferred_element_type=F32)

    def consume(slot, c):
        return conv_down(u_ref.at[slot], cw_ref[c], cb_ref[c], wd_ref[c])

    u_ref[0] = up(0)
    acc_ref[...] = jnp.zeros_like(acc_ref)

    def body(pair, carry):
        c = 2 * pair
        u_ref[1] = up(c + 1)
        acc_ref[...] += consume(0, c)
        u_ref[0] = up(c + 2)
        acc_ref[...] += consume(1, c + 1)
        return carry
    lax.fori_loop(0, (n_main - 1) // 2, body, 0, unroll=True)
    last = n_main - 1
    if n_main % 2 == 0:
        u_ref[1] = up(last)
        acc_ref[...] += consume(0, last - 1)
    ut_ref[...] = jnp.dot(h_ref[...], wut_ref[...], preferred_element_type=F32)
    acc_ref[...] += consume(last % 2, last)
    y = acc_ref[...] + conv_down(ut_ref, cwt_ref[...], cbt_ref[...], wdt_ref[...])
    o_ref[0] = x_ref[0] + gate_ref[0] * _rms_norm(y, post_g_ref[...])


def _ffn_weights(w_up, conv_w, conv_b, w_down):
    d_ff = w_down.shape[0]
    n_main = d_ff // FFN_CHUNK
    split = n_main * FFN_CHUNK

    def pair_cols(a):
        r = a.shape[0]
        gelu, gate = a[:, :d_ff], a[:, d_ff:]
        main = jnp.concatenate([gelu[:, :split].reshape(r, n_main, FFN_CHUNK),
                                gate[:, :split].reshape(r, n_main, FFN_CHUNK)], axis=2)
        tail = jnp.concatenate([gelu[:, split:], gate[:, split:]], axis=1)
        return jnp.swapaxes(main, 0, 1), tail

    wu, wut = pair_cols(w_up.astype(BF16))
    cw, cwt = pair_cols(conv_w)
    cb, cbt = pair_cols(conv_b[None])
    wd = w_down.astype(BF16)
    return (wu, wut, cw, cwt, cb, cbt, wd[:split].reshape(n_main, FFN_CHUNK, -1), wd[split:])


def _conv_ffn(x, mods, mod_row, pre_g, post_g, weights, *, tm_want=512):
    b, t, d = x.shape
    tm = _row_tile(t, tm_want)
    n_main = weights[0].shape[0]
    tail = weights[-1].shape[0]
    assert tm % HALO == 0 and n_main >= 2 and tail > 0
    halo_blocks = tm // HALO
    last_halo = t // HALO - 1
    rows = tm + 2 * HALO

    def resident(a):
        zeros = (0,) * a.ndim
        return pl.BlockSpec(a.shape, lambda bi, i: zeros, pipeline_mode=pl.Buffered(1))

    in_specs = [
        pl.BlockSpec((1, tm, d), lambda bi, i: (bi, i, 0)),
        pl.BlockSpec((1, HALO, d), lambda bi, i: (bi, jnp.maximum(i * halo_blocks - 1, 0), 0)),
        pl.BlockSpec((1, HALO, d),
                     lambda bi, i: (bi, jnp.minimum((i + 1) * halo_blocks, last_halo), 0)),
        pl.BlockSpec((1, 1, d), lambda bi, i: (mod_row(bi), 0, 3)),
        pl.BlockSpec((1, 1, d), lambda bi, i: (mod_row(bi), 0, 4)),
        pl.BlockSpec((1, 1, d), lambda bi, i: (mod_row(bi), 0, 5)),
        pl.BlockSpec((1, d), lambda bi, i: (0, 0)),
        pl.BlockSpec((1, d), lambda bi, i: (0, 0)),
    ] + [resident(a) for a in weights]
    weight_bytes = sum(_nbytes(a.shape, a.dtype) for a in weights)
    blocks = 2 * _nbytes((tm, d), F32) + 2 * _nbytes((HALO, d), F32)
    scratch = (_nbytes((rows, d), BF16) + _nbytes((tm, d), F32)
               + _nbytes((2, rows, 2 * FFN_CHUNK), F32) + _nbytes((rows, 2 * tail), F32)
               + weight_bytes)
    temps = 4 * _nbytes((tm, 2 * FFN_CHUNK), F32) + _nbytes((tm, d), F32)
    return pl.pallas_call(
        functools.partial(_ffn_kernel, tm=tm, n_main=n_main),
        grid=(b, t // tm),
        in_specs=in_specs,
        out_specs=pl.BlockSpec((1, tm, d), lambda bi, i: (bi, i, 0)),
        out_shape=jax.ShapeDtypeStruct((b, t, d), F32),
        scratch_shapes=[
            pltpu.VMEM((rows, d), BF16),
            pltpu.VMEM((tm, d), F32),
            pltpu.VMEM((2, rows, 2 * FFN_CHUNK), F32),
            pltpu.VMEM((rows, 2 * tail), F32),
        ],
        compiler_params=pltpu.CompilerParams(
            dimension_semantics=("parallel", "parallel"),
            vmem_limit_bytes=_vmem_limit(blocks, scratch, temps)),
        name="conv_ffn",
    )(x, x, x, mods, mods, mods, pre_g, post_g, *weights)


def _rope_tables(t):
    rows = t // GRID_W
    row = jnp.repeat(jnp.arange(rows), GRID_W).astype(F32)
    col = jnp.tile(jnp.arange(GRID_W), rows).astype(F32)
    n_freq = HEAD_DIM // 4
    freqs = ROPE_BASE ** (-jnp.arange(n_freq, dtype=F32) / n_freq)
    ang = jnp.concatenate([row[:, None] * freqs, col[:, None] * freqs], axis=-1)
    cos, sin = jnp.cos(ang), jnp.sin(ang)
    return (jnp.concatenate([cos, cos, cos, cos], axis=-1),
            jnp.concatenate([-sin, sin, -sin, sin], axis=-1))


def _dup_heads(w, n_heads):
    d = w.shape[0]
    w = w.reshape(d, n_heads, 1, HEAD_DIM)
    return jnp.broadcast_to(w, (d, n_heads, 2, HEAD_DIM)).reshape(d, n_heads * PAIR)


def _sink_row(sink, tq):
    return jnp.repeat(sink.astype(F32) * LOG2E, tq, axis=1)[:, None, :]


def kernel(x, c, ctx, c_ctx, ada_w, ada_b, mix_pre_g, mix_post_g, ffn_pre_g, ffn_post_g,
           ab_w_in, ab_w_out, pool_w, pool_scale, sink_logit,
           c_w_qkv, c_w_out, c_q_g, c_k_g,
           ffn_w_up, ffn_conv_w, ffn_conv_b, ffn_w_down):
    batch, t, d = x.shape
    n_ctx = ctx.shape[1]
    depth = ada_w.shape[0]
    pool_ch = pool_scale.shape[1]
    b_kv = sink_logit.shape[1]
    b_q = b_kv * GQA_GROUP * HEAD_DIM
    c_q = c_w_out.shape[1]
    c_kv = c_q // HEAD_DIM // GQA_GROUP
    d_ff = ffn_w_down.shape[1]

    ctx_row = batch
    n_rows = -(-(batch + 1) // SUBLANES) * SUBLANES
    cc = jnp.zeros((n_rows, d), F32).at[:batch].set(c).at[ctx_row].set(c_ctx)
    mods_all = _ada_params(cc, ada_w, ada_b).reshape(depth, n_rows, 1, N_MOD * d)

    rope_tabs = _rope_tables(t)
    ones_bd = jnp.kron(jnp.eye(2, dtype=F32), jnp.ones((HEAD_DIM, HEAD_DIM), F32)).astype(BF16)
    latent_row = lambda bi: bi
    context_row = lambda bi: ctx_row
    win_tq = _row_tile(t, 256)
    ctx_tq = _row_tile(n_ctx, 256)

    def qkv_weights(w, n_lead, n_kv_heads):
        k_end = n_lead + n_kv_heads * HEAD_DIM
        return jnp.concatenate([w[:, :n_lead], _dup_heads(w[:, n_lead:k_end], n_kv_heads),
                                w[:, k_end:]], axis=1).astype(BF16)

    xc = ctx
    for l in range(depth):
        need_ctx = l < depth - 1
        i = l // 2
        mods = mods_all[l]
        pre_g, post_g = mix_pre_g[l][None], mix_post_g[l][None]
        if l % 2 == 0:
            w_in = qkv_weights(ab_w_in[i], pool_ch + b_q, b_kv)
            widths = dict(n_u=pool_ch, n_q=b_q, n_kv=b_kv * PAIR, n_v=b_kv * HEAD_DIM)
            u, q, k2, vt = _inproj(x, mods, latent_row, pre_g, w_in, rope_tabs=rope_tabs,
                                   key_chunk=BAND_CHUNK, **widths)
            uc, qc, kc2, vtc = _inproj(xc, mods, context_row, pre_g, w_in, **widths)
            w_out = ab_w_out[i].astype(BF16)
            w_outs = [w_out[:pool_ch], w_out[pool_ch:]]
            pw, ps = pool_w[i].astype(BF16), pool_scale[i][None]
            a = _window_attention(q, k2, vt, kc2, vtc, _sink_row(sink_logit[i], win_tq),
                                  tq=win_tq)
            x = _outproj([_pool_mix(u, pw, ps), a], w_outs, x, mods, latent_row, post_g)
            if need_ctx:
                ac = _flash_attention(qc, kc2, vtc, _sink_row(sink_logit[i], ctx_tq),
                                      tq_want=ctx_tq)
                xc = _outproj([_pool_mix(uc, pw, ps), ac], w_outs, xc, mods, context_row, post_g)
        else:
            w_in = qkv_weights(c_w_qkv[i], c_q, c_kv)
            gains = (jnp.tile(c_q_g[i], 2)[None], jnp.tile(c_k_g[i], 2)[None], ones_bd)
            widths = dict(n_u=0, n_q=c_q, n_kv=c_kv * PAIR, n_v=c_kv * HEAD_DIM)
            q, k2, vt = _inproj(x, mods, latent_row, pre_g, w_in, rope_tabs=rope_tabs,
                                qk_gains=gains, **widths)
            qc, kc2, vtc = _inproj(xc, mods, context_row, pre_g, w_in, qk_gains=gains, **widths)
            w_outs = [c_w_out[i].astype(BF16)]
            a = _flash_attention(q, jnp.concatenate([k2, kc2], axis=1),
                                 jnp.concatenate([vt, vtc], axis=1))
            x = _outproj([a], w_outs, x, mods, latent_row, post_g)
            if need_ctx:
                ac = _flash_attention(qc, kc2, vtc)
                xc = _outproj([ac], w_outs, xc, mods, context_row, post_g)

        ffn_args = (ffn_pre_g[l][None], ffn_post_g[l][None],
                    _ffn_weights(ffn_w_up[l], ffn_conv_w[l], ffn_conv_b[l], ffn_w_down[l]))
        x = _conv_ffn(x, mods, latent_row, *ffn_args)
        if need_ctx:
            xc = _conv_ffn(xc, mods, context_row, *ffn_args)
    return x
```

```python
import functools

import jax
import jax.numpy as jnp
from jax import lax
from jax.experimental import pallas as pl
from jax.experimental.pallas import tpu as pltpu

F32 = jnp.float32
BF16 = jnp.bfloat16

HEAD_DIM = 64
GQA_GROUP = 4
POOL_WINDOWS = (2, 4, 8, 16)
POOL_GC = 128
WINDOW = 128
GRID_W = 64
ROPE_BASE = 10000.0
EPS = 1e-6
NEG = -1e30
N_MOD = 6

LANES = 128
SUBLANES = 8
VMEM_LIMIT_CAP = 56 * 1024 * 1024
VMEM_LIMIT_FLOOR = 32 * 1024 * 1024

PAIR = 2 * HEAD_DIM
GROUP_COLS = GQA_GROUP * HEAD_DIM
HALO = SUBLANES
KEY_CHUNK = 256
BAND_CHUNK = WINDOW
LOG2E = 1.4426950408889634
Q_SCALE = HEAD_DIM ** -0.5 * LOG2E
ONES_ROWS = 16
VT_ROWS = HEAD_DIM + ONES_ROWS
FFN_CHUNK = 256


def _vmem_limit(block_bytes, scratch_bytes=0, temp_bytes=0):
    need = 2 * block_bytes + scratch_bytes + temp_bytes
    return int(min(max(need + need // 4, VMEM_LIMIT_FLOOR), VMEM_LIMIT_CAP))


def _nbytes(shape, dtype):
    n = 1
    for s in shape:
        n *= s
    return n * jnp.dtype(dtype).itemsize


def _row_tile(t, want):
    tile = min(t, want)
    assert t % tile == 0
    return tile


def _ada_kernel(c_ref, w_ref, b_ref, o_ref):
    c = c_ref[...]
    a = (c * jax.nn.sigmoid(c)).astype(BF16)
    o_ref[0] = jnp.dot(a, w_ref[0].astype(BF16), preferred_element_type=F32) + b_ref[0]


def _ada_params(cc, ada_w, ada_b):
    depth, d, n = ada_w.shape
    rows = cc.shape[0]
    tn = n // N_MOD
    blocks = _nbytes((rows, d), F32) + _nbytes((d, tn), F32) + _nbytes((rows, tn), F32)
    return pl.pallas_call(
        _ada_kernel,
        grid=(depth, n // tn),
        in_specs=[
            pl.BlockSpec((rows, d), lambda l, j: (0, 0)),
            pl.BlockSpec((1, d, tn), lambda l, j: (l, 0, j)),
            pl.BlockSpec((1, 1, tn), lambda l, j: (l, 0, j)),
        ],
        out_specs=pl.BlockSpec((1, rows, tn), lambda l, j: (l, 0, j)),
        out_shape=jax.ShapeDtypeStruct((depth, rows, n), F32),
        compiler_params=pltpu.CompilerParams(
            dimension_semantics=("parallel", "parallel"),
            vmem_limit_bytes=_vmem_limit(blocks, temp_bytes=_nbytes((d, tn), BF16))),
        name="ada_params",
    )(cc, ada_w, ada_b.reshape(depth, 1, n))


def _rms_norm(x, g):
    return x * lax.rsqrt(jnp.mean(x * x, axis=-1, keepdims=True) + EPS) * g


def _pre_mod(x, g, shift, scale):
    return _rms_norm(x, g) * (1.0 + scale) + shift


def _inproj_kernel(*refs, n_u, n_q, n_kv, n_v, key_chunk, rope, qk_norm, tm):
    x_ref, sh_ref, sc_ref, g_ref, w_ref = refs[:5]
    pos = 5
    if rope:
        cos_ref, sin_ref = refs[pos:pos + 2]
        pos += 2
    if qk_norm:
        qg_ref, kg_ref, ones_ref = refs[pos:pos + 3]
        pos += 3
    out_refs = refs[pos:]

    h = _pre_mod(x_ref[0], g_ref[...], sh_ref[0], sc_ref[0]).astype(BF16)

    if rope:
        lane = lax.broadcasted_iota(jnp.int32, (tm, PAIR), 1)
        first_half = (lane & (HEAD_DIM - 1)) < HEAD_DIM // 2
        cos = cos_ref[...]
        sin = sin_ref[...]

    def head_epilogue(a, gain_ref, scale):
        if qk_norm:
            a2 = a * a
            hi = a2.astype(BF16)
            lo = (a2 - hi.astype(F32)).astype(BF16)
            ss = (jnp.dot(hi, ones_ref[...], preferred_element_type=F32)
                  + jnp.dot(lo, ones_ref[...], preferred_element_type=F32))
            a = a * lax.rsqrt(ss * (1.0 / HEAD_DIM) + EPS) * gain_ref[...]
        if rope:
            partner = jnp.where(first_half,
                                pltpu.roll(a, PAIR - HEAD_DIM // 2, 1),
                                pltpu.roll(a, HEAD_DIM // 2, 1))
            a = a * cos + partner * sin
        if scale != 1.0:
            a = a * scale
        return a.astype(BF16)

    col = 0
    out_idx = 0
    if n_u:
        out_refs[out_idx][0] = jnp.dot(h, w_ref[:, col:col + n_u], preferred_element_type=F32)
        col += n_u
        out_idx += 1

    acc = jnp.dot(h, w_ref[:, col:col + n_q], preferred_element_type=F32)
    for c in range(n_q // PAIR):
        out_refs[out_idx][0, :, c * PAIR:(c + 1) * PAIR] = head_epilogue(
            acc[:, c * PAIR:(c + 1) * PAIR], qg_ref if qk_norm else None, Q_SCALE)
    col += n_q
    out_idx += 1

    acc = jnp.dot(h, w_ref[:, col:col + n_kv], preferred_element_type=F32)
    for c in range(n_kv // PAIR):
        out_refs[out_idx][0, :, c * PAIR:(c + 1) * PAIR] = head_epilogue(
            acc[:, c * PAIR:(c + 1) * PAIR], kg_ref if qk_norm else None, 1.0)
    col += n_kv
    out_idx += 1

    acc = jnp.dot(h, w_ref[:, col:col + n_v], preferred_element_type=F32)
    ones = jnp.ones((ONES_ROWS, key_chunk), F32)
    for cc in range(tm // key_chunk):
        at = acc[cc * key_chunk:(cc + 1) * key_chunk, :].T
        pieces = []
        for j in range(n_v // HEAD_DIM):
            pieces += [at[j * HEAD_DIM:(j + 1) * HEAD_DIM], ones]
        out_refs[out_idx][0, cc] = jnp.concatenate(pieces, axis=0).astype(BF16)


def _inproj(x, mods, mod_row, pre_g, w, *, n_u, n_q, n_kv, n_v, key_chunk=KEY_CHUNK,
            rope_tabs=None, qk_gains=None):
    b, t, d = x.shape
    n = w.shape[1]
    tm = _row_tile(t, 512)
    rope = rope_tabs is not None
    qk_norm = qk_gains is not None
    vt_rows = n_v // HEAD_DIM * VT_ROWS
    assert tm % key_chunk == 0

    in_specs = [
        pl.BlockSpec((1, tm, d), lambda bi, i: (bi, i, 0)),
        pl.BlockSpec((1, 1, d), lambda bi, i: (mod_row(bi), 0, 0)),
        pl.BlockSpec((1, 1, d), lambda bi, i: (mod_row(bi), 0, 1)),
        pl.BlockSpec((1, d), lambda bi, i: (0, 0)),
        pl.BlockSpec((d, n), lambda bi, i: (0, 0)),
    ]
    args = [x, mods, mods, pre_g, w]
    if rope:
        in_specs += [pl.BlockSpec((tm, PAIR), lambda bi, i: (i, 0))] * 2
        args += list(rope_tabs)
    if qk_norm:
        in_specs += [pl.BlockSpec((1, PAIR), lambda bi, i: (0, 0))] * 2
        in_specs += [pl.BlockSpec((PAIR, PAIR), lambda bi, i: (0, 0))]
        args += list(qk_gains)

    out_shape, out_specs = [], []
    for width, dtype in ((n_u, F32), (n_q, BF16), (n_kv, BF16)):
        if width:
            out_shape.append(jax.ShapeDtypeStruct((b, t, width), dtype))
            out_specs.append(pl.BlockSpec((1, tm, width), lambda bi, i: (bi, i, 0)))
    out_shape.append(jax.ShapeDtypeStruct((b, t // key_chunk, vt_rows, key_chunk), BF16))
    out_specs.append(pl.BlockSpec((1, tm // key_chunk, vt_rows, key_chunk),
                                  lambda bi, i: (bi, i, 0, 0)))

    blocks = (_nbytes((tm, d), F32) + _nbytes((d, n), BF16) + _nbytes((tm, n_u), F32)
              + _nbytes((tm, n_q + n_kv + 2 * n_v), BF16) + 2 * _nbytes((tm, PAIR), F32))
    temps = _nbytes((tm, d), F32) * 2 + _nbytes((tm, max(n_q, n_kv, n_u)), F32) * 2
    return pl.pallas_call(
        functools.partial(_inproj_kernel, n_u=n_u, n_q=n_q, n_kv=n_kv, n_v=n_v,
                          key_chunk=key_chunk, rope=rope, qk_norm=qk_norm, tm=tm),
        grid=(b, t // tm),
        in_specs=in_specs,
        out_specs=out_specs,
        out_shape=out_shape,
        compiler_params=pltpu.CompilerParams(
            dimension_semantics=("parallel", "parallel"),
            vmem_limit_bytes=_vmem_limit(blocks, temp_bytes=temps)),
        name="mixer_inproj",
    )(*args)


def _stack_heads_transposed(q_ref, qt_ref, tq, rows=slice(None)):
    sub = lax.broadcasted_iota(jnp.int32, (PAIR, tq), 0)
    top = sub < HEAD_DIM
    for p in range(GROUP_COLS // PAIR):
        pair_t = q_ref[0, rows, p * PAIR:(p + 1) * PAIR].astype(F32).T
        qt_ref[:, (2 * p) * tq:(2 * p + 1) * tq] = jnp.where(top, pair_t, 0.0).astype(BF16)
        qt_ref[:, (2 * p + 1) * tq:(2 * p + 2) * tq] = jnp.where(top, 0.0, pair_t).astype(BF16)


def _store_heads(o_t, o_ref, tq, rows=slice(None)):
    for p in range(GROUP_COLS // PAIR):
        pair_t = jnp.concatenate([o_t[:, (2 * p) * tq:(2 * p + 1) * tq],
                                  o_t[:, (2 * p + 1) * tq:(2 * p + 2) * tq]], axis=0)
        o_ref[0, rows, p * PAIR:(p + 1) * PAIR] = pair_t.T.astype(o_ref.dtype)


def _flash_kernel(*refs, tq, n_tiles, n_chunks, has_sink):
    if has_sink:
        q_ref, k_ref, vt_ref, sink_ref, o_ref, qt_ref, s_ref, m_ref, acc_ref = refs
    else:
        q_ref, k_ref, vt_ref, o_ref, qt_ref, s_ref, m_ref, acc_ref = refs
    n_pairs = n_chunks // 2

    def tile_rows(i):
        return pl.ds(pl.multiple_of(i * tq, tq), tq)

    def scores(c):
        start = pl.multiple_of(c * KEY_CHUNK, KEY_CHUNK)
        return jnp.dot(k_ref[0, pl.ds(start, KEY_CHUNK), :], qt_ref[...],
                       preferred_element_type=F32)

    def begin_tile(i, slot):
        _stack_heads_transposed(q_ref, qt_ref, tq, tile_rows(i))
        s_ref[slot] = scores(0)

    def consume(slot, c):
        s = s_ref[slot]
        m_prev = m_ref[...]
        m_new = jnp.maximum(m_prev, jnp.max(s, axis=0, keepdims=True))
        alpha = jnp.exp2(m_prev - m_new)
        p = jnp.exp2(s - m_new)
        acc_ref[...] = alpha * acc_ref[...] + jnp.dot(vt_ref[0, c], p.astype(BF16),
                                                     preferred_element_type=F32)
        m_ref[...] = m_new

    def run_tile(i, x, next_tile):
        y = 1 - x
        if has_sink:
            m_ref[...] = sink_ref[0]
            den_rows = lax.broadcasted_iota(jnp.int32, acc_ref.shape, 0) >= HEAD_DIM
            acc_ref[...] = jnp.where(den_rows, 1.0, 0.0)
        else:
            m_ref[...] = jnp.full_like(m_ref, NEG)
            acc_ref[...] = jnp.zeros_like(acc_ref)

        def body(pair, carry):
            c = 2 * pair
            s_ref[y] = scores(c + 1)
            consume(x, c)
            s_ref[x] = scores(c + 2)
            consume(y, c + 1)
            return carry
        lax.fori_loop(0, n_pairs, body, 0, unroll=4 if n_pairs % 4 == 0 else 1)
        if next_tile is not None:
            begin_tile(next_tile, y)
        consume(x, n_chunks - 1)
        _store_heads(acc_ref[0:HEAD_DIM, :] / acc_ref[HEAD_DIM:HEAD_DIM + 1, :], o_ref, tq,
                     tile_rows(i))

    begin_tile(0, 0)
    if n_tiles == 1:
        run_tile(0, 0, None)
    else:
        def tile_pair(k, carry):
            i = 2 * k
            run_tile(i, 0, i + 1)
            run_tile(i + 1, 1, jnp.minimum(i + 2, n_tiles - 1))
            return carry
        lax.fori_loop(0, n_tiles // 2, tile_pair, 0)


def _flash_attention(q, k2, vt, sink_row=None, *, tq_want=256):
    b, t, hq = q.shape
    n_chunks = vt.shape[1]
    n_groups = hq // GROUP_COLS
    tq = _row_tile(t, tq_want)
    n_tiles = t // tq
    cols = GQA_GROUP * tq
    has_sink = sink_row is not None
    assert n_chunks % 2 == 1 and k2.shape[1] == n_chunks * KEY_CHUNK
    assert n_tiles == 1 or n_tiles % 2 == 0

    in_specs = [
        pl.BlockSpec((1, t, GROUP_COLS), lambda bi, j: (bi, 0, j)),
        pl.BlockSpec((1, n_chunks * KEY_CHUNK, PAIR), lambda bi, j: (bi, 0, j)),
        pl.BlockSpec((1, n_chunks, VT_ROWS, KEY_CHUNK), lambda bi, j: (bi, 0, j, 0)),
    ]
    args = [q, k2, vt]
    if has_sink:
        in_specs.append(pl.BlockSpec((1, 1, cols), lambda bi, j: (j, 0, 0)))
        args.append(sink_row)
    blocks = (2 * _nbytes((t, GROUP_COLS), BF16)
              + _nbytes((n_chunks * KEY_CHUNK, PAIR + VT_ROWS), BF16))
    scratch = (_nbytes((PAIR, cols), BF16) + _nbytes((2, KEY_CHUNK, cols), F32)
               + _nbytes((VT_ROWS + SUBLANES, cols), F32))
    temps = 2 * _nbytes((KEY_CHUNK, cols), F32)
    return pl.pallas_call(
        functools.partial(_flash_kernel, tq=tq, n_tiles=n_tiles, n_chunks=n_chunks,
                          has_sink=has_sink),
        grid=(b, n_groups),
        in_specs=in_specs,
        out_specs=pl.BlockSpec((1, t, GROUP_COLS), lambda bi, j: (bi, 0, j)),
        out_shape=jax.ShapeDtypeStruct((b, t, hq), BF16),
        scratch_shapes=[
            pltpu.VMEM((PAIR, cols), BF16),
            pltpu.VMEM((2, KEY_CHUNK, cols), F32),
            pltpu.VMEM((1, cols), F32),
            pltpu.VMEM((VT_ROWS, cols), F32),
        ],
        compiler_params=pltpu.CompilerParams(
            dimension_semantics=("parallel", "parallel"),
            vmem_limit_bytes=_vmem_limit(blocks, scratch, temps)),
        name="flash_attention",
    )(*args)


def _window_kernel(q_ref, k_ref, vt_ref, kc_ref, vtc_ref, bias_ref, sink_ref, o_ref,
                   qt_ref, sb_ref, sc_ref, *, tq, t):
    span = tq + 2 * WINDOW
    n_tiles = t // tq

    def window_start(i):
        return pl.multiple_of(jnp.clip(i * tq - WINDOW, 0, t - span), WINDOW)

    def issue(i, slot):
        start = window_start(i)
        bias = bias_ref[(i * tq - start) // WINDOW]
        _stack_heads_transposed(q_ref, qt_ref, tq, pl.ds(pl.multiple_of(i * tq, tq), tq))
        qt = qt_ref[...]
        s_band = jnp.dot(k_ref[0, pl.ds(start, span), :], qt, preferred_element_type=F32)
        sb_ref[slot] = s_band + jnp.concatenate([bias] * GQA_GROUP, axis=1)
        sc_ref[slot] = jnp.dot(kc_ref[0], qt, preferred_element_type=F32)

    def finish(i, slot):
        s_band, s_ctx = sb_ref[slot], sc_ref[slot]
        sink = sink_ref[0]
        m = jnp.maximum(jnp.maximum(jnp.max(s_band, axis=0, keepdims=True),
                                    jnp.max(s_ctx, axis=0, keepdims=True)), sink)
        p_band = jnp.exp2(s_band - m).astype(BF16)
        p_ctx = jnp.exp2(s_ctx - m).astype(BF16)
        first = window_start(i) // BAND_CHUNK
        vt_band = jnp.concatenate([vt_ref[0, first + c] for c in range(span // BAND_CHUNK)],
                                  axis=1)
        acc = (jnp.dot(vt_band, p_band, preferred_element_type=F32)
               + jnp.dot(vtc_ref[0, 0], p_ctx, preferred_element_type=F32))
        den = acc[HEAD_DIM:HEAD_DIM + 1, :] + jnp.exp2(sink - m)
        _store_heads(acc[0:HEAD_DIM, :] / den, o_ref, tq,
                     pl.ds(pl.multiple_of(i * tq, tq), tq))

    issue(0, 0)

    def body(pair, carry):
        i = 2 * pair
        issue(i + 1, 1)
        finish(i, 0)
        issue(i + 2, 0)
        finish(i + 1, 1)
        return carry
    lax.fori_loop(0, n_tiles // 2 - 1, body, 0)
    issue(n_tiles - 1, 1)
    finish(n_tiles - 2, 0)
    finish(n_tiles - 1, 1)


def _band_bias(tq):
    span = tq + 2 * WINDOW
    key = jnp.arange(span)[None, :, None]
    qry = jnp.arange(tq)[None, None, :]
    back = (jnp.arange(3) * WINDOW)[:, None, None]
    delta = qry + back - key
    return jnp.where(jnp.abs(delta) <= WINDOW, 0.0, NEG).astype(F32)


def _window_attention(q, k2, vt, kc2, vtc, sink_row, *, tq):
    b, t, hq = q.shape
    n_ctx = kc2.shape[1]
    n_groups = hq // GROUP_COLS
    cols = GQA_GROUP * tq
    span = tq + 2 * WINDOW
    n_band = t // BAND_CHUNK
    assert t % (2 * tq) == 0 and t >= span and tq % WINDOW == 0 and vtc.shape[1] == 1
    blocks = (2 * _nbytes((t, GROUP_COLS), BF16) + _nbytes((t, PAIR + VT_ROWS), BF16)
              + _nbytes((n_ctx, PAIR + VT_ROWS), BF16) + _nbytes((3, span, tq), F32))
    scratch = _nbytes((PAIR, cols), BF16) + 2 * _nbytes((span + n_ctx, cols), F32)
    temps = 2 * _nbytes((span + n_ctx, cols), F32)
    return pl.pallas_call(
        functools.partial(_window_kernel, tq=tq, t=t),
        grid=(b, n_groups),
        in_specs=[
            pl.BlockSpec((1, t, GROUP_COLS), lambda bi, j: (bi, 0, j)),
            pl.BlockSpec((1, t, PAIR), lambda bi, j: (bi, 0, j)),
            pl.BlockSpec((1, n_band, VT_ROWS, BAND_CHUNK), lambda bi, j: (bi, 0, j, 0)),
            pl.BlockSpec((1, n_ctx, PAIR), lambda bi, j: (bi, 0, j)),
            pl.BlockSpec((1, 1, VT_ROWS, n_ctx), lambda bi, j: (bi, 0, j, 0)),
            pl.BlockSpec((3, span, tq), lambda bi, j: (0, 0, 0)),
            pl.BlockSpec((1, 1, cols), lambda bi, j: (j, 0, 0)),
        ],
        out_specs=pl.BlockSpec((1, t, GROUP_COLS), lambda bi, j: (bi, 0, j)),
        out_shape=jax.ShapeDtypeStruct((b, t, hq), BF16),
        scratch_shapes=[
            pltpu.VMEM((PAIR, cols), BF16),
            pltpu.VMEM((2, span, cols), F32),
            pltpu.VMEM((2, n_ctx, cols), F32),
        ],
        compiler_params=pltpu.CompilerParams(
            dimension_semantics=("parallel", "parallel"),
            vmem_limit_bytes=_vmem_limit(blocks, scratch, temps)),
        name="window_attention",
    )(q, k2, vt, kc2, vtc, _band_bias(tq), sink_row)


def _pool_kernel(u_ref, w_ref, scale_ref, o_ref, pad_ref, *, t):
    reach = max(POOL_WINDOWS) // 2
    assert reach <= HALO
    pos = lax.broadcasted_iota(jnp.int32, (t, POOL_GC), 0)
    zeros = jnp.zeros((HALO, POOL_GC), F32)
    pad_ref[0:HALO, :] = zeros
    pad_ref[HALO + t:2 * HALO + t, :] = zeros
    for g, win in enumerate(POOL_WINDOWS):
        half = win // 2
        u = u_ref[0, :, g * POOL_GC:(g + 1) * POOL_GC]
        pad_ref[HALO:HALO + t, :] = u
        total = pad_ref[pl.ds(HALO - half, t), :]
        for off in range(1 - half, half):
            total = total + pad_ref[pl.ds(HALO + off, t), :]
        cnt = (jnp.minimum(pos + half, t) - jnp.maximum(pos - half, 0)).astype(F32)
        d = (total / cnt - u).astype(BF16)
        y = jnp.dot(d, w_ref[g], preferred_element_type=F32)
        o_ref[0, :, g * POOL_GC:(g + 1) * POOL_GC] = (
            y * scale_ref[:, g * POOL_GC:(g + 1) * POOL_GC]).astype(BF16)


def _pool_mix(u, pool_w, pool_scale):
    b, t, ch = u.shape
    n_g = len(POOL_WINDOWS)
    assert ch == n_g * POOL_GC
    blocks = _nbytes((t, ch), F32) + _nbytes((t, ch), BF16) + _nbytes((n_g, POOL_GC, POOL_GC), BF16)
    scratch = _nbytes((t + 2 * HALO, POOL_GC), F32)
    return pl.pallas_call(
        functools.partial(_pool_kernel, t=t),
        grid=(b,),
        in_specs=[
            pl.BlockSpec((1, t, ch), lambda bi: (bi, 0, 0)),
            pl.BlockSpec((n_g, POOL_GC, POOL_GC), lambda bi: (0, 0, 0)),
            pl.BlockSpec((1, ch), lambda bi: (0, 0)),
        ],
        out_specs=pl.BlockSpec((1, t, ch), lambda bi: (bi, 0, 0)),
        out_shape=jax.ShapeDtypeStruct((b, t, ch), BF16),
        scratch_shapes=[pltpu.VMEM((t + 2 * HALO, POOL_GC), F32)],
        compiler_params=pltpu.CompilerParams(
            dimension_semantics=("parallel",),
            vmem_limit_bytes=_vmem_limit(blocks, scratch, 6 * _nbytes((t, POOL_GC), F32))),
        name="pool_mix",
    )(u, pool_w, pool_scale)


def _outproj_kernel(*refs, n_in):
    a_refs = refs[:n_in]
    w_refs = refs[n_in:2 * n_in]
    x_ref, gate_ref, g_ref, o_ref = refs[2 * n_in:]
    y = jnp.dot(a_refs[0][0], w_refs[0][...], preferred_element_type=F32)
    for a_ref, w_ref in zip(a_refs[1:], w_refs[1:]):
        y = y + jnp.dot(a_ref[0], w_ref[...], preferred_element_type=F32)
    o_ref[0] = x_ref[0] + gate_ref[0] * _rms_norm(y, g_ref[...])


def _outproj(acts, weights, x, mods, mod_row, post_g):
    b, t, d = x.shape
    tm = _row_tile(t, 512)
    n_in = len(acts)
    in_specs = [pl.BlockSpec((1, tm, a.shape[2]), lambda bi, i: (bi, i, 0)) for a in acts]
    in_specs += [pl.BlockSpec(w.shape, lambda bi, i: (0, 0)) for w in weights]
    in_specs += [
        pl.BlockSpec((1, tm, d), lambda bi, i: (bi, i, 0)),
        pl.BlockSpec((1, 1, d), lambda bi, i: (mod_row(bi), 0, 2)),
        pl.BlockSpec((1, d), lambda bi, i: (0, 0)),
    ]
    blocks = (sum(_nbytes((tm, a.shape[2]), BF16) for a in acts)
              + sum(_nbytes(w.shape, BF16) for w in weights) + 2 * _nbytes((tm, d), F32))
    return pl.pallas_call(
        functools.partial(_outproj_kernel, n_in=n_in),
        grid=(b, t // tm),
        in_specs=in_specs,
        out_specs=pl.BlockSpec((1, tm, d), lambda bi, i: (bi, i, 0)),
        out_shape=jax.ShapeDtypeStruct((b, t, d), F32),
        compiler_params=pltpu.CompilerParams(
            dimension_semantics=("parallel", "parallel"),
            vmem_limit_bytes=_vmem_limit(blocks, temp_bytes=3 * _nbytes((tm, d), F32))),
        name="mixer_outproj",
    )(*acts, *weights, x, mods, post_g)


def _ffn_kernel(x_ref, xp_ref, xn_ref, sh_ref, sc_ref, gate_ref, pre_g_ref, post_g_ref,
                wu_ref, wut_ref, cw_ref, cwt_ref, cb_ref, cbt_ref, wd_ref, wdt_ref, o_ref,
                h_ref, acc_ref, u_ref, ut_ref, *, tm, n_main):
    i = pl.program_id(1)
    g, sh, sc = pre_g_ref[...], sh_ref[0], sc_ref[0]
    h_prev = _pre_mod(xp_ref[0], g, sh, sc)
    h_next = _pre_mod(xn_ref[0], g, sh, sc)
    h_prev = jnp.where(i == 0, 0.0, h_prev)
    h_next = jnp.where(i == pl.num_programs(1) - 1, 0.0, h_next)
    h_ref[0:HALO, :] = h_prev.astype(BF16)
    h_ref[HALO:HALO + tm, :] = _pre_mod(x_ref[0], g, sh, sc).astype(BF16)
    h_ref[HALO + tm:2 * HALO + tm, :] = h_next.astype(BF16)

    def up(c):
        return jnp.dot(h_ref[...], wu_ref[c], preferred_element_type=F32)

    def conv_down(u, cw, cb, wd):
        w = wd.shape[0]
        conv = (u[pl.ds(HALO - 1, tm), :] * cw[0:1, :]
                + u[pl.ds(HALO, tm), :] * cw[1:2, :]
                + u[pl.ds(HALO + 1, tm), :] * cw[2:3, :]
                + cb)
        gated = jax.nn.gelu(conv[:, :w]) * conv[:, w:]
        return jnp.dot(gated.astype(BF16), wd, preferred_element_type=F32)

    def consume(slot, c):
        return conv_down(u_ref.at[slot], cw_ref[c], cb_ref[c], wd_ref[c])

    u_ref[0] = up(0)
    acc_ref[...] = jnp.zeros_like(acc_ref)

    def body(pair, carry):
        c = 2 * pair
        u_ref[1] = up(c + 1)
        acc_ref[...] += consume(0, c)
        u_ref[0] = up(c + 2)
        acc_ref[...] += consume(1, c + 1)
        return carry
    lax.fori_loop(0, (n_main - 1) // 2, body, 0, unroll=True)
    last = n_main - 1
    if n_main % 2 == 0:
        u_ref[1] = up(last)
        acc_ref[...] += consume(0, last - 1)
    ut_ref[...] = jnp.dot(h_ref[...], wut_ref[...], preferred_element_type=F32)
    acc_ref[...] += consume(last % 2, last)
    y = acc_ref[...] + conv_down(ut_ref, cwt_ref[...], cbt_ref[...], wdt_ref[...])
    o_ref[0] = x_ref[0] + gate_ref[0] * _rms_norm(y, post_g_ref[...])


def _ffn_weights(w_up, conv_w, conv_b, w_down):
    d_ff = w_down.shape[0]
    n_main = d_ff // FFN_CHUNK
    split = n_main * FFN_CHUNK

    def pair_cols(a):
        r = a.shape[0]
        gelu, gate = a[:, :d_ff], a[:, d_ff:]
        main = jnp.concatenate([gelu[:, :split].reshape(r, n_main, FFN_CHUNK),
                                gate[:, :split].reshape(r, n_main, FFN_CHUNK)], axis=2)
        tail = jnp.concatenate([gelu[:, split:], gate[:, split:]], axis=1)
        return jnp.swapaxes(main, 0, 1), tail

    wu, wut = pair_cols(w_up.astype(BF16))
    cw, cwt = pair_cols(conv_w)
    cb, cbt = pair_cols(conv_b[None])
    wd = w_down.astype(BF16)
    return (wu, wut, cw, cwt, cb, cbt, wd[:split].reshape(n_main, FFN_CHUNK, -1), wd[split:])


def _conv_ffn(x, mods, mod_row, pre_g, post_g, weights, *, tm_want=512):
    b, t, d = x.shape
    tm = _row_tile(t, tm_want)
    n_main = weights[0].shape[0]
    tail = weights[-1].shape[0]
    assert tm % HALO == 0 and n_main >= 2 and tail > 0
    halo_blocks = tm // HALO
    last_halo = t // HALO - 1
    rows = tm + 2 * HALO

    def resident(a):
        zeros = (0,) * a.ndim
        return pl.BlockSpec(a.shape, lambda bi, i: zeros, pipeline_mode=pl.Buffered(1))

    in_specs = [
        pl.BlockSpec((1, tm, d), lambda bi, i: (bi, i, 0)),
        pl.BlockSpec((1, HALO, d), lambda bi, i: (bi, jnp.maximum(i * halo_blocks - 1, 0), 0)),
        pl.BlockSpec((1, HALO, d),
                     lambda bi, i: (bi, jnp.minimum((i + 1) * halo_blocks, last_halo), 0)),
        pl.BlockSpec((1, 1, d), lambda bi, i: (mod_row(bi), 0, 3)),
        pl.BlockSpec((1, 1, d), lambda bi, i: (mod_row(bi), 0, 4)),
        pl.BlockSpec((1, 1, d), lambda bi, i: (mod_row(bi), 0, 5)),
        pl.BlockSpec((1, d), lambda bi, i: (0, 0)),
        pl.BlockSpec((1, d), lambda bi, i: (0, 0)),
    ] + [resident(a) for a in weights]
    weight_bytes = sum(_nbytes(a.shape, a.dtype) for a in weights)
    blocks = 2 * _nbytes((tm, d), F32) + 2 * _nbytes((HALO, d), F32)
    scratch = (_nbytes((rows, d), BF16) + _nbytes((tm, d), F32)
               + _nbytes((2, rows, 2 * FFN_CHUNK), F32) + _nbytes((rows, 2 * tail), F32)
               + weight_bytes)
    temps = 4 * _nbytes((tm, 2 * FFN_CHUNK), F32) + _nbytes((tm, d), F32)
    return pl.pallas_call(
        functools.partial(_ffn_kernel, tm=tm, n_main=n_main),
        grid=(b, t // tm),
        in_specs=in_specs,
        out_specs=pl.BlockSpec((1, tm, d), lambda bi, i: (bi, i, 0)),
        out_shape=jax.ShapeDtypeStruct((b, t, d), F32),
        scratch_shapes=[
            pltpu.VMEM((rows, d), BF16),
            pltpu.VMEM((tm, d), F32),
            pltpu.VMEM((2, rows, 2 * FFN_CHUNK), F32),
            pltpu.VMEM((rows, 2 * tail), F32),
        ],
        compiler_params=pltpu.CompilerParams(
            dimension_semantics=("parallel", "parallel"),
            vmem_limit_bytes=_vmem_limit(blocks, scratch, temps)),
        name="conv_ffn",
    )(x, x, x, mods, mods, mods, pre_g, post_g, *weights)


def _rope_tables(t):
    rows = t // GRID_W
    row = jnp.repeat(jnp.arange(rows), GRID_W).astype(F32)
    col = jnp.tile(jnp.arange(GRID_W), rows).astype(F32)
    n_freq = HEAD_DIM // 4
    freqs = ROPE_BASE ** (-jnp.arange(n_freq, dtype=F32) / n_freq)
    ang = jnp.concatenate([row[:, None] * freqs, col[:, None] * freqs], axis=-1)
    cos, sin = jnp.cos(ang), jnp.sin(ang)
    return (jnp.concatenate([cos, cos, cos, cos], axis=-1),
            jnp.concatenate([-sin, sin, -sin, sin], axis=-1))


def _dup_heads(w, n_heads):
    d = w.shape[0]
    w = w.reshape(d, n_heads, 1, HEAD_DIM)
    return jnp.broadcast_to(w, (d, n_heads, 2, HEAD_DIM)).reshape(d, n_heads * PAIR)


def _sink_row(sink, tq):
    return jnp.repeat(sink.astype(F32) * LOG2E, tq, axis=1)[:, None, :]


def kernel(x, c, ctx, c_ctx, ada_w, ada_b, mix_pre_g, mix_post_g, ffn_pre_g, ffn_post_g,
           ab_w_in, ab_w_out, pool_w, pool_scale, sink_logit,
           c_w_qkv, c_w_out, c_q_g, c_k_g,
           ffn_w_up, ffn_conv_w, ffn_conv_b, ffn_w_down):
    batch, t, d = x.shape
    n_ctx = ctx.shape[1]
    depth = ada_w.shape[0]
    pool_ch = pool_scale.shape[1]
    b_kv = sink_logit.shape[1]
    b_q = b_kv * GQA_GROUP * HEAD_DIM
    c_q = c_w_out.shape[1]
    c_kv = c_q // HEAD_DIM // GQA_GROUP
    d_ff = ffn_w_down.shape[1]

    ctx_row = batch
    n_rows = -(-(batch + 1) // SUBLANES) * SUBLANES
    cc = jnp.zeros((n_rows, d), F32).at[:batch].set(c).at[ctx_row].set(c_ctx)
    mods_all = _ada_params(cc, ada_w, ada_b).reshape(depth, n_rows, 1, N_MOD * d)

    rope_tabs = _rope_tables(t)
    ones_bd = jnp.kron(jnp.eye(2, dtype=F32), jnp.ones((HEAD_DIM, HEAD_DIM), F32)).astype(BF16)
    latent_row = lambda bi: bi
    context_row = lambda bi: ctx_row
    win_tq = _row_tile(t, 256)
    ctx_tq = _row_tile(n_ctx, 256)

    def qkv_weights(w, n_lead, n_kv_heads):
        k_end = n_lead + n_kv_heads * HEAD_DIM
        return jnp.concatenate([w[:, :n_lead], _dup_heads(w[:, n_lead:k_end], n_kv_heads),
                                w[:, k_end:]], axis=1).astype(BF16)

    xc = ctx
    for l in range(depth):
        need_ctx = l < depth - 1
        i = l // 2
        mods = mods_all[l]
        pre_g, post_g = mix_pre_g[l][None], mix_post_g[l][None]
        if l % 2 == 0:
            w_in = qkv_weights(ab_w_in[i], pool_ch + b_q, b_kv)
            widths = dict(n_u=pool_ch, n_q=b_q, n_kv=b_kv * PAIR, n_v=b_kv * HEAD_DIM)
            u, q, k2, vt = _inproj(x, mods, latent_row, pre_g, w_in, rope_tabs=rope_tabs,
                                   key_chunk=BAND_CHUNK, **widths)
            uc, qc, kc2, vtc = _inproj(xc, mods, context_row, pre_g, w_in, **widths)
            w_out = ab_w_out[i].astype(BF16)
            w_outs = [w_out[:pool_ch], w_out[pool_ch:]]
            pw, ps = pool_w[i].astype(BF16), pool_scale[i][None]
            a = _window_attention(q, k2, vt, kc2, vtc, _sink_row(sink_logit[i], win_tq),
                                  tq=win_tq)
            x = _outproj([_pool_mix(u, pw, ps), a], w_outs, x, mods, latent_row, post_g)
            if need_ctx:
                ac = _flash_attention(qc, kc2, vtc, _sink_row(sink_logit[i], ctx_tq),
                                      tq_want=ctx_tq)
                xc = _outproj([_pool_mix(uc, pw, ps), ac], w_outs, xc, mods, context_row, post_g)
        else:
            w_in = qkv_weights(c_w_qkv[i], c_q, c_kv)
            gains = (jnp.tile(c_q_g[i], 2)[None], jnp.tile(c_k_g[i], 2)[None], ones_bd)
            widths = dict(n_u=0, n_q=c_q, n_kv=c_kv * PAIR, n_v=c_kv * HEAD_DIM)
            q, k2, vt = _inproj(x, mods, latent_row, pre_g, w_in, rope_tabs=rope_tabs,
                                qk_gains=gains, **widths)
            qc, kc2, vtc = _inproj(xc, mods, context_row, pre_g, w_in, qk_gains=gains, **widths)
            w_outs = [c_w_out[i].astype(BF16)]
            a = _flash_attention(q, jnp.concatenate([k2, kc2], axis=1),
                                 jnp.concatenate([vt, vtc], axis=1))
            x = _outproj([a], w_outs, x, mods, latent_row, post_g)
            if need_ctx:
                ac = _flash_attention(qc, kc2, vtc)
                xc = _outproj([ac], w_outs, xc, mods, context_row, post_g)

        ffn_args = (ffn_pre_g[l][None], ffn_post_g[l][None],
                    _ffn_weights(ffn_w_up[l], ffn_conv_w[l], ffn_conv_b[l], ffn_w_down[l]))
        x = _conv_ffn(x, mods, latent_row, *ffn_args)
        if need_ctx:
            xc = _conv_ffn(xc, mods, context_row, *ffn_args)
    return x
```

```python
import functools

import jax
import jax.numpy as jnp
from jax import lax
from jax.experimental import pallas as pl
from jax.experimental.pallas import tpu as pltpu

F32 = jnp.float32
BF16 = jnp.bfloat16

HEAD_DIM = 64
GQA_GROUP = 4
POOL_WINDOWS = (2, 4, 8, 16)
POOL_GC = 128
WINDOW = 128
GRID_W = 64
ROPE_BASE = 10000.0
EPS = 1e-6
NEG = -1e30
N_MOD = 6

LANES = 128
SUBLANES = 8
VMEM_LIMIT_CAP = 56 * 1024 * 1024
VMEM_LIMIT_FLOOR = 32 * 1024 * 1024

PAIR = 2 * HEAD_DIM
GROUP_COLS = GQA_GROUP * HEAD_DIM
HALO = SUBLANES
KEY_CHUNK = 256
BAND_CHUNK = WINDOW
LOG2E = 1.4426950408889634
Q_SCALE = HEAD_DIM ** -0.5 * LOG2E
ONES_ROWS = 16
VT_ROWS = HEAD_DIM + ONES_ROWS
FFN_CHUNK = 256


def _vmem_limit(block_bytes, scratch_bytes=0, temp_bytes=0):
    need = 2 * block_bytes + scratch_bytes + temp_bytes
    return int(min(max(need + need // 4, VMEM_LIMIT_FLOOR), VMEM_LIMIT_CAP))


def _nbytes(shape, dtype):
    n = 1
    for s in shape:
        n *= s
    return n * jnp.dtype(dtype).itemsize


def _row_tile(t, want):
    tile = min(t, want)
    assert t % tile == 0
    return tile


def _ada_kernel(c_ref, w_ref, b_ref, o_ref):
    c = c_ref[...]
    a = (c * jax.nn.sigmoid(c)).astype(BF16)
    o_ref[0] = jnp.dot(a, w_ref[0].astype(BF16), preferred_element_type=F32) + b_ref[0]


def _ada_params(cc, ada_w, ada_b):
    depth, d, n = ada_w.shape
    rows = cc.shape[0]
    tn = n // N_MOD
    blocks = _nbytes((rows, d), F32) + _nbytes((d, tn), F32) + _nbytes((rows, tn), F32)
    return pl.pallas_call(
        _ada_kernel,
        grid=(depth, n // tn),
        in_specs=[
            pl.BlockSpec((rows, d), lambda l, j: (0, 0)),
            pl.BlockSpec((1, d, tn), lambda l, j: (l, 0, j)),
            pl.BlockSpec((1, 1, tn), lambda l, j: (l, 0, j)),
        ],
        out_specs=pl.BlockSpec((1, rows, tn), lambda l, j: (l, 0, j)),
        out_shape=jax.ShapeDtypeStruct((depth, rows, n), F32),
        compiler_params=pltpu.CompilerParams(
            dimension_semantics=("parallel", "parallel"),
            vmem_limit_bytes=_vmem_limit(blocks, temp_bytes=_nbytes((d, tn), BF16))),
        name="ada_params",
    )(cc, ada_w, ada_b.reshape(depth, 1, n))


def _rms_norm(x, g):
    return x * lax.rsqrt(jnp.mean(x * x, axis=-1, keepdims=True) + EPS) * g


def _pre_mod(x, g, shift, scale):
    return _rms_norm(x, g) * (1.0 + scale) + shift


def _inproj_kernel(*refs, n_u, n_q, n_kv, n_v, key_chunk, rope, qk_norm, tm):
    x_ref, sh_ref, sc_ref, g_ref, w_ref = refs[:5]
    pos = 5
    if rope:
        cos_ref, sin_ref = refs[pos:pos + 2]
        pos += 2
    if qk_norm:
        qg_ref, kg_ref, ones_ref = refs[pos:pos + 3]
        pos += 3
    out_refs = refs[pos:]

    h = _pre_mod(x_ref[0], g_ref[...], sh_ref[0], sc_ref[0]).astype(BF16)

    if rope:
        lane = lax.broadcasted_iota(jnp.int32, (tm, PAIR), 1)
        first_half = (lane & (HEAD_DIM - 1)) < HEAD_DIM // 2
        cos = cos_ref[...]
        sin = sin_ref[...]

    def head_epilogue(a, gain_ref, scale):
        if qk_norm:
            a2 = a * a
            hi = a2.astype(BF16)
            lo = (a2 - hi.astype(F32)).astype(BF16)
            ss = (jnp.dot(hi, ones_ref[...], preferred_element_type=F32)
                  + jnp.dot(lo, ones_ref[...], preferred_element_type=F32))
            a = a * lax.rsqrt(ss * (1.0 / HEAD_DIM) + EPS) * gain_ref[...]
        if rope:
            partner = jnp.where(first_half,
                                pltpu.roll(a, PAIR - HEAD_DIM // 2, 1),
                                pltpu.roll(a, HEAD_DIM // 2, 1))
            a = a * cos + partner * sin
        if scale != 1.0:
            a = a * scale
        return a.astype(BF16)

    col = 0
    out_idx = 0
    if n_u:
        out_refs[out_idx][0] = jnp.dot(h, w_ref[:, col:col + n_u], preferred_element_type=F32)
        col += n_u
        out_idx += 1

    acc = jnp.dot(h, w_ref[:, col:col + n_q], preferred_element_type=F32)
    for c in range(n_q // PAIR):
        out_refs[out_idx][0, :, c * PAIR:(c + 1) * PAIR] = head_epilogue(
            acc[:, c * PAIR:(c + 1) * PAIR], qg_ref if qk_norm else None, Q_SCALE)
    col += n_q
    out_idx += 1

    acc = jnp.dot(h, w_ref[:, col:col + n_kv], preferred_element_type=F32)
    for c in range(n_kv // PAIR):
        out_refs[out_idx][0, :, c * PAIR:(c + 1) * PAIR] = head_epilogue(
            acc[:, c * PAIR:(c + 1) * PAIR], kg_ref if qk_norm else None, 1.0)
    col += n_kv
    out_idx += 1

    acc = jnp.dot(h, w_ref[:, col:col + n_v], preferred_element_type=F32)
    ones = jnp.ones((ONES_ROWS, key_chunk), F32)
    for cc in range(tm // key_chunk):
        at = acc[cc * key_chunk:(cc + 1) * key_chunk, :].T
        pieces = []
        for j in range(n_v // HEAD_DIM):
            pieces += [at[j * HEAD_DIM:(j + 1) * HEAD_DIM], ones]
        out_refs[out_idx][0, cc] = jnp.concatenate(pieces, axis=0).astype(BF16)


def _inproj(x, mods, mod_row, pre_g, w, *, n_u, n_q, n_kv, n_v, key_chunk=KEY_CHUNK,
            rope_tabs=None, qk_gains=None):
    b, t, d = x.shape
    n = w.shape[1]
    tm = _row_tile(t, 512)
    rope = rope_tabs is not None
    qk_norm = qk_gains is not None
    vt_rows = n_v // HEAD_DIM * VT_ROWS
    assert tm % key_chunk == 0

    in_specs = [
        pl.BlockSpec((1, tm, d), lambda bi, i: (bi, i, 0)),
        pl.BlockSpec((1, 1, d), lambda bi, i: (mod_row(bi), 0, 0)),
        pl.BlockSpec((1, 1, d), lambda bi, i: (mod_row(bi), 0, 1)),
        pl.BlockSpec((1, d), lambda bi, i: (0, 0)),
        pl.BlockSpec((d, n), lambda bi, i: (0, 0)),
    ]
    args = [x, mods, mods, pre_g, w]
    if rope:
        in_specs += [pl.BlockSpec((tm, PAIR), lambda bi, i: (i, 0))] * 2
        args += list(rope_tabs)
    if qk_norm:
        in_specs += [pl.BlockSpec((1, PAIR), lambda bi, i: (0, 0))] * 2
        in_specs += [pl.BlockSpec((PAIR, PAIR), lambda bi, i: (0, 0))]
        args += list(qk_gains)

    out_shape, out_specs = [], []
    for width, dtype in ((n_u, F32), (n_q, BF16), (n_kv, BF16)):
        if width:
            out_shape.append(jax.ShapeDtypeStruct((b, t, width), dtype))
            out_specs.append(pl.BlockSpec((1, tm, width), lambda bi, i: (bi, i, 0)))
    out_shape.append(jax.ShapeDtypeStruct((b, t // key_chunk, vt_rows, key_chunk), BF16))
    out_specs.append(pl.BlockSpec((1, tm // key_chunk, vt_rows, key_chunk),
                                  lambda bi, i: (bi, i, 0, 0)))

    blocks = (_nbytes((tm, d), F32) + _nbytes((d, n), BF16) + _nbytes((tm, n_u), F32)
              + _nbytes((tm, n_q + n_kv + 2 * n_v), BF16) + 2 * _nbytes((tm, PAIR), F32))
    temps = _nbytes((tm, d), F32) * 2 + _nbytes((tm, max(n_q, n_kv, n_u)), F32) * 2
    return pl.pallas_call(
        functools.partial(_inproj_kernel, n_u=n_u, n_q=n_q, n_kv=n_kv, n_v=n_v,
                          key_chunk=key_chunk, rope=rope, qk_norm=qk_norm, tm=tm),
        grid=(b, t // tm),
        in_specs=in_specs,
        out_specs=out_specs,
        out_shape=out_shape,
        compiler_params=pltpu.CompilerParams(
            dimension_semantics=("parallel", "parallel"),
            vmem_limit_bytes=_vmem_limit(blocks, temp_bytes=temps)),
        name="mixer_inproj",
    )(*args)


def _stack_heads_transposed(q_ref, qt_ref, tq, rows=slice(None)):
    sub = lax.broadcasted_iota(jnp.int32, (PAIR, tq), 0)
    top = sub < HEAD_DIM
    for p in range(GROUP_COLS // PAIR):
        pair_t = q_ref[0, rows, p * PAIR:(p + 1) * PAIR].astype(F32).T
        qt_ref[:, (2 * p) * tq:(2 * p + 1) * tq] = jnp.where(top, pair_t, 0.0).astype(BF16)
        qt_ref[:, (2 * p + 1) * tq:(2 * p + 2) * tq] = jnp.where(top, 0.0, pair_t).astype(BF16)


def _store_heads(o_t, o_ref, tq, rows=slice(None)):
    for p in range(GROUP_COLS // PAIR):
        pair_t = jnp.concatenate([o_t[:, (2 * p) * tq:(2 * p + 1) * tq],
                                  o_t[:, (2 * p + 1) * tq:(2 * p + 2) * tq]], axis=0)
        o_ref[0, rows, p * PAIR:(p + 1) * PAIR] = pair_t.T.astype(o_ref.dtype)


def _flash_kernel(*refs, tq, n_tiles, n_chunks, has_sink):
    if has_sink:
        q_ref, k_ref, vt_ref, sink_ref, o_ref, qt_ref, s_ref, mx_ref, m_ref, acc_ref = refs
    else:
        q_ref, k_ref, vt_ref, o_ref, qt_ref, s_ref, mx_ref, m_ref, acc_ref = refs
    n_pairs = n_chunks // 2

    def tile_rows(i):
        return pl.ds(pl.multiple_of(i * tq, tq), tq)

    def issue(c, slot):
        start = pl.multiple_of(c * KEY_CHUNK, KEY_CHUNK)
        s = jnp.dot(k_ref[0, pl.ds(start, KEY_CHUNK), :], qt_ref[...],
                    preferred_element_type=F32)
        s_ref[slot] = s
        mx_ref[slot] = jnp.max(s, axis=0, keepdims=True)

    def begin_tile(i, slot):
        _stack_heads_transposed(q_ref, qt_ref, tq, tile_rows(i))
        issue(0, slot)

    def consume(slot, c):
        m_prev = m_ref[...]
        m_new = jnp.maximum(m_prev, mx_ref[slot])
        alpha = jnp.exp2(m_prev - m_new)
        p = jnp.exp2(s_ref[slot] - m_new)
        acc_ref[...] = alpha * acc_ref[...] + jnp.dot(vt_ref[0, c], p.astype(BF16),
                                                     preferred_element_type=F32)
        m_ref[...] = m_new

    def run_tile(i, x, next_tile):
        y = 1 - x
        if has_sink:
            m_ref[...] = sink_ref[0]
            den_rows = lax.broadcasted_iota(jnp.int32, acc_ref.shape, 0) >= HEAD_DIM
            acc_ref[...] = jnp.where(den_rows, 1.0, 0.0)
        else:
            m_ref[...] = jnp.full_like(m_ref, NEG)
            acc_ref[...] = jnp.zeros_like(acc_ref)

        def body(pair, carry):
            c = 2 * pair
            issue(c + 1, y)
            consume(x, c)
            issue(c + 2, x)
            consume(y, c + 1)
            return carry
        lax.fori_loop(0, n_pairs, body, 0, unroll=4 if n_pairs % 4 == 0 else 1)
        if next_tile is not None:
            begin_tile(next_tile, y)
        consume(x, n_chunks - 1)
        _store_heads(acc_ref[0:HEAD_DIM, :] / acc_ref[HEAD_DIM:HEAD_DIM + 1, :], o_ref, tq,
                     tile_rows(i))

    begin_tile(0, 0)
    if n_tiles == 1:
        run_tile(0, 0, None)
    else:
        def tile_pair(k, carry):
            i = 2 * k
            run_tile(i, 0, i + 1)
            run_tile(i + 1, 1, jnp.minimum(i + 2, n_tiles - 1))
            return carry
        lax.fori_loop(0, n_tiles // 2, tile_pair, 0)


def _flash_attention(q, k2, vt, sink_row=None, *, tq_want=256):
    b, t, hq = q.shape
    n_chunks = vt.shape[1]
    n_groups = hq // GROUP_COLS
    tq = _row_tile(t, tq_want)
    n_tiles = t // tq
    cols = GQA_GROUP * tq
    has_sink = sink_row is not None
    assert n_chunks % 2 == 1 and k2.shape[1] == n_chunks * KEY_CHUNK
    assert n_tiles == 1 or n_tiles % 2 == 0

    in_specs = [
        pl.BlockSpec((1, t, GROUP_COLS), lambda bi, j: (bi, 0, j)),
        pl.BlockSpec((1, n_chunks * KEY_CHUNK, PAIR), lambda bi, j: (bi, 0, j)),
        pl.BlockSpec((1, n_chunks, VT_ROWS, KEY_CHUNK), lambda bi, j: (bi, 0, j, 0)),
    ]
    args = [q, k2, vt]
    if has_sink:
        in_specs.append(pl.BlockSpec((1, 1, cols), lambda bi, j: (j, 0, 0)))
        args.append(sink_row)
    blocks = (2 * _nbytes((t, GROUP_COLS), BF16)
              + _nbytes((n_chunks * KEY_CHUNK, PAIR + VT_ROWS), BF16))
    scratch = (_nbytes((PAIR, cols), BF16) + _nbytes((2, KEY_CHUNK, cols), F32)
               + _nbytes((VT_ROWS + SUBLANES, cols), F32))
    temps = 2 * _nbytes((KEY_CHUNK, cols), F32)
    return pl.pallas_call(
        functools.partial(_flash_kernel, tq=tq, n_tiles=n_tiles, n_chunks=n_chunks,
                          has_sink=has_sink),
        grid=(b, n_groups),
        in_specs=in_specs,
        out_specs=pl.BlockSpec((1, t, GROUP_COLS), lambda bi, j: (bi, 0, j)),
        out_shape=jax.ShapeDtypeStruct((b, t, hq), BF16),
        scratch_shapes=[
            pltpu.VMEM((PAIR, cols), BF16),
            pltpu.VMEM((2, KEY_CHUNK, cols), F32),
            pltpu.VMEM((2, 1, cols), F32),
            pltpu.VMEM((1, cols), F32),
            pltpu.VMEM((VT_ROWS, cols), F32),
        ],
        compiler_params=pltpu.CompilerParams(
            dimension_semantics=("parallel", "parallel"),
            vmem_limit_bytes=_vmem_limit(blocks, scratch, temps)),
        name="flash_attention",
    )(*args)


def _window_kernel(q_ref, k_ref, vt_ref, kc_ref, vtc_ref, bias_ref, sink_ref, o_ref,
                   qt_ref, sb_ref, sc_ref, mx_ref, *, tq, t):
    span = tq + 2 * WINDOW
    n_tiles = t // tq

    def window_start(i):
        return pl.multiple_of(jnp.clip(i * tq - WINDOW, 0, t - span), WINDOW)

    def issue(i, slot):
        start = window_start(i)
        bias = bias_ref[(i * tq - start) // WINDOW]
        _stack_heads_transposed(q_ref, qt_ref, tq, pl.ds(pl.multiple_of(i * tq, tq), tq))
        qt = qt_ref[...]
        s_band = jnp.dot(k_ref[0, pl.ds(start, span), :], qt, preferred_element_type=F32)
        s_band = s_band + jnp.concatenate([bias] * GQA_GROUP, axis=1)
        s_ctx = jnp.dot(kc_ref[0], qt, preferred_element_type=F32)
        sb_ref[slot] = s_band
        sc_ref[slot] = s_ctx
        mx_ref[slot] = jnp.maximum(jnp.max(s_band, axis=0, keepdims=True),
                                   jnp.max(s_ctx, axis=0, keepdims=True))

    def finish(i, slot):
        sink = sink_ref[0]
        m = jnp.maximum(mx_ref[slot], sink)
        p_band = jnp.exp2(sb_ref[slot] - m).astype(BF16)
        p_ctx = jnp.exp2(sc_ref[slot] - m).astype(BF16)
        first = window_start(i) // BAND_CHUNK
        vt_band = jnp.concatenate([vt_ref[0, first + c] for c in range(span // BAND_CHUNK)],
                                  axis=1)
        acc = (jnp.dot(vt_band, p_band, preferred_element_type=F32)
               + jnp.dot(vtc_ref[0, 0], p_ctx, preferred_element_type=F32))
        den = acc[HEAD_DIM:HEAD_DIM + 1, :] + jnp.exp2(sink - m)
        _store_heads(acc[0:HEAD_DIM, :] / den, o_ref, tq,
                     pl.ds(pl.multiple_of(i * tq, tq), tq))

    issue(0, 0)

    def body(pair, carry):
        i = 2 * pair
        issue(i + 1, 1)
        finish(i, 0)
        issue(i + 2, 0)
        finish(i + 1, 1)
        return carry
    lax.fori_loop(0, n_tiles // 2 - 1, body, 0)
    issue(n_tiles - 1, 1)
    finish(n_tiles - 2, 0)
    finish(n_tiles - 1, 1)


def _band_bias(tq):
    span = tq + 2 * WINDOW
    key = jnp.arange(span)[None, :, None]
    qry = jnp.arange(tq)[None, None, :]
    back = (jnp.arange(3) * WINDOW)[:, None, None]
    delta = qry + back - key
    return jnp.where(jnp.abs(delta) <= WINDOW, 0.0, NEG).astype(F32)


def _window_attention(q, k2, vt, kc2, vtc, sink_row, *, tq):
    b, t, hq = q.shape
    n_ctx = kc2.shape[1]
    n_groups = hq // GROUP_COLS
    cols = GQA_GROUP * tq
    span = tq + 2 * WINDOW
    n_band = t // BAND_CHUNK
    assert t % (2 * tq) == 0 and t >= span and tq % WINDOW == 0 and vtc.shape[1] == 1
    blocks = (2 * _nbytes((t, GROUP_COLS), BF16) + _nbytes((t, PAIR + VT_ROWS), BF16)
              + _nbytes((n_ctx, PAIR + VT_ROWS), BF16) + _nbytes((3, span, tq), F32))
    scratch = _nbytes((PAIR, cols), BF16) + 2 * _nbytes((span + n_ctx, cols), F32)
    temps = 2 * _nbytes((span + n_ctx, cols), F32)
    return pl.pallas_call(
        functools.partial(_window_kernel, tq=tq, t=t),
        grid=(b, n_groups),
        in_specs=[
            pl.BlockSpec((1, t, GROUP_COLS), lambda bi, j: (bi, 0, j)),
            pl.BlockSpec((1, t, PAIR), lambda bi, j: (bi, 0, j)),
            pl.BlockSpec((1, n_band, VT_ROWS, BAND_CHUNK), lambda bi, j: (bi, 0, j, 0)),
            pl.BlockSpec((1, n_ctx, PAIR), lambda bi, j: (bi, 0, j)),
            pl.BlockSpec((1, 1, VT_ROWS, n_ctx), lambda bi, j: (bi, 0, j, 0)),
            pl.BlockSpec((3, span, tq), lambda bi, j: (0, 0, 0)),
            pl.BlockSpec((1, 1, cols), lambda bi, j: (j, 0, 0)),
        ],
        out_specs=pl.BlockSpec((1, t, GROUP_COLS), lambda bi, j: (bi, 0, j)),
        out_shape=jax.ShapeDtypeStruct((b, t, hq), BF16),
        scratch_shapes=[
            pltpu.VMEM((PAIR, cols), BF16),
            pltpu.VMEM((2, span, cols), F32),
            pltpu.VMEM((2, n_ctx, cols), F32),
            pltpu.VMEM((2, 1, cols), F32),
        ],
        compiler_params=pltpu.CompilerParams(
            dimension_semantics=("parallel", "parallel"),
            vmem_limit_bytes=_vmem_limit(blocks, scratch, temps)),
        name="window_attention",
    )(q, k2, vt, kc2, vtc, _band_bias(tq), sink_row)


def _pool_kernel(u_ref, w_ref, scale_ref, o_ref, pad_ref, *, t):
    reach = max(POOL_WINDOWS) // 2
    assert reach <= HALO
    pos = lax.broadcasted_iota(jnp.int32, (t, POOL_GC), 0)
    zeros = jnp.zeros((HALO, POOL_GC), F32)
    pad_ref[0:HALO, :] = zeros
    pad_ref[HALO + t:2 * HALO + t, :] = zeros
    for g, win in enumerate(POOL_WINDOWS):
        half = win // 2
        u = u_ref[0, :, g * POOL_GC:(g + 1) * POOL_GC]
        pad_ref[HALO:HALO + t, :] = u
        total = pad_ref[pl.ds(HALO - half, t), :]
        for off in range(1 - half, half):
            total = total + pad_ref[pl.ds(HALO + off, t), :]
        cnt = (jnp.minimum(pos + half, t) - jnp.maximum(pos - half, 0)).astype(F32)
        d = (total / cnt - u).astype(BF16)
        y = jnp.dot(d, w_ref[g], preferred_element_type=F32)
        o_ref[0, :, g * POOL_GC:(g + 1) * POOL_GC] = (
            y * scale_ref[:, g * POOL_GC:(g + 1) * POOL_GC]).astype(BF16)


def _pool_mix(u, pool_w, pool_scale):
    b, t, ch = u.shape
    n_g = len(POOL_WINDOWS)
    assert ch == n_g * POOL_GC
    blocks = _nbytes((t, ch), F32) + _nbytes((t, ch), BF16) + _nbytes((n_g, POOL_GC, POOL_GC), BF16)
    scratch = _nbytes((t + 2 * HALO, POOL_GC), F32)
    return pl.pallas_call(
        functools.partial(_pool_kernel, t=t),
        grid=(b,),
        in_specs=[
            pl.BlockSpec((1, t, ch), lambda bi: (bi, 0, 0)),
            pl.BlockSpec((n_g, POOL_GC, POOL_GC), lambda bi: (0, 0, 0)),
            pl.BlockSpec((1, ch), lambda bi: (0, 0)),
        ],
        out_specs=pl.BlockSpec((1, t, ch), lambda bi: (bi, 0, 0)),
        out_shape=jax.ShapeDtypeStruct((b, t, ch), BF16),
        scratch_shapes=[pltpu.VMEM((t + 2 * HALO, POOL_GC), F32)],
        compiler_params=pltpu.CompilerParams(
            dimension_semantics=("parallel",),
            vmem_limit_bytes=_vmem_limit(blocks, scratch, 6 * _nbytes((t, POOL_GC), F32))),
        name="pool_mix",
    )(u, pool_w, pool_scale)


def _outproj_kernel(*refs, n_in):
    a_refs = refs[:n_in]
    w_refs = refs[n_in:2 * n_in]
    x_ref, gate_ref, g_ref, o_ref = refs[2 * n_in:]
    y = jnp.dot(a_refs[0][0], w_refs[0][...], preferred_element_type=F32)
    for a_ref, w_ref in zip(a_refs[1:], w_refs[1:]):
        y = y + jnp.dot(a_ref[0], w_ref[...], preferred_element_type=F32)
    o_ref[0] = x_ref[0] + gate_ref[0] * _rms_norm(y, g_ref[...])


def _outproj(acts, weights, x, mods, mod_row, post_g):
    b, t, d = x.shape
    tm = _row_tile(t, 512)
    n_in = len(acts)
    in_specs = [pl.BlockSpec((1, tm, a.shape[2]), lambda bi, i: (bi, i, 0)) for a in acts]
    in_specs += [pl.BlockSpec(w.shape, lambda bi, i: (0, 0)) for w in weights]
    in_specs += [
        pl.BlockSpec((1, tm, d), lambda bi, i: (bi, i, 0)),
        pl.BlockSpec((1, 1, d), lambda bi, i: (mod_row(bi), 0, 2)),
        pl.BlockSpec((1, d), lambda bi, i: (0, 0)),
    ]
    blocks = (sum(_nbytes((tm, a.shape[2]), BF16) for a in acts)
              + sum(_nbytes(w.shape, BF16) for w in weights) + 2 * _nbytes((tm, d), F32))
    return pl.pallas_call(
        functools.partial(_outproj_kernel, n_in=n_in),
        grid=(b, t // tm),
        in_specs=in_specs,
        out_specs=pl.BlockSpec((1, tm, d), lambda bi, i: (bi, i, 0)),
        out_shape=jax.ShapeDtypeStruct((b, t, d), F32),
        compiler_params=pltpu.CompilerParams(
            dimension_semantics=("parallel", "parallel"),
            vmem_limit_bytes=_vmem_limit(blocks, temp_bytes=3 * _nbytes((tm, d), F32))),
        name="mixer_outproj",
    )(*acts, *weights, x, mods, post_g)


def _ffn_kernel(x_ref, xp_ref, xn_ref, sh_ref, sc_ref, gate_ref, pre_g_ref, post_g_ref,
                wu_ref, wut_ref, cw_ref, cwt_ref, cb_ref, cbt_ref, wd_ref, wdt_ref, o_ref,
                h_ref, acc_ref, u_ref, ut_ref, *, tm, n_main):
    i = pl.program_id(1)
    g, sh, sc = pre_g_ref[...], sh_ref[0], sc_ref[0]
    h_prev = _pre_mod(xp_ref[0], g, sh, sc)
    h_next = _pre_mod(xn_ref[0], g, sh, sc)
    h_prev = jnp.where(i == 0, 0.0, h_prev)
    h_next = jnp.where(i == pl.num_programs(1) - 1, 0.0, h_next)
    h_ref[0:HALO, :] = h_prev.astype(BF16)
    h_ref[HALO:HALO + tm, :] = _pre_mod(x_ref[0], g, sh, sc).astype(BF16)
    h_ref[HALO + tm:2 * HALO + tm, :] = h_next.astype(BF16)

    def up(c):
        return jnp.dot(h_ref[...], wu_ref[c], preferred_element_type=F32)

    def conv_down(u, cw, cb, wd):
        w = wd.shape[0]
        conv = (u[pl.ds(HALO - 1, tm), :] * cw[0:1, :]
                + u[pl.ds(HALO, tm), :] * cw[1:2, :]
                + u[pl.ds(HALO + 1, tm), :] * cw[2:3, :]
                + cb)
        gated = jax.nn.gelu(conv[:, :w]) * conv[:, w:]
        return jnp.dot(gated.astype(BF16), wd, preferred_element_type=F32)

    def consume(slot, c):
        return conv_down(u_ref.at[slot], cw_ref[c], cb_ref[c], wd_ref[c])

    u_ref[0] = up(0)
    acc_ref[...] = jnp.zeros_like(acc_ref)

    def body(pair, carry):
        c = 2 * pair
        u_ref[1] = up(c + 1)
        acc_ref[...] += consume(0, c)
        u_ref[0] = up(c + 2)
        acc_ref[...] += consume(1, c + 1)
        return carry
    lax.fori_loop(0, (n_main - 1) // 2, body, 0, unroll=True)
    last = n_main - 1
    if n_main % 2 == 0:
        u_ref[1] = up(last)
        acc_ref[...] += consume(0, last - 1)
    ut_ref[...] = jnp.dot(h_ref[...], wut_ref[...], preferred_element_type=F32)
    acc_ref[...] += consume(last % 2, last)
    y = acc_ref[...] + conv_down(ut_ref, cwt_ref[...], cbt_ref[...], wdt_ref[...])
    o_ref[0] = x_ref[0] + gate_ref[0] * _rms_norm(y, post_g_ref[...])


def _ffn_weights(w_up, conv_w, conv_b, w_down):
    d_ff = w_down.shape[0]
    n_main = d_ff // FFN_CHUNK
    split = n_main * FFN_CHUNK

    def pair_cols(a):
        r = a.shape[0]
        gelu, gate = a[:, :d_ff], a[:, d_ff:]
        main = jnp.concatenate([gelu[:, :split].reshape(r, n_main, FFN_CHUNK),
                                gate[:, :split].reshape(r, n_main, FFN_CHUNK)], axis=2)
        tail = jnp.concatenate([gelu[:, split:], gate[:, split:]], axis=1)
        return jnp.swapaxes(main, 0, 1), tail

    wu, wut = pair_cols(w_up.astype(BF16))
    cw, cwt = pair_cols(conv_w)
    cb, cbt = pair_cols(conv_b[None])
    wd = w_down.astype(BF16)
    return (wu, wut, cw, cwt, cb, cbt, wd[:split].reshape(n_main, FFN_CHUNK, -1), wd[split:])


def _conv_ffn(x, mods, mod_row, pre_g, post_g, weights, *, tm_want=512):
    b, t, d = x.shape
    tm = _row_tile(t, tm_want)
    n_main = weights[0].shape[0]
    tail = weights[-1].shape[0]
    assert tm % HALO == 0 and n_main >= 2 and tail > 0
    halo_blocks = tm // HALO
    last_halo = t // HALO - 1
    rows = tm + 2 * HALO

    def resident(a):
        zeros = (0,) * a.ndim
        return pl.BlockSpec(a.shape, lambda bi, i: zeros, pipeline_mode=pl.Buffered(1))

    in_specs = [
        pl.BlockSpec((1, tm, d), lambda bi, i: (bi, i, 0)),
        pl.BlockSpec((1, HALO, d), lambda bi, i: (bi, jnp.maximum(i * halo_blocks - 1, 0), 0)),
        pl.BlockSpec((1, HALO, d),
                     lambda bi, i: (bi, jnp.minimum((i + 1) * halo_blocks, last_halo), 0)),
        pl.BlockSpec((1, 1, d), lambda bi, i: (mod_row(bi), 0, 3)),
        pl.BlockSpec((1, 1, d), lambda bi, i: (mod_row(bi), 0, 4)),
        pl.BlockSpec((1, 1, d), lambda bi, i: (mod_row(bi), 0, 5)),
        pl.BlockSpec((1, d), lambda bi, i: (0, 0)),
        pl.BlockSpec((1, d), lambda bi, i: (0, 0)),
    ] + [resident(a) for a in weights]
    weight_bytes = sum(_nbytes(a.shape, a.dtype) for a in weights)
    blocks = 2 * _nbytes((tm, d), F32) + 2 * _nbytes((HALO, d), F32)
    scratch = (_nbytes((rows, d), BF16) + _nbytes((tm, d), F32)
               + _nbytes((2, rows, 2 * FFN_CHUNK), F32) + _nbytes((rows, 2 * tail), F32)
               + weight_bytes)
    temps = 4 * _nbytes((tm, 2 * FFN_CHUNK), F32) + _nbytes((tm, d), F32)
    return pl.pallas_call(
        functools.partial(_ffn_kernel, tm=tm, n_main=n_main),
        grid=(b, t // tm),
        in_specs=in_specs,
        out_specs=pl.BlockSpec((1, tm, d), lambda bi, i: (bi, i, 0)),
        out_shape=jax.ShapeDtypeStruct((b, t, d), F32),
        scratch_shapes=[
            pltpu.VMEM((rows, d), BF16),
            pltpu.VMEM((tm, d), F32),
            pltpu.VMEM((2, rows, 2 * FFN_CHUNK), F32),
            pltpu.VMEM((rows, 2 * tail), F32),
        ],
        compiler_params=pltpu.CompilerParams(
            dimension_semantics=("parallel", "parallel"),
            vmem_limit_bytes=_vmem_limit(blocks, scratch, temps)),
        name="conv_ffn",
    )(x, x, x, mods, mods, mods, pre_g, post_g, *weights)


def _rope_tables(t):
    rows = t // GRID_W
    row = jnp.repeat(jnp.arange(rows), GRID_W).astype(F32)
    col = jnp.tile(jnp.arange(GRID_W), rows).astype(F32)
    n_freq = HEAD_DIM // 4
    freqs = ROPE_BASE ** (-jnp.arange(n_freq, dtype=F32) / n_freq)
    ang = jnp.concatenate([row[:, None] * freqs, col[:, None] * freqs], axis=-1)
    cos, sin = jnp.cos(ang), jnp.sin(ang)
    return (jnp.concatenate([cos, cos, cos, cos], axis=-1),
            jnp.concatenate([-sin, sin, -sin, sin], axis=-1))


def _dup_heads(w, n_heads):
    d = w.shape[0]
    w = w.reshape(d, n_heads, 1, HEAD_DIM)
    return jnp.broadcast_to(w, (d, n_heads, 2, HEAD_DIM)).reshape(d, n_heads * PAIR)


def _sink_row(sink, tq):
    return jnp.repeat(sink.astype(F32) * LOG2E, tq, axis=1)[:, None, :]


def kernel(x, c, ctx, c_ctx, ada_w, ada_b, mix_pre_g, mix_post_g, ffn_pre_g, ffn_post_g,
           ab_w_in, ab_w_out, pool_w, pool_scale, sink_logit,
           c_w_qkv, c_w_out, c_q_g, c_k_g,
           ffn_w_up, ffn_conv_w, ffn_conv_b, ffn_w_down):
    batch, t, d = x.shape
    n_ctx = ctx.shape[1]
    depth = ada_w.shape[0]
    pool_ch = pool_scale.shape[1]
    b_kv = sink_logit.shape[1]
    b_q = b_kv * GQA_GROUP * HEAD_DIM
    c_q = c_w_out.shape[1]
    c_kv = c_q // HEAD_DIM // GQA_GROUP
    d_ff = ffn_w_down.shape[1]

    ctx_row = batch
    n_rows = -(-(batch + 1) // SUBLANES) * SUBLANES
    cc = jnp.zeros((n_rows, d), F32).at[:batch].set(c).at[ctx_row].set(c_ctx)
    mods_all = _ada_params(cc, ada_w, ada_b).reshape(depth, n_rows, 1, N_MOD * d)

    rope_tabs = _rope_tables(t)
    ones_bd = jnp.kron(jnp.eye(2, dtype=F32), jnp.ones((HEAD_DIM, HEAD_DIM), F32)).astype(BF16)
    latent_row = lambda bi: bi
    context_row = lambda bi: ctx_row
    win_tq = _row_tile(t, 256)
    ctx_tq = _row_tile(n_ctx, 256)

    def qkv_weights(w, n_lead, n_kv_heads):
        k_end = n_lead + n_kv_heads * HEAD_DIM
        return jnp.concatenate([w[:, :n_lead], _dup_heads(w[:, n_lead:k_end], n_kv_heads),
                                w[:, k_end:]], axis=1).astype(BF16)

    xc = ctx
    for l in range(depth):
        need_ctx = l < depth - 1
        i = l // 2
        mods = mods_all[l]
        pre_g, post_g = mix_pre_g[l][None], mix_post_g[l][None]
        if l % 2 == 0:
            w_in = qkv_weights(ab_w_in[i], pool_ch + b_q, b_kv)
            widths = dict(n_u=pool_ch, n_q=b_q, n_kv=b_kv * PAIR, n_v=b_kv * HEAD_DIM)
            u, q, k2, vt = _inproj(x, mods, latent_row, pre_g, w_in, rope_tabs=rope_tabs,
                                   key_chunk=BAND_CHUNK, **widths)
            uc, qc, kc2, vtc = _inproj(xc, mods, context_row, pre_g, w_in, **widths)
            w_out = ab_w_out[i].astype(BF16)
            w_outs = [w_out[:pool_ch], w_out[pool_ch:]]
            pw, ps = pool_w[i].astype(BF16), pool_scale[i][None]
            a = _window_attention(q, k2, vt, kc2, vtc, _sink_row(sink_logit[i], win_tq),
                                  tq=win_tq)
            x = _outproj([_pool_mix(u, pw, ps), a], w_outs, x, mods, latent_row, post_g)
            if need_ctx:
                ac = _flash_attention(qc, kc2, vtc, _sink_row(sink_logit[i], ctx_tq),
                                      tq_want=ctx_tq)
                xc = _outproj([_pool_mix(uc, pw, ps), ac], w_outs, xc, mods, context_row, post_g)
        else:
            w_in = qkv_weights(c_w_qkv[i], c_q, c_kv)
            gains = (jnp.tile(c_q_g[i], 2)[None], jnp.tile(c_k_g[i], 2)[None], ones_bd)
            widths = dict(n_u=0, n_q=c_q, n_kv=c_kv * PAIR, n_v=c_kv * HEAD_DIM)
            q, k2, vt = _inproj(x, mods, latent_row, pre_g, w_in, rope_tabs=rope_tabs,
                                qk_gains=gains, **widths)
            qc, kc2, vtc = _inproj(xc, mods, context_row, pre_g, w_in, qk_gains=gains, **widths)
            w_outs = [c_w_out[i].astype(BF16)]
            a = _flash_attention(q, jnp.concatenate([k2, kc2], axis=1),
                                 jnp.concatenate([vt, vtc], axis=1))
            x = _outproj([a], w_outs, x, mods, latent_row, post_g)
            if need_ctx:
                ac = _flash_attention(qc, kc2, vtc)
                xc = _outproj([ac], w_outs, xc, mods, context_row, post_g)

        ffn_args = (ffn_pre_g[l][None], ffn_post_g[l][None],
                    _ffn_weights(ffn_w_up[l], ffn_conv_w[l], ffn_conv_b[l], ffn_w_down[l]))
        x = _conv_ffn(x, mods, latent_row, *ffn_args)
        if need_ctx:
            xc = _conv_ffn(xc, mods, context_row, *ffn_args)
    return x
```

```python
import functools

import jax
import jax.numpy as jnp
from jax import lax
from jax.experimental import pallas as pl
from jax.experimental.pallas import tpu as pltpu

F32 = jnp.float32
BF16 = jnp.bfloat16

HEAD_DIM = 64
GQA_GROUP = 4
POOL_WINDOWS = (2, 4, 8, 16)
POOL_GC = 128
WINDOW = 128
GRID_W = 64
ROPE_BASE = 10000.0
EPS = 1e-6
NEG = -1e30
N_MOD = 6

LANES = 128
SUBLANES = 8
VMEM_LIMIT_CAP = 56 * 1024 * 1024
VMEM_LIMIT_FLOOR = 32 * 1024 * 1024
VMEM_MARGIN_DIV = 4
ROW_TILE = 512
OUT_ROW_TILE = 1024
Q_TILE = 256

PAIR = 2 * HEAD_DIM
GROUP_COLS = GQA_GROUP * HEAD_DIM
HALO = SUBLANES
KEY_CHUNK = 256
BAND_CHUNK = WINDOW
LOG2E = 1.4426950408889634
Q_SCALE = HEAD_DIM ** -0.5 * LOG2E
ONES_ROWS = 16
VT_ROWS = HEAD_DIM + ONES_ROWS
FFN_CHUNK = 256


def _vmem_limit(block_bytes, scratch_bytes=0, temp_bytes=0):
    need = 2 * block_bytes + scratch_bytes + temp_bytes
    return int(min(max(need + need // VMEM_MARGIN_DIV, VMEM_LIMIT_FLOOR), VMEM_LIMIT_CAP))


def _nbytes(shape, dtype):
    n = 1
    for s in shape:
        n *= s
    return n * jnp.dtype(dtype).itemsize


def _row_tile(t, want):
    tile = min(t, want)
    assert t % tile == 0
    return tile


def _ada_kernel(c_ref, w_ref, b_ref, o_ref):
    c = c_ref[...]
    a = (c * jax.nn.sigmoid(c)).astype(BF16)
    o_ref[0] = jnp.dot(a, w_ref[0].astype(BF16), preferred_element_type=F32) + b_ref[0]


def _ada_params(cc, ada_w, ada_b):
    depth, d, n = ada_w.shape
    rows = cc.shape[0]
    tn = n // N_MOD
    blocks = _nbytes((rows, d), F32) + _nbytes((d, tn), F32) + _nbytes((rows, tn), F32)
    return pl.pallas_call(
        _ada_kernel,
        grid=(depth, n // tn),
        in_specs=[
            pl.BlockSpec((rows, d), lambda l, j: (0, 0)),
            pl.BlockSpec((1, d, tn), lambda l, j: (l, 0, j)),
            pl.BlockSpec((1, 1, tn), lambda l, j: (l, 0, j)),
        ],
        out_specs=pl.BlockSpec((1, rows, tn), lambda l, j: (l, 0, j)),
        out_shape=jax.ShapeDtypeStruct((depth, rows, n), F32),
        compiler_params=pltpu.CompilerParams(
            dimension_semantics=("parallel", "parallel"),
            vmem_limit_bytes=_vmem_limit(blocks, temp_bytes=_nbytes((d, tn), BF16))),
        name="ada_params",
    )(cc, ada_w, ada_b.reshape(depth, 1, n))


def _rms_norm(x, g):
    return x * lax.rsqrt(jnp.mean(x * x, axis=-1, keepdims=True) + EPS) * g


def _pre_mod(x, g, shift, scale):
    return _rms_norm(x, g) * (1.0 + scale) + shift


def _inproj_kernel(*refs, n_u, n_q, n_kv, n_v, key_chunk, rope, qk_norm, tm):
    x_ref, sh_ref, sc_ref, g_ref, w_ref = refs[:5]
    pos = 5
    if rope:
        cos_ref, sin_ref = refs[pos:pos + 2]
        pos += 2
    if qk_norm:
        qg_ref, kg_ref, ones_ref = refs[pos:pos + 3]
        pos += 3
    out_refs = refs[pos:]

    h = _pre_mod(x_ref[0], g_ref[...], sh_ref[0], sc_ref[0]).astype(BF16)

    if rope:
        lane = lax.broadcasted_iota(jnp.int32, (tm, PAIR), 1)
        first_half = (lane & (HEAD_DIM - 1)) < HEAD_DIM // 2
        cos = cos_ref[...]
        sin = sin_ref[...]

    def head_epilogue(a, gain_ref, scale):
        if qk_norm:
            a2 = a * a
            hi = a2.astype(BF16)
            lo = (a2 - hi.astype(F32)).astype(BF16)
            ss = jnp.dot(jnp.concatenate([hi, lo], axis=1), ones_ref[...],
                         preferred_element_type=F32)
            a = a * lax.rsqrt(ss * (1.0 / HEAD_DIM) + EPS) * gain_ref[...]
        if rope:
            partner = jnp.where(first_half,
                                pltpu.roll(a, PAIR - HEAD_DIM // 2, 1),
                                pltpu.roll(a, HEAD_DIM // 2, 1))
            a = a * cos + partner * sin
        if scale != 1.0:
            a = a * scale
        return a.astype(BF16)

    col = 0
    out_idx = 0
    if n_u:
        out_refs[out_idx][0] = jnp.dot(h, w_ref[:, col:col + n_u], preferred_element_type=F32)
        col += n_u
        out_idx += 1

    acc = jnp.dot(h, w_ref[:, col:col + n_q], preferred_element_type=F32)
    for c in range(n_q // PAIR):
        out_refs[out_idx][0, :, c * PAIR:(c + 1) * PAIR] = head_epilogue(
            acc[:, c * PAIR:(c + 1) * PAIR], qg_ref if qk_norm else None, Q_SCALE)
    col += n_q
    out_idx += 1

    acc = jnp.dot(h, w_ref[:, col:col + n_kv], preferred_element_type=F32)
    for c in range(n_kv // PAIR):
        out_refs[out_idx][0, :, c * PAIR:(c + 1) * PAIR] = head_epilogue(
            acc[:, c * PAIR:(c + 1) * PAIR], kg_ref if qk_norm else None, 1.0)
    col += n_kv
    out_idx += 1

    acc = jnp.dot(h, w_ref[:, col:col + n_v], preferred_element_type=F32)
    ones = jnp.ones((ONES_ROWS, key_chunk), F32)
    for cc in range(tm // key_chunk):
        at = acc[cc * key_chunk:(cc + 1) * key_chunk, :].T
        pieces = []
        for j in range(n_v // HEAD_DIM):
            pieces += [at[j * HEAD_DIM:(j + 1) * HEAD_DIM], ones]
        out_refs[out_idx][0, cc] = jnp.concatenate(pieces, axis=0).astype(BF16)


def _inproj(x, mods, mod_row, pre_g, w, *, n_u, n_q, n_kv, n_v, key_chunk=KEY_CHUNK,
            rope_tabs=None, qk_gains=None):
    b, t, d = x.shape
    n = w.shape[1]
    tm = _row_tile(t, ROW_TILE)
    rope = rope_tabs is not None
    qk_norm = qk_gains is not None
    vt_rows = n_v // HEAD_DIM * VT_ROWS
    assert tm % key_chunk == 0

    in_specs = [
        pl.BlockSpec((1, tm, d), lambda bi, i: (bi, i, 0)),
        pl.BlockSpec((1, 1, d), lambda bi, i: (mod_row(bi), 0, 0)),
        pl.BlockSpec((1, 1, d), lambda bi, i: (mod_row(bi), 0, 1)),
        pl.BlockSpec((1, d), lambda bi, i: (0, 0)),
        pl.BlockSpec((d, n), lambda bi, i: (0, 0)),
    ]
    args = [x, mods, mods, pre_g, w]
    if rope:
        in_specs += [pl.BlockSpec((tm, PAIR), lambda bi, i: (i, 0))] * 2
        args += list(rope_tabs)
    if qk_norm:
        in_specs += [pl.BlockSpec((1, PAIR), lambda bi, i: (0, 0))] * 2
        in_specs += [pl.BlockSpec((2 * PAIR, PAIR), lambda bi, i: (0, 0))]
        args += list(qk_gains)

    out_shape, out_specs = [], []
    for width, dtype in ((n_u, F32), (n_q, BF16), (n_kv, BF16)):
        if width:
            out_shape.append(jax.ShapeDtypeStruct((b, t, width), dtype))
            out_specs.append(pl.BlockSpec((1, tm, width), lambda bi, i: (bi, i, 0)))
    out_shape.append(jax.ShapeDtypeStruct((b, t // key_chunk, vt_rows, key_chunk), BF16))
    out_specs.append(pl.BlockSpec((1, tm // key_chunk, vt_rows, key_chunk),
                                  lambda bi, i: (bi, i, 0, 0)))

    blocks = (_nbytes((tm, d), F32) + _nbytes((d, n), BF16) + _nbytes((tm, n_u), F32)
              + _nbytes((tm, n_q + n_kv + 2 * n_v), BF16) + 2 * _nbytes((tm, PAIR), F32))
    temps = _nbytes((tm, d), F32) * 2 + _nbytes((tm, max(n_q, n_kv, n_u)), F32) * 2
    return pl.pallas_call(
        functools.partial(_inproj_kernel, n_u=n_u, n_q=n_q, n_kv=n_kv, n_v=n_v,
                          key_chunk=key_chunk, rope=rope, qk_norm=qk_norm, tm=tm),
        grid=(b, t // tm),
        in_specs=in_specs,
        out_specs=out_specs,
        out_shape=out_shape,
        compiler_params=pltpu.CompilerParams(
            dimension_semantics=("parallel", "parallel"),
            vmem_limit_bytes=_vmem_limit(blocks, temp_bytes=temps)),
        name="mixer_inproj",
    )(*args)


def _stack_heads_transposed(q_ref, qt_ref, tq, rows=slice(None)):
    sub = lax.broadcasted_iota(jnp.int32, (PAIR, tq), 0)
    top = sub < HEAD_DIM
    for p in range(GROUP_COLS // PAIR):
        pair_t = q_ref[0, rows, p * PAIR:(p + 1) * PAIR].astype(F32).T
        qt_ref[:, (2 * p) * tq:(2 * p + 1) * tq] = jnp.where(top, pair_t, 0.0).astype(BF16)
        qt_ref[:, (2 * p + 1) * tq:(2 * p + 2) * tq] = jnp.where(top, 0.0, pair_t).astype(BF16)


def _store_heads(o_t, o_ref, tq, rows=slice(None)):
    for p in range(GROUP_COLS // PAIR):
        pair_t = jnp.concatenate([o_t[:, (2 * p) * tq:(2 * p + 1) * tq],
                                  o_t[:, (2 * p + 1) * tq:(2 * p + 2) * tq]], axis=0)
        o_ref[0, rows, p * PAIR:(p + 1) * PAIR] = pair_t.T.astype(o_ref.dtype)


def _flash_kernel(*refs, tq, n_tiles, n_chunks, has_sink):
    if has_sink:
        q_ref, k_ref, vt_ref, sink_ref, o_ref, qt_ref, s_ref, m_ref, acc_ref = refs
    else:
        q_ref, k_ref, vt_ref, o_ref, qt_ref, s_ref, m_ref, acc_ref = refs
    n_pairs = n_chunks // 2

    def tile_rows(i):
        return pl.ds(pl.multiple_of(i * tq, tq), tq)

    def scores(c):
        start = pl.multiple_of(c * KEY_CHUNK, KEY_CHUNK)
        return jnp.dot(k_ref[0, pl.ds(start, KEY_CHUNK), :], qt_ref[...],
                       preferred_element_type=F32)

    def begin_tile(i, slot):
        _stack_heads_transposed(q_ref, qt_ref, tq, tile_rows(i))
        s_ref[slot] = scores(0)

    def consume(slot, c):
        s = s_ref[slot]
        m_prev = m_ref[...]
        m_new = jnp.maximum(m_prev, jnp.max(s, axis=0, keepdims=True))
        alpha = jnp.exp2(m_prev - m_new)
        p = jnp.exp2(s - m_new)
        acc_ref[...] = alpha * acc_ref[...] + jnp.dot(vt_ref[0, c], p.astype(BF16),
                                                     preferred_element_type=F32)
        m_ref[...] = m_new

    def run_tile(i, x, next_tile):
        y = 1 - x
        if has_sink:
            m_ref[...] = sink_ref[0]
            den_rows = lax.broadcasted_iota(jnp.int32, acc_ref.shape, 0) >= HEAD_DIM
            acc_ref[...] = jnp.where(den_rows, 1.0, 0.0)
        else:
            m_ref[...] = jnp.full_like(m_ref, NEG)
            acc_ref[...] = jnp.zeros_like(acc_ref)

        def body(pair, carry):
            c = 2 * pair
            s_ref[y] = scores(c + 1)
            consume(x, c)
            s_ref[x] = scores(c + 2)
            consume(y, c + 1)
            return carry
        lax.fori_loop(0, n_pairs, body, 0, unroll=4 if n_pairs % 4 == 0 else 1)
        if next_tile is not None:
            begin_tile(next_tile, y)
        consume(x, n_chunks - 1)
        _store_heads(acc_ref[0:HEAD_DIM, :] / acc_ref[HEAD_DIM:HEAD_DIM + 1, :], o_ref, tq,
                     tile_rows(i))

    begin_tile(0, 0)
    if n_tiles == 1:
        run_tile(0, 0, None)
    else:
        def tile_pair(k, carry):
            i = 2 * k
            run_tile(i, 0, i + 1)
            run_tile(i + 1, 1, jnp.minimum(i + 2, n_tiles - 1))
            return carry
        lax.fori_loop(0, n_tiles // 2, tile_pair, 0)


def _flash_attention(q, k2, vt, sink_row=None, *, tq_want=Q_TILE):
    b, t, hq = q.shape
    n_chunks = vt.shape[1]
    n_groups = hq // GROUP_COLS
    tq = _row_tile(t, tq_want)
    n_tiles = t // tq
    cols = GQA_GROUP * tq
    has_sink = sink_row is not None
    assert n_chunks % 2 == 1 and k2.shape[1] == n_chunks * KEY_CHUNK
    assert n_tiles == 1 or n_tiles % 2 == 0

    in_specs = [
        pl.BlockSpec((1, t, GROUP_COLS), lambda bi, j: (bi, 0, j)),
        pl.BlockSpec((1, n_chunks * KEY_CHUNK, PAIR), lambda bi, j: (bi, 0, j)),
        pl.BlockSpec((1, n_chunks, VT_ROWS, KEY_CHUNK), lambda bi, j: (bi, 0, j, 0)),
    ]
    args = [q, k2, vt]
    if has_sink:
        in_specs.append(pl.BlockSpec((1, 1, cols), lambda bi, j: (j, 0, 0)))
        args.append(sink_row)
    blocks = (2 * _nbytes((t, GROUP_COLS), BF16)
              + _nbytes((n_chunks * KEY_CHUNK, PAIR + VT_ROWS), BF16))
    scratch = (_nbytes((PAIR, cols), BF16) + _nbytes((2, KEY_CHUNK, cols), F32)
               + _nbytes((VT_ROWS + SUBLANES, cols), F32))
    temps = 2 * _nbytes((KEY_CHUNK, cols), F32)
    return pl.pallas_call(
        functools.partial(_flash_kernel, tq=tq, n_tiles=n_tiles, n_chunks=n_chunks,
                          has_sink=has_sink),
        grid=(b, n_groups),
        in_specs=in_specs,
        out_specs=pl.BlockSpec((1, t, GROUP_COLS), lambda bi, j: (bi, 0, j)),
        out_shape=jax.ShapeDtypeStruct((b, t, hq), BF16),
        scratch_shapes=[
            pltpu.VMEM((PAIR, cols), BF16),
            pltpu.VMEM((2, KEY_CHUNK, cols), F32),
            pltpu.VMEM((1, cols), F32),
            pltpu.VMEM((VT_ROWS, cols), F32),
        ],
        compiler_params=pltpu.CompilerParams(
            dimension_semantics=("parallel", "parallel"),
            vmem_limit_bytes=_vmem_limit(blocks, scratch, temps)),
        name="flash_attention",
    )(*args)


def _window_kernel(q_ref, k_ref, vt_ref, kc_ref, vtc_ref, bias_ref, sink_ref, o_ref,
                   qt_ref, sb_ref, sc_ref, *, tq, t):
    span = tq + 2 * WINDOW
    n_tiles = t // tq

    def window_start(i):
        return pl.multiple_of(jnp.clip(i * tq - WINDOW, 0, t - span), WINDOW)

    def issue(i, slot):
        start = window_start(i)
        bias = bias_ref[(i * tq - start) // WINDOW]
        _stack_heads_transposed(q_ref, qt_ref, tq, pl.ds(pl.multiple_of(i * tq, tq), tq))
        qt = qt_ref[...]
        s_band = jnp.dot(k_ref[0, pl.ds(start, span), :], qt, preferred_element_type=F32)
        sb_ref[slot] = s_band + jnp.concatenate([bias] * GQA_GROUP, axis=1)
        sc_ref[slot] = jnp.dot(kc_ref[0], qt, preferred_element_type=F32)

    def finish(i, slot):
        s_band, s_ctx = sb_ref[slot], sc_ref[slot]
        sink = sink_ref[0]
        m = jnp.maximum(jnp.maximum(jnp.max(s_band, axis=0, keepdims=True),
                                    jnp.max(s_ctx, axis=0, keepdims=True)), sink)
        p_band = jnp.exp2(s_band - m).astype(BF16)
        p_ctx = jnp.exp2(s_ctx - m).astype(BF16)
        first = window_start(i) // BAND_CHUNK
        vt_band = jnp.concatenate([vt_ref[0, first + c] for c in range(span // BAND_CHUNK)],
                                  axis=1)
        acc = (jnp.dot(vt_band, p_band, preferred_element_type=F32)
               + jnp.dot(vtc_ref[0, 0], p_ctx, preferred_element_type=F32))
        den = acc[HEAD_DIM:HEAD_DIM + 1, :] + jnp.exp2(sink - m)
        _store_heads(acc[0:HEAD_DIM, :] / den, o_ref, tq,
                     pl.ds(pl.multiple_of(i * tq, tq), tq))

    issue(0, 0)

    def body(pair, carry):
        i = 2 * pair
        issue(i + 1, 1)
        finish(i, 0)
        issue(i + 2, 0)
        finish(i + 1, 1)
        return carry
    lax.fori_loop(0, n_tiles // 2 - 1, body, 0)
    issue(n_tiles - 1, 1)
    finish(n_tiles - 2, 0)
    finish(n_tiles - 1, 1)


def _band_bias(tq):
    span = tq + 2 * WINDOW
    key = jnp.arange(span)[None, :, None]
    qry = jnp.arange(tq)[None, None, :]
    back = (jnp.arange(3) * WINDOW)[:, None, None]
    delta = qry + back - key
    return jnp.where(jnp.abs(delta) <= WINDOW, 0.0, NEG).astype(F32)


def _window_attention(q, k2, vt, kc2, vtc, sink_row, *, tq):
    b, t, hq = q.shape
    n_ctx = kc2.shape[1]
    n_groups = hq // GROUP_COLS
    cols = GQA_GROUP * tq
    span = tq + 2 * WINDOW
    n_band = t // BAND_CHUNK
    assert t % (2 * tq) == 0 and t >= span and tq % WINDOW == 0 and vtc.shape[1] == 1
    blocks = (2 * _nbytes((t, GROUP_COLS), BF16) + _nbytes((t, PAIR + VT_ROWS), BF16)
              + _nbytes((n_ctx, PAIR + VT_ROWS), BF16) + _nbytes((3, span, tq), F32))
    scratch = _nbytes((PAIR, cols), BF16) + 2 * _nbytes((span + n_ctx, cols), F32)
    temps = 2 * _nbytes((span + n_ctx, cols), F32)
    return pl.pallas_call(
        functools.partial(_window_kernel, tq=tq, t=t),
        grid=(b, n_groups),
        in_specs=[
            pl.BlockSpec((1, t, GROUP_COLS), lambda bi, j: (bi, 0, j)),
            pl.BlockSpec((1, t, PAIR), lambda bi, j: (bi, 0, j)),
            pl.BlockSpec((1, n_band, VT_ROWS, BAND_CHUNK), lambda bi, j: (bi, 0, j, 0)),
            pl.BlockSpec((1, n_ctx, PAIR), lambda bi, j: (bi, 0, j)),
            pl.BlockSpec((1, 1, VT_ROWS, n_ctx), lambda bi, j: (bi, 0, j, 0)),
            pl.BlockSpec((3, span, tq), lambda bi, j: (0, 0, 0)),
            pl.BlockSpec((1, 1, cols), lambda bi, j: (j, 0, 0)),
        ],
        out_specs=pl.BlockSpec((1, t, GROUP_COLS), lambda bi, j: (bi, 0, j)),
        out_shape=jax.ShapeDtypeStruct((b, t, hq), BF16),
        scratch_shapes=[
            pltpu.VMEM((PAIR, cols), BF16),
            pltpu.VMEM((2, span, cols), F32),
            pltpu.VMEM((2, n_ctx, cols), F32),
        ],
        compiler_params=pltpu.CompilerParams(
            dimension_semantics=("parallel", "parallel"),
            vmem_limit_bytes=_vmem_limit(blocks, scratch, temps)),
        name="window_attention",
    )(q, k2, vt, kc2, vtc, _band_bias(tq), sink_row)


def _pool_kernel(u_ref, w_ref, scale_ref, o_ref, pad_ref, *, t):
    reach = max(POOL_WINDOWS) // 2
    assert reach <= HALO
    pos = lax.broadcasted_iota(jnp.int32, (t, POOL_GC), 0)
    zeros = jnp.zeros((HALO, POOL_GC), F32)
    pad_ref[0:HALO, :] = zeros
    pad_ref[HALO + t:2 * HALO + t, :] = zeros
    for g, win in enumerate(POOL_WINDOWS):
        half = win // 2
        u = u_ref[0, :, g * POOL_GC:(g + 1) * POOL_GC]
        pad_ref[HALO:HALO + t, :] = u
        total = pad_ref[pl.ds(HALO - half, t), :]
        for off in range(1 - half, half):
            total = total + pad_ref[pl.ds(HALO + off, t), :]
        cnt = (jnp.minimum(pos + half, t) - jnp.maximum(pos - half, 0)).astype(F32)
        d = (total / cnt - u).astype(BF16)
        y = jnp.dot(d, w_ref[g], preferred_element_type=F32)
        o_ref[0, :, g * POOL_GC:(g + 1) * POOL_GC] = (
            y * scale_ref[:, g * POOL_GC:(g + 1) * POOL_GC]).astype(BF16)


def _pool_mix(u, pool_w, pool_scale):
    b, t, ch = u.shape
    n_g = len(POOL_WINDOWS)
    assert ch == n_g * POOL_GC
    blocks = _nbytes((t, ch), F32) + _nbytes((t, ch), BF16) + _nbytes((n_g, POOL_GC, POOL_GC), BF16)
    scratch = _nbytes((t + 2 * HALO, POOL_GC), F32)
    return pl.pallas_call(
        functools.partial(_pool_kernel, t=t),
        grid=(b,),
        in_specs=[
            pl.BlockSpec((1, t, ch), lambda bi: (bi, 0, 0)),
            pl.BlockSpec((n_g, POOL_GC, POOL_GC), lambda bi: (0, 0, 0)),
            pl.BlockSpec((1, ch), lambda bi: (0, 0)),
        ],
        out_specs=pl.BlockSpec((1, t, ch), lambda bi: (bi, 0, 0)),
        out_shape=jax.ShapeDtypeStruct((b, t, ch), BF16),
        scratch_shapes=[pltpu.VMEM((t + 2 * HALO, POOL_GC), F32)],
        compiler_params=pltpu.CompilerParams(
            dimension_semantics=("parallel",),
            vmem_limit_bytes=_vmem_limit(blocks, scratch, 6 * _nbytes((t, POOL_GC), F32))),
        name="pool_mix",
    )(u, pool_w, pool_scale)


def _outproj_kernel(*refs, n_in):
    a_refs = refs[:n_in]
    w_refs = refs[n_in:2 * n_in]
    x_ref, gate_ref, g_ref, o_ref = refs[2 * n_in:]
    y = jnp.dot(a_refs[0][0], w_refs[0][...], preferred_element_type=F32)
    for a_ref, w_ref in zip(a_refs[1:], w_refs[1:]):
        y = y + jnp.dot(a_ref[0], w_ref[...], preferred_element_type=F32)
    o_ref[0] = x_ref[0] + gate_ref[0] * _rms_norm(y, g_ref[...])


def _outproj(acts, weights, x, mods, mod_row, post_g):
    b, t, d = x.shape
    tm = _row_tile(t, OUT_ROW_TILE)
    n_in = len(acts)
    in_specs = [pl.BlockSpec((1, tm, a.shape[2]), lambda bi, i: (bi, i, 0)) for a in acts]
    in_specs += [pl.BlockSpec(w.shape, lambda bi, i: (0, 0)) for w in weights]
    in_specs += [
        pl.BlockSpec((1, tm, d), lambda bi, i: (bi, i, 0)),
        pl.BlockSpec((1, 1, d), lambda bi, i: (mod_row(bi), 0, 2)),
        pl.BlockSpec((1, d), lambda bi, i: (0, 0)),
    ]
    blocks = (sum(_nbytes((tm, a.shape[2]), BF16) for a in acts)
              + sum(_nbytes(w.shape, BF16) for w in weights) + 2 * _nbytes((tm, d), F32))
    return pl.pallas_call(
        functools.partial(_outproj_kernel, n_in=n_in),
        grid=(b, t // tm),
        in_specs=in_specs,
        out_specs=pl.BlockSpec((1, tm, d), lambda bi, i: (bi, i, 0)),
        out_shape=jax.ShapeDtypeStruct((b, t, d), F32),
        compiler_params=pltpu.CompilerParams(
            dimension_semantics=("parallel", "parallel"),
            vmem_limit_bytes=_vmem_limit(blocks, temp_bytes=3 * _nbytes((tm, d), F32))),
        name="mixer_outproj",
    )(*acts, *weights, x, mods, post_g)


def _ffn_kernel(x_ref, xp_ref, xn_ref, sh_ref, sc_ref, gate_ref, pre_g_ref, post_g_ref,
                wu_ref, wut_ref, cw_ref, cwt_ref, cb_ref, cbt_ref, wd_ref, wdt_ref, o_ref,
                h_ref, acc_ref, u_ref, ut_ref, *, tm, n_main):
    i = pl.program_id(1)
    g, sh, sc = pre_g_ref[...], sh_ref[0], sc_ref[0]
    h_prev = _pre_mod(xp_ref[0], g, sh, sc)
    h_next = _pre_mod(xn_ref[0], g, sh, sc)
    h_prev = jnp.where(i == 0, 0.0, h_prev)
    h_next = jnp.where(i == pl.num_programs(1) - 1, 0.0, h_next)
    h_ref[0:HALO, :] = h_prev.astype(BF16)
    h_ref[HALO:HALO + tm, :] = _pre_mod(x_ref[0], g, sh, sc).astype(BF16)
    h_ref[HALO + tm:2 * HALO + tm, :] = h_next.astype(BF16)

    def up(c):
        return jnp.dot(h_ref[...], wu_ref[c], preferred_element_type=F32)

    def conv_down(u, cw, cb, wd):
        w = wd.shape[0]
        uv = u[...]
        n_rows = uv.shape[0]
        conv = (pltpu.roll(uv, 1, 0)[HALO:HALO + tm] * cw[0:1, :]
                + uv[HALO:HALO + tm] * cw[1:2, :]
                + pltpu.roll(uv, n_rows - 1, 0)[HALO:HALO + tm] * cw[2:3, :]
                + cb)
        gated = jax.nn.gelu(conv[:, :w]) * conv[:, w:]
        return jnp.dot(gated.astype(BF16), wd, preferred_element_type=F32)

    def consume(slot, c):
        return conv_down(u_ref.at[slot], cw_ref[c], cb_ref[c], wd_ref[c])

    u_ref[0] = up(0)
    acc_ref[...] = jnp.zeros_like(acc_ref)

    def body(pair, carry):
        c = 2 * pair
        u_ref[1] = up(c + 1)
        acc_ref[...] += consume(0, c)
        u_ref[0] = up(c + 2)
        acc_ref[...] += consume(1, c + 1)
        return carry
    lax.fori_loop(0, (n_main - 1) // 2, body, 0, unroll=True)
    last = n_main - 1
    if n_main % 2 == 0:
        u_ref[1] = up(last)
        acc_ref[...] += consume(0, last - 1)
    ut_ref[...] = jnp.dot(h_ref[...], wut_ref[...], preferred_element_type=F32)
    acc_ref[...] += consume(last % 2, last)
    y = acc_ref[...] + conv_down(ut_ref, cwt_ref[...], cbt_ref[...], wdt_ref[...])
    o_ref[0] = x_ref[0] + gate_ref[0] * _rms_norm(y, post_g_ref[...])


def _ffn_weights(w_up, conv_w, conv_b, w_down):
    d_ff = w_down.shape[0]
    n_main = d_ff // FFN_CHUNK
    split = n_main * FFN_CHUNK

    def pair_cols(a):
        r = a.shape[0]
        gelu, gate = a[:, :d_ff], a[:, d_ff:]
        main = jnp.concatenate([gelu[:, :split].reshape(r, n_main, FFN_CHUNK),
                                gate[:, :split].reshape(r, n_main, FFN_CHUNK)], axis=2)
        tail = jnp.concatenate([gelu[:, split:], gate[:, split:]], axis=1)
        return jnp.swapaxes(main, 0, 1), tail

    wu, wut = pair_cols(w_up.astype(BF16))
    cw, cwt = pair_cols(conv_w)
    cb, cbt = pair_cols(conv_b[None])
    wd = w_down.astype(BF16)
    return (wu, wut, cw, cwt, cb, cbt, wd[:split].reshape(n_main, FFN_CHUNK, -1), wd[split:])


def _conv_ffn(x, mods, mod_row, pre_g, post_g, weights, *, tm_want=ROW_TILE):
    b, t, d = x.shape
    tm = _row_tile(t, tm_want)
    n_main = weights[0].shape[0]
    tail = weights[-1].shape[0]
    assert tm % HALO == 0 and n_main >= 2 and tail > 0
    halo_blocks = tm // HALO
    last_halo = t // HALO - 1
    rows = tm + 2 * HALO

    def resident(a):
        zeros = (0,) * a.ndim
        return pl.BlockSpec(a.shape, lambda bi, i: zeros, pipeline_mode=pl.Buffered(1))

    in_specs = [
        pl.BlockSpec((1, tm, d), lambda bi, i: (bi, i, 0)),
        pl.BlockSpec((1, HALO, d), lambda bi, i: (bi, jnp.maximum(i * halo_blocks - 1, 0), 0)),
        pl.BlockSpec((1, HALO, d),
                     lambda bi, i: (bi, jnp.minimum((i + 1) * halo_blocks, last_halo), 0)),
        pl.BlockSpec((1, 1, d), lambda bi, i: (mod_row(bi), 0, 3)),
        pl.BlockSpec((1, 1, d), lambda bi, i: (mod_row(bi), 0, 4)),
        pl.BlockSpec((1, 1, d), lambda bi, i: (mod_row(bi), 0, 5)),
        pl.BlockSpec((1, d), lambda bi, i: (0, 0)),
        pl.BlockSpec((1, d), lambda bi, i: (0, 0)),
    ] + [resident(a) for a in weights]
    weight_bytes = sum(_nbytes(a.shape, a.dtype) for a in weights)
    blocks = 2 * _nbytes((tm, d), F32) + 2 * _nbytes((HALO, d), F32)
    scratch = (_nbytes((rows, d), BF16) + _nbytes((tm, d), F32)
               + _nbytes((2, rows, 2 * FFN_CHUNK), F32) + _nbytes((rows, 2 * tail), F32)
               + weight_bytes)
    temps = 4 * _nbytes((tm, 2 * FFN_CHUNK), F32) + _nbytes((tm, d), F32)
    return pl.pallas_call(
        functools.partial(_ffn_kernel, tm=tm, n_main=n_main),
        grid=(b, t // tm),
        in_specs=in_specs,
        out_specs=pl.BlockSpec((1, tm, d), lambda bi, i: (bi, i, 0)),
        out_shape=jax.ShapeDtypeStruct((b, t, d), F32),
        scratch_shapes=[
            pltpu.VMEM((rows, d), BF16),
            pltpu.VMEM((tm, d), F32),
            pltpu.VMEM((2, rows, 2 * FFN_CHUNK), F32),
            pltpu.VMEM((rows, 2 * tail), F32),
        ],
        compiler_params=pltpu.CompilerParams(
            dimension_semantics=("parallel", "parallel"),
            vmem_limit_bytes=_vmem_limit(blocks, scratch, temps)),
        name="conv_ffn",
    )(x, x, x, mods, mods, mods, pre_g, post_g, *weights)


def _rope_tables(t):
    rows = t // GRID_W
    row = jnp.repeat(jnp.arange(rows), GRID_W).astype(F32)
    col = jnp.tile(jnp.arange(GRID_W), rows).astype(F32)
    n_freq = HEAD_DIM // 4
    freqs = ROPE_BASE ** (-jnp.arange(n_freq, dtype=F32) / n_freq)
    ang = jnp.concatenate([row[:, None] * freqs, col[:, None] * freqs], axis=-1)
    cos, sin = jnp.cos(ang), jnp.sin(ang)
    return (jnp.concatenate([cos, cos, cos, cos], axis=-1),
            jnp.concatenate([-sin, sin, -sin, sin], axis=-1))


def _dup_heads(w, n_heads):
    d = w.shape[0]
    w = w.reshape(d, n_heads, 1, HEAD_DIM)
    return jnp.broadcast_to(w, (d, n_heads, 2, HEAD_DIM)).reshape(d, n_heads * PAIR)


def _sink_row(sink, tq):
    return jnp.repeat(sink.astype(F32) * LOG2E, tq, axis=1)[:, None, :]


def kernel(x, c, ctx, c_ctx, ada_w, ada_b, mix_pre_g, mix_post_g, ffn_pre_g, ffn_post_g,
           ab_w_in, ab_w_out, pool_w, pool_scale, sink_logit,
           c_w_qkv, c_w_out, c_q_g, c_k_g,
           ffn_w_up, ffn_conv_w, ffn_conv_b, ffn_w_down):
    batch, t, d = x.shape
    n_ctx = ctx.shape[1]
    depth = ada_w.shape[0]
    pool_ch = pool_scale.shape[1]
    b_kv = sink_logit.shape[1]
    b_q = b_kv * GQA_GROUP * HEAD_DIM
    c_q = c_w_out.shape[1]
    c_kv = c_q // HEAD_DIM // GQA_GROUP
    d_ff = ffn_w_down.shape[1]

    ctx_row = batch
    n_rows = -(-(batch + 1) // SUBLANES) * SUBLANES
    cc = jnp.zeros((n_rows, d), F32).at[:batch].set(c).at[ctx_row].set(c_ctx)
    mods_all = _ada_params(cc, ada_w, ada_b).reshape(depth, n_rows, 1, N_MOD * d)

    rope_tabs = _rope_tables(t)
    ones_bd = jnp.kron(jnp.ones((2, 1), F32),
                       jnp.kron(jnp.eye(2, dtype=F32), jnp.ones((HEAD_DIM, HEAD_DIM), F32))
                       ).astype(BF16)
    latent_row = lambda bi: bi
    context_row = lambda bi: ctx_row
    win_tq = _row_tile(t, Q_TILE)
    ctx_tq = _row_tile(n_ctx, Q_TILE)

    def qkv_weights(w, n_lead, n_kv_heads):
        k_end = n_lead + n_kv_heads * HEAD_DIM
        return jnp.concatenate([w[:, :n_lead], _dup_heads(w[:, n_lead:k_end], n_kv_heads),
                                w[:, k_end:]], axis=1).astype(BF16)

    xc = ctx
    for l in range(depth):
        need_ctx = l < depth - 1
        i = l // 2
        mods = mods_all[l]
        pre_g, post_g = mix_pre_g[l][None], mix_post_g[l][None]
        if l % 2 == 0:
            w_in = qkv_weights(ab_w_in[i], pool_ch + b_q, b_kv)
            widths = dict(n_u=pool_ch, n_q=b_q, n_kv=b_kv * PAIR, n_v=b_kv * HEAD_DIM)
            u, q, k2, vt = _inproj(x, mods, latent_row, pre_g, w_in, rope_tabs=rope_tabs,
                                   key_chunk=BAND_CHUNK, **widths)
            uc, qc, kc2, vtc = _inproj(xc, mods, context_row, pre_g, w_in, **widths)
            w_out = ab_w_out[i].astype(BF16)
            w_outs = [w_out[:pool_ch], w_out[pool_ch:]]
            pw, ps = pool_w[i].astype(BF16), pool_scale[i][None]
            a = _window_attention(q, k2, vt, kc2, vtc, _sink_row(sink_logit[i], win_tq),
                                  tq=win_tq)
            x = _outproj([_pool_mix(u, pw, ps), a], w_outs, x, mods, latent_row, post_g)
            if need_ctx:
                ac = _flash_attention(qc, kc2, vtc, _sink_row(sink_logit[i], ctx_tq),
                                      tq_want=ctx_tq)
                xc = _outproj([_pool_mix(uc, pw, ps), ac], w_outs, xc, mods, context_row, post_g)
        else:
            w_in = qkv_weights(c_w_qkv[i], c_q, c_kv)
            gains = (jnp.tile(c_q_g[i], 2)[None], jnp.tile(c_k_g[i], 2)[None], ones_bd)
            widths = dict(n_u=0, n_q=c_q, n_kv=c_kv * PAIR, n_v=c_kv * HEAD_DIM)
            q, k2, vt = _inproj(x, mods, latent_row, pre_g, w_in, rope_tabs=rope_tabs,
                                qk_gains=gains, **widths)
            qc, kc2, vtc = _inproj(xc, mods, context_row, pre_g, w_in, qk_gains=gains, **widths)
            w_outs = [c_w_out[i].astype(BF16)]
            a = _flash_attention(q, jnp.concatenate([k2, kc2], axis=1),
                                 jnp.concatenate([vt, vtc], axis=1))
            x = _outproj([a], w_outs, x, mods, latent_row, post_g)
            if need_ctx:
                ac = _flash_attention(qc, kc2, vtc)
                xc = _outproj([ac], w_outs, xc, mods, context_row, post_g)

        ffn_args = (ffn_pre_g[l][None], ffn_post_g[l][None],
                    _ffn_weights(ffn_w_up[l], ffn_conv_w[l], ffn_conv_b[l], ffn_w_down[l]))
        x = _conv_ffn(x, mods, latent_row, *ffn_args)
        if need_ctx:
            xc = _conv_ffn(xc, mods, context_row, *ffn_args)
    return x
```

```python
import functools

import jax
import jax.numpy as jnp
from jax import lax
from jax.experimental import pallas as pl
from jax.experimental.pallas import tpu as pltpu

F32 = jnp.float32
BF16 = jnp.bfloat16

HEAD_DIM = 64
GQA_GROUP = 4
POOL_WINDOWS = (2, 4, 8, 16)
POOL_GC = 128
WINDOW = 128
GRID_W = 64
ROPE_BASE = 10000.0
EPS = 1e-6
NEG = -1e30
N_MOD = 6

LANES = 128
SUBLANES = 8
VMEM_LIMIT_CAP = 56 * 1024 * 1024
VMEM_LIMIT_FLOOR = 32 * 1024 * 1024
VMEM_MARGIN_DIV = 4
ROW_TILE = 512
OUT_ROW_TILE = 1024
Q_TILE = 256

PAIR = 2 * HEAD_DIM
GROUP_COLS = GQA_GROUP * HEAD_DIM
HALO = SUBLANES
KEY_CHUNK = 256
FLASH_CHUNK = 512
BAND_CHUNK = WINDOW
LOG2E = 1.4426950408889634
Q_SCALE = HEAD_DIM ** -0.5 * LOG2E
ONES_ROWS = 16
VT_ROWS = HEAD_DIM + ONES_ROWS
FFN_CHUNK = 256


def _vmem_limit(block_bytes, scratch_bytes=0, temp_bytes=0):
    need = 2 * block_bytes + scratch_bytes + temp_bytes
    return int(min(max(need + need // VMEM_MARGIN_DIV, VMEM_LIMIT_FLOOR), VMEM_LIMIT_CAP))


def _nbytes(shape, dtype):
    n = 1
    for s in shape:
        n *= s
    return n * jnp.dtype(dtype).itemsize


def _row_tile(t, want):
    tile = min(t, want)
    assert t % tile == 0
    return tile


def _ada_kernel(c_ref, w_ref, b_ref, o_ref):
    c = c_ref[...]
    a = (c * jax.nn.sigmoid(c)).astype(BF16)
    o_ref[0] = jnp.dot(a, w_ref[0].astype(BF16), preferred_element_type=F32) + b_ref[0]


def _ada_params(cc, ada_w, ada_b):
    depth, d, n = ada_w.shape
    rows = cc.shape[0]
    tn = n // N_MOD
    blocks = _nbytes((rows, d), F32) + _nbytes((d, tn), F32) + _nbytes((rows, tn), F32)
    return pl.pallas_call(
        _ada_kernel,
        grid=(depth, n // tn),
        in_specs=[
            pl.BlockSpec((rows, d), lambda l, j: (0, 0)),
            pl.BlockSpec((1, d, tn), lambda l, j: (l, 0, j)),
            pl.BlockSpec((1, 1, tn), lambda l, j: (l, 0, j)),
        ],
        out_specs=pl.BlockSpec((1, rows, tn), lambda l, j: (l, 0, j)),
        out_shape=jax.ShapeDtypeStruct((depth, rows, n), F32),
        compiler_params=pltpu.CompilerParams(
            dimension_semantics=("parallel", "parallel"),
            vmem_limit_bytes=_vmem_limit(blocks, temp_bytes=_nbytes((d, tn), BF16))),
        name="ada_params",
    )(cc, ada_w, ada_b.reshape(depth, 1, n))


def _rms_norm(x, g):
    return x * lax.rsqrt(jnp.mean(x * x, axis=-1, keepdims=True) + EPS) * g


def _pre_mod(x, g, shift, scale):
    return _rms_norm(x, g) * (1.0 + scale) + shift


def _inproj_kernel(*refs, n_u, n_q, n_kv, n_v, key_chunk, rope, qk_norm, tm):
    x_ref, sh_ref, sc_ref, g_ref, w_ref = refs[:5]
    pos = 5
    if rope:
        cos_ref, sin_ref = refs[pos:pos + 2]
        pos += 2
    if qk_norm:
        qg_ref, kg_ref, ones_ref = refs[pos:pos + 3]
        pos += 3
    out_refs = refs[pos:]

    h = _pre_mod(x_ref[0], g_ref[...], sh_ref[0], sc_ref[0]).astype(BF16)

    if rope:
        lane = lax.broadcasted_iota(jnp.int32, (tm, PAIR), 1)
        first_half = (lane & (HEAD_DIM - 1)) < HEAD_DIM // 2
        cos = cos_ref[...]
        sin = sin_ref[...]

    def head_epilogue(a, gain_ref, scale):
        if qk_norm:
            a2 = a * a
            hi = a2.astype(BF16)
            lo = (a2 - hi.astype(F32)).astype(BF16)
            ss = jnp.dot(jnp.concatenate([hi, lo], axis=1), ones_ref[...],
                         preferred_element_type=F32)
            a = a * lax.rsqrt(ss * (1.0 / HEAD_DIM) + EPS) * gain_ref[...]
        if rope:
            partner = jnp.where(first_half,
                                pltpu.roll(a, PAIR - HEAD_DIM // 2, 1),
                                pltpu.roll(a, HEAD_DIM // 2, 1))
            a = a * cos + partner * sin
        if scale != 1.0:
            a = a * scale
        return a.astype(BF16)

    col = 0
    out_idx = 0
    if n_u:
        out_refs[out_idx][0] = jnp.dot(h, w_ref[:, col:col + n_u], preferred_element_type=F32)
        col += n_u
        out_idx += 1

    acc = jnp.dot(h, w_ref[:, col:col + n_q], preferred_element_type=F32)
    for c in range(n_q // PAIR):
        out_refs[out_idx][0, :, c * PAIR:(c + 1) * PAIR] = head_epilogue(
            acc[:, c * PAIR:(c + 1) * PAIR], qg_ref if qk_norm else None, Q_SCALE)
    col += n_q
    out_idx += 1

    acc = jnp.dot(h, w_ref[:, col:col + n_kv], preferred_element_type=F32)
    for c in range(n_kv // PAIR):
        out_refs[out_idx][0, :, c * PAIR:(c + 1) * PAIR] = head_epilogue(
            acc[:, c * PAIR:(c + 1) * PAIR], kg_ref if qk_norm else None, 1.0)
    col += n_kv
    out_idx += 1

    acc = jnp.dot(h, w_ref[:, col:col + n_v], preferred_element_type=F32)
    ones = jnp.ones((ONES_ROWS, key_chunk), F32)
    for cc in range(tm // key_chunk):
        at = acc[cc * key_chunk:(cc + 1) * key_chunk, :].T
        pieces = []
        for j in range(n_v // HEAD_DIM):
            pieces += [at[j * HEAD_DIM:(j + 1) * HEAD_DIM], ones]
        out_refs[out_idx][0, cc] = jnp.concatenate(pieces, axis=0).astype(BF16)


def _inproj(x, mods, mod_row, pre_g, w, *, n_u, n_q, n_kv, n_v, key_chunk=KEY_CHUNK,
            rope_tabs=None, qk_gains=None):
    b, t, d = x.shape
    n = w.shape[1]
    tm = _row_tile(t, ROW_TILE)
    rope = rope_tabs is not None
    qk_norm = qk_gains is not None
    vt_rows = n_v // HEAD_DIM * VT_ROWS
    assert tm % key_chunk == 0

    in_specs = [
        pl.BlockSpec((1, tm, d), lambda bi, i: (bi, i, 0)),
        pl.BlockSpec((1, 1, d), lambda bi, i: (mod_row(bi), 0, 0)),
        pl.BlockSpec((1, 1, d), lambda bi, i: (mod_row(bi), 0, 1)),
        pl.BlockSpec((1, d), lambda bi, i: (0, 0)),
        pl.BlockSpec((d, n), lambda bi, i: (0, 0)),
    ]
    args = [x, mods, mods, pre_g, w]
    if rope:
        in_specs += [pl.BlockSpec((tm, PAIR), lambda bi, i: (i, 0))] * 2
        args += list(rope_tabs)
    if qk_norm:
        in_specs += [pl.BlockSpec((1, PAIR), lambda bi, i: (0, 0))] * 2
        in_specs += [pl.BlockSpec((2 * PAIR, PAIR), lambda bi, i: (0, 0))]
        args += list(qk_gains)

    out_shape, out_specs = [], []
    for width, dtype in ((n_u, F32), (n_q, BF16), (n_kv, BF16)):
        if width:
            out_shape.append(jax.ShapeDtypeStruct((b, t, width), dtype))
            out_specs.append(pl.BlockSpec((1, tm, width), lambda bi, i: (bi, i, 0)))
    out_shape.append(jax.ShapeDtypeStruct((b, t // key_chunk, vt_rows, key_chunk), BF16))
    out_specs.append(pl.BlockSpec((1, tm // key_chunk, vt_rows, key_chunk),
                                  lambda bi, i: (bi, i, 0, 0)))

    blocks = (_nbytes((tm, d), F32) + _nbytes((d, n), BF16) + _nbytes((tm, n_u), F32)
              + _nbytes((tm, n_q + n_kv + 2 * n_v), BF16) + 2 * _nbytes((tm, PAIR), F32))
    temps = _nbytes((tm, d), F32) * 2 + _nbytes((tm, max(n_q, n_kv, n_u)), F32) * 2
    return pl.pallas_call(
        functools.partial(_inproj_kernel, n_u=n_u, n_q=n_q, n_kv=n_kv, n_v=n_v,
                          key_chunk=key_chunk, rope=rope, qk_norm=qk_norm, tm=tm),
        grid=(b, t // tm),
        in_specs=in_specs,
        out_specs=out_specs,
        out_shape=out_shape,
        compiler_params=pltpu.CompilerParams(
            dimension_semantics=("parallel", "parallel"),
            vmem_limit_bytes=_vmem_limit(blocks, temp_bytes=temps)),
        name="mixer_inproj",
    )(*args)


def _stack_heads_transposed(q_ref, qt_ref, tq, rows=slice(None)):
    sub = lax.broadcasted_iota(jnp.int32, (PAIR, tq), 0)
    top = sub < HEAD_DIM
    for p in range(GROUP_COLS // PAIR):
        pair_t = q_ref[0, rows, p * PAIR:(p + 1) * PAIR].astype(F32).T
        qt_ref[:, (2 * p) * tq:(2 * p + 1) * tq] = jnp.where(top, pair_t, 0.0).astype(BF16)
        qt_ref[:, (2 * p + 1) * tq:(2 * p + 2) * tq] = jnp.where(top, 0.0, pair_t).astype(BF16)


def _store_heads(o_t, o_ref, tq, rows=slice(None)):
    for p in range(GROUP_COLS // PAIR):
        pair_t = jnp.concatenate([o_t[:, (2 * p) * tq:(2 * p + 1) * tq],
                                  o_t[:, (2 * p + 1) * tq:(2 * p + 2) * tq]], axis=0)
        o_ref[0, rows, p * PAIR:(p + 1) * PAIR] = pair_t.T.astype(o_ref.dtype)


def _flash_kernel(*refs, tq, n_tiles, n_lat, chunk, has_sink):
    refs = list(refs)
    q_ref, kc_ref, vtc_ref = refs[:3]
    del refs[:3]
    if n_lat:
        k_ref, vt_ref = refs[:2]
        del refs[:2]
    if has_sink:
        sink_ref = refs.pop(0)
    o_ref, qt_ref, sc_ref = refs[:3]
    del refs[:3]
    if n_lat:
        s_ref = refs.pop(0)
    m_ref, acc_ref = refs

    def tile_rows(i):
        return pl.ds(pl.multiple_of(i * tq, tq), tq)

    def scores(c):
        start = pl.multiple_of(c * chunk, chunk)
        return jnp.dot(k_ref[0, pl.ds(start, chunk), :], qt_ref[...],
                       preferred_element_type=F32)

    def begin_tile(i):
        _stack_heads_transposed(q_ref, qt_ref, tq, tile_rows(i))
        sc_ref[...] = jnp.dot(kc_ref[0], qt_ref[...], preferred_element_type=F32)
        if n_lat:
            s_ref[0] = scores(0)

    def update(s, vt):
        m_prev = m_ref[...]
        m_new = jnp.maximum(m_prev, jnp.max(s, axis=0, keepdims=True))
        alpha = jnp.exp2(m_prev - m_new)
        p = jnp.exp2(s - m_new)
        acc_ref[...] = alpha * acc_ref[...] + jnp.dot(vt, p.astype(BF16),
                                                     preferred_element_type=F32)
        m_ref[...] = m_new

    def run_tile(i, next_tile):
        if has_sink:
            m_ref[...] = sink_ref[0]
            den_rows = lax.broadcasted_iota(jnp.int32, acc_ref.shape, 0) >= HEAD_DIM
            acc_ref[...] = jnp.where(den_rows, 1.0, 0.0)
        else:
            m_ref[...] = jnp.full_like(m_ref, NEG)
            acc_ref[...] = jnp.zeros_like(acc_ref)
        update(sc_ref[...], vtc_ref[0, 0])
        if n_lat:
            def body(pair, carry):
                c = 2 * pair
                s_ref[1] = scores(c + 1)
                update(s_ref[0], vt_ref[0, c])
                s_ref[0] = scores(c + 2)
                update(s_ref[1], vt_ref[0, c + 1])
                return carry
            lax.fori_loop(0, n_lat // 2 - 1, body, 0, unroll=True)
            s_ref[1] = scores(n_lat - 1)
            update(s_ref[0], vt_ref[0, n_lat - 2])
        if next_tile is not None:
            begin_tile(next_tile)
        if n_lat:
            update(s_ref[1], vt_ref[0, n_lat - 1])
        _store_heads(acc_ref[0:HEAD_DIM, :] / acc_ref[HEAD_DIM:HEAD_DIM + 1, :], o_ref, tq,
                     tile_rows(i))

    begin_tile(0)
    if n_tiles == 1:
        run_tile(0, None)
    else:
        def tile_body(i, carry):
            run_tile(i, jnp.minimum(i + 1, n_tiles - 1))
            return carry
        lax.fori_loop(0, n_tiles, tile_body, 0)


def _flash_chunk(t):
    if t % FLASH_CHUNK == 0 and (t // FLASH_CHUNK) % 2 == 0:
        return FLASH_CHUNK
    return FLASH_CHUNK // 2


def _flash_attention(q, kc2, vtc, k2=None, vt=None, sink_row=None, *, tq_want=Q_TILE):
    b, t, hq = q.shape
    n_ctx = kc2.shape[1]
    n_groups = hq // GROUP_COLS
    tq = _row_tile(t, tq_want)
    n_tiles = t // tq
    cols = GQA_GROUP * tq
    has_sink = sink_row is not None
    n_lat, chunk = (vt.shape[1], vt.shape[3]) if vt is not None else (0, 0)
    assert n_lat % 2 == 0 and vtc.shape[1] == 1

    in_specs = [
        pl.BlockSpec((1, t, GROUP_COLS), lambda bi, j: (bi, 0, j)),
        pl.BlockSpec((1, n_ctx, PAIR), lambda bi, j: (bi, 0, j)),
        pl.BlockSpec((1, 1, VT_ROWS, n_ctx), lambda bi, j: (bi, 0, j, 0)),
    ]
    args = [q, kc2, vtc]
    scratch_shapes = [pltpu.VMEM((PAIR, cols), BF16), pltpu.VMEM((n_ctx, cols), F32)]
    if n_lat:
        assert k2.shape[1] == n_lat * chunk
        in_specs += [
            pl.BlockSpec((1, n_lat * chunk, PAIR), lambda bi, j: (bi, 0, j)),
            pl.BlockSpec((1, n_lat, VT_ROWS, chunk), lambda bi, j: (bi, 0, j, 0)),
        ]
        args += [k2, vt]
        scratch_shapes.append(pltpu.VMEM((2, chunk, cols), F32))
    if has_sink:
        in_specs.append(pl.BlockSpec((1, 1, cols), lambda bi, j: (j, 0, 0)))
        args.append(sink_row)
    scratch_shapes += [pltpu.VMEM((1, cols), F32), pltpu.VMEM((VT_ROWS, cols), F32)]
    blocks = (2 * _nbytes((t, GROUP_COLS), BF16)
              + _nbytes((n_ctx + n_lat * chunk, PAIR + VT_ROWS), BF16))
    scratch = (_nbytes((PAIR, cols), BF16) + _nbytes((n_ctx + 2 * chunk, cols), F32)
               + _nbytes((VT_ROWS + SUBLANES, cols), F32))
    temps = 2 * _nbytes((max(chunk, n_ctx), cols), F32)
    return pl.pallas_call(
        functools.partial(_flash_kernel, tq=tq, n_tiles=n_tiles, n_lat=n_lat, chunk=chunk,
                          has_sink=has_sink),
        grid=(b, n_groups),
        in_specs=in_specs,
        out_specs=pl.BlockSpec((1, t, GROUP_COLS), lambda bi, j: (bi, 0, j)),
        out_shape=jax.ShapeDtypeStruct((b, t, hq), BF16),
        scratch_shapes=scratch_shapes,
        compiler_params=pltpu.CompilerParams(
            dimension_semantics=("parallel", "parallel"),
            vmem_limit_bytes=_vmem_limit(blocks, scratch, temps)),
        name="flash_attention",
    )(*args)


def _window_kernel(q_ref, k_ref, vt_ref, kc_ref, vtc_ref, bias_ref, sink_ref, o_ref,
                   qt_ref, sb_ref, sc_ref, *, tq, t):
    span = tq + 2 * WINDOW
    n_tiles = t // tq

    def window_start(i):
        return pl.multiple_of(jnp.clip(i * tq - WINDOW, 0, t - span), WINDOW)

    def issue(i, slot):
        start = window_start(i)
        bias = bias_ref[(i * tq - start) // WINDOW]
        _stack_heads_transposed(q_ref, qt_ref, tq, pl.ds(pl.multiple_of(i * tq, tq), tq))
        qt = qt_ref[...]
        s_band = jnp.dot(k_ref[0, pl.ds(start, span), :], qt, preferred_element_type=F32)
        sb_ref[slot] = s_band + jnp.concatenate([bias] * GQA_GROUP, axis=1)
        sc_ref[slot] = jnp.dot(kc_ref[0], qt, preferred_element_type=F32)

    def finish(i, slot):
        s_band, s_ctx = sb_ref[slot], sc_ref[slot]
        sink = sink_ref[0]
        m = jnp.maximum(jnp.maximum(jnp.max(s_band, axis=0, keepdims=True),
                                    jnp.max(s_ctx, axis=0, keepdims=True)), sink)
        p_band = jnp.exp2(s_band - m).astype(BF16)
        p_ctx = jnp.exp2(s_ctx - m).astype(BF16)
        first = window_start(i) // BAND_CHUNK
        vt_band = jnp.concatenate([vt_ref[0, first + c] for c in range(span // BAND_CHUNK)],
                                  axis=1)
        acc = (jnp.dot(vt_band, p_band, preferred_element_type=F32)
               + jnp.dot(vtc_ref[0, 0], p_ctx, preferred_element_type=F32))
        den = acc[HEAD_DIM:HEAD_DIM + 1, :] + jnp.exp2(sink - m)
        _store_heads(acc[0:HEAD_DIM, :] / den, o_ref, tq,
                     pl.ds(pl.multiple_of(i * tq, tq), tq))

    issue(0, 0)

    def body(pair, carry):
        i = 2 * pair
        issue(i + 1, 1)
        finish(i, 0)
        issue(i + 2, 0)
        finish(i + 1, 1)
        return carry
    lax.fori_loop(0, n_tiles // 2 - 1, body, 0)
    issue(n_tiles - 1, 1)
    finish(n_tiles - 2, 0)
    finish(n_tiles - 1, 1)


def _band_bias(tq):
    span = tq + 2 * WINDOW
    key = jnp.arange(span)[None, :, None]
    qry = jnp.arange(tq)[None, None, :]
    back = (jnp.arange(3) * WINDOW)[:, None, None]
    delta = qry + back - key
    return jnp.where(jnp.abs(delta) <= WINDOW, 0.0, NEG).astype(F32)


def _window_attention(q, k2, vt, kc2, vtc, sink_row, *, tq):
    b, t, hq = q.shape
    n_ctx = kc2.shape[1]
    n_groups = hq // GROUP_COLS
    cols = GQA_GROUP * tq
    span = tq + 2 * WINDOW
    n_band = t // BAND_CHUNK
    assert t % (2 * tq) == 0 and t >= span and tq % WINDOW == 0 and vtc.shape[1] == 1
    blocks = (2 * _nbytes((t, GROUP_COLS), BF16) + _nbytes((t, PAIR + VT_ROWS), BF16)
              + _nbytes((n_ctx, PAIR + VT_ROWS), BF16) + _nbytes((3, span, tq), F32))
    scratch = _nbytes((PAIR, cols), BF16) + 2 * _nbytes((span + n_ctx, cols), F32)
    temps = 2 * _nbytes((span + n_ctx, cols), F32)
    return pl.pallas_call(
        functools.partial(_window_kernel, tq=tq, t=t),
        grid=(b, n_groups),
        in_specs=[
            pl.BlockSpec((1, t, GROUP_COLS), lambda bi, j: (bi, 0, j)),
            pl.BlockSpec((1, t, PAIR), lambda bi, j: (bi, 0, j)),
            pl.BlockSpec((1, n_band, VT_ROWS, BAND_CHUNK), lambda bi, j: (bi, 0, j, 0)),
            pl.BlockSpec((1, n_ctx, PAIR), lambda bi, j: (bi, 0, j)),
            pl.BlockSpec((1, 1, VT_ROWS, n_ctx), lambda bi, j: (bi, 0, j, 0)),
            pl.BlockSpec((3, span, tq), lambda bi, j: (0, 0, 0)),
            pl.BlockSpec((1, 1, cols), lambda bi, j: (j, 0, 0)),
        ],
        out_specs=pl.BlockSpec((1, t, GROUP_COLS), lambda bi, j: (bi, 0, j)),
        out_shape=jax.ShapeDtypeStruct((b, t, hq), BF16),
        scratch_shapes=[
            pltpu.VMEM((PAIR, cols), BF16),
            pltpu.VMEM((2, span, cols), F32),
            pltpu.VMEM((2, n_ctx, cols), F32),
        ],
        compiler_params=pltpu.CompilerParams(
            dimension_semantics=("parallel", "parallel"),
            vmem_limit_bytes=_vmem_limit(blocks, scratch, temps)),
        name="window_attention",
    )(q, k2, vt, kc2, vtc, _band_bias(tq), sink_row)


def _pool_kernel(u_ref, w_ref, scale_ref, o_ref, pad_ref, *, t):
    reach = max(POOL_WINDOWS) // 2
    assert reach <= HALO
    pos = lax.broadcasted_iota(jnp.int32, (t, POOL_GC), 0)
    zeros = jnp.zeros((HALO, POOL_GC), F32)
    pad_ref[0:HALO, :] = zeros
    pad_ref[HALO + t:2 * HALO + t, :] = zeros
    for g, win in enumerate(POOL_WINDOWS):
        half = win // 2
        u = u_ref[0, :, g * POOL_GC:(g + 1) * POOL_GC]
        pad_ref[HALO:HALO + t, :] = u
        total = pad_ref[pl.ds(HALO - half, t), :]
        for off in range(1 - half, half):
            total = total + pad_ref[pl.ds(HALO + off, t), :]
        cnt = (jnp.minimum(pos + half, t) - jnp.maximum(pos - half, 0)).astype(F32)
        d = (total / cnt - u).astype(BF16)
        y = jnp.dot(d, w_ref[g], preferred_element_type=F32)
        o_ref[0, :, g * POOL_GC:(g + 1) * POOL_GC] = (
            y * scale_ref[:, g * POOL_GC:(g + 1) * POOL_GC]).astype(BF16)


def _pool_mix(u, pool_w, pool_scale):
    b, t, ch = u.shape
    n_g = len(POOL_WINDOWS)
    assert ch == n_g * POOL_GC
    blocks = _nbytes((t, ch), F32) + _nbytes((t, ch), BF16) + _nbytes((n_g, POOL_GC, POOL_GC), BF16)
    scratch = _nbytes((t + 2 * HALO, POOL_GC), F32)
    return pl.pallas_call(
        functools.partial(_pool_kernel, t=t),
        grid=(b,),
        in_specs=[
            pl.BlockSpec((1, t, ch), lambda bi: (bi, 0, 0)),
            pl.BlockSpec((n_g, POOL_GC, POOL_GC), lambda bi: (0, 0, 0)),
            pl.BlockSpec((1, ch), lambda bi: (0, 0)),
        ],
        out_specs=pl.BlockSpec((1, t, ch), lambda bi: (bi, 0, 0)),
        out_shape=jax.ShapeDtypeStruct((b, t, ch), BF16),
        scratch_shapes=[pltpu.VMEM((t + 2 * HALO, POOL_GC), F32)],
        compiler_params=pltpu.CompilerParams(
            dimension_semantics=("parallel",),
            vmem_limit_bytes=_vmem_limit(blocks, scratch, 6 * _nbytes((t, POOL_GC), F32))),
        name="pool_mix",
    )(u, pool_w, pool_scale)


def _outproj_kernel(*refs, n_in):
    a_refs = refs[:n_in]
    w_refs = refs[n_in:2 * n_in]
    x_ref, gate_ref, g_ref, o_ref = refs[2 * n_in:]
    y = jnp.dot(a_refs[0][0], w_refs[0][...], preferred_element_type=F32)
    for a_ref, w_ref in zip(a_refs[1:], w_refs[1:]):
        y = y + jnp.dot(a_ref[0], w_ref[...], preferred_element_type=F32)
    o_ref[0] = x_ref[0] + gate_ref[0] * _rms_norm(y, g_ref[...])


def _outproj(acts, weights, x, mods, mod_row, post_g):
    b, t, d = x.shape
    tm = _row_tile(t, OUT_ROW_TILE)
    n_in = len(acts)
    in_specs = [pl.BlockSpec((1, tm, a.shape[2]), lambda bi, i: (bi, i, 0)) for a in acts]
    in_specs += [pl.BlockSpec(w.shape, lambda bi, i: (0, 0)) for w in weights]
    in_specs += [
        pl.BlockSpec((1, tm, d), lambda bi, i: (bi, i, 0)),
        pl.BlockSpec((1, 1, d), lambda bi, i: (mod_row(bi), 0, 2)),
        pl.BlockSpec((1, d), lambda bi, i: (0, 0)),
    ]
    blocks = (sum(_nbytes((tm, a.shape[2]), BF16) for a in acts)
              + sum(_nbytes(w.shape, BF16) for w in weights) + 2 * _nbytes((tm, d), F32))
    return pl.pallas_call(
        functools.partial(_outproj_kernel, n_in=n_in),
        grid=(b, t // tm),
        in_specs=in_specs,
        out_specs=pl.BlockSpec((1, tm, d), lambda bi, i: (bi, i, 0)),
        out_shape=jax.ShapeDtypeStruct((b, t, d), F32),
        compiler_params=pltpu.CompilerParams(
            dimension_semantics=("parallel", "parallel"),
            vmem_limit_bytes=_vmem_limit(blocks, temp_bytes=3 * _nbytes((tm, d), F32))),
        name="mixer_outproj",
    )(*acts, *weights, x, mods, post_g)


def _ffn_kernel(x_ref, xp_ref, xn_ref, sh_ref, sc_ref, gate_ref, pre_g_ref, post_g_ref,
                wu_ref, wut_ref, cw_ref, cwt_ref, cb_ref, cbt_ref, wd_ref, wdt_ref, o_ref,
                h_ref, acc_ref, u_ref, ut_ref, *, tm, n_main):
    i = pl.program_id(1)
    g, sh, sc = pre_g_ref[...], sh_ref[0], sc_ref[0]
    h_prev = _pre_mod(xp_ref[0], g, sh, sc)
    h_next = _pre_mod(xn_ref[0], g, sh, sc)
    h_prev = jnp.where(i == 0, 0.0, h_prev)
    h_next = jnp.where(i == pl.num_programs(1) - 1, 0.0, h_next)
    h_ref[0:HALO, :] = h_prev.astype(BF16)
    h_ref[HALO:HALO + tm, :] = _pre_mod(x_ref[0], g, sh, sc).astype(BF16)
    h_ref[HALO + tm:2 * HALO + tm, :] = h_next.astype(BF16)

    def up(c):
        return jnp.dot(h_ref[...], wu_ref[c], preferred_element_type=F32)

    def conv_down(u, cw, cb, wd):
        w = wd.shape[0]
        uv = u[...]
        n_rows = uv.shape[0]
        conv = (pltpu.roll(uv, 1, 0)[HALO:HALO + tm] * cw[0:1, :]
                + uv[HALO:HALO + tm] * cw[1:2, :]
                + pltpu.roll(uv, n_rows - 1, 0)[HALO:HALO + tm] * cw[2:3, :]
                + cb)
        gated = jax.nn.gelu(conv[:, :w]) * conv[:, w:]
        return jnp.dot(gated.astype(BF16), wd, preferred_element_type=F32)

    def consume(slot, c):
        return conv_down(u_ref.at[slot], cw_ref[c], cb_ref[c], wd_ref[c])

    u_ref[0] = up(0)
    acc_ref[...] = jnp.zeros_like(acc_ref)

    def body(pair, carry):
        c = 2 * pair
        u_ref[1] = up(c + 1)
        acc_ref[...] += consume(0, c)
        u_ref[0] = up(c + 2)
        acc_ref[...] += consume(1, c + 1)
        return carry
    lax.fori_loop(0, (n_main - 1) // 2, body, 0, unroll=True)
    last = n_main - 1
    if n_main % 2 == 0:
        u_ref[1] = up(last)
        acc_ref[...] += consume(0, last - 1)
    ut_ref[...] = jnp.dot(h_ref[...], wut_ref[...], preferred_element_type=F32)
    acc_ref[...] += consume(last % 2, last)
    y = acc_ref[...] + conv_down(ut_ref, cwt_ref[...], cbt_ref[...], wdt_ref[...])
    o_ref[0] = x_ref[0] + gate_ref[0] * _rms_norm(y, post_g_ref[...])


def _ffn_weights(w_up, conv_w, conv_b, w_down):
    d_ff = w_down.shape[0]
    n_main = d_ff // FFN_CHUNK
    split = n_main * FFN_CHUNK

    def pair_cols(a):
        r = a.shape[0]
        gelu, gate = a[:, :d_ff], a[:, d_ff:]
        main = jnp.concatenate([gelu[:, :split].reshape(r, n_main, FFN_CHUNK),
                                gate[:, :split].reshape(r, n_main, FFN_CHUNK)], axis=2)
        tail = jnp.concatenate([gelu[:, split:], gate[:, split:]], axis=1)
        return jnp.swapaxes(main, 0, 1), tail

    wu, wut = pair_cols(w_up.astype(BF16))
    cw, cwt = pair_cols(conv_w)
    cb, cbt = pair_cols(conv_b[None])
    wd = w_down.astype(BF16)
    return (wu, wut, cw, cwt, cb, cbt, wd[:split].reshape(n_main, FFN_CHUNK, -1), wd[split:])


def _conv_ffn(x, mods, mod_row, pre_g, post_g, weights, *, tm_want=ROW_TILE):
    b, t, d = x.shape
    tm = _row_tile(t, tm_want)
    n_main = weights[0].shape[0]
    tail = weights[-1].shape[0]
    assert tm % HALO == 0 and n_main >= 2 and tail > 0
    halo_blocks = tm // HALO
    last_halo = t // HALO - 1
    rows = tm + 2 * HALO

    def resident(a):
        zeros = (0,) * a.ndim
        return pl.BlockSpec(a.shape, lambda bi, i: zeros, pipeline_mode=pl.Buffered(1))

    in_specs = [
        pl.BlockSpec((1, tm, d), lambda bi, i: (bi, i, 0)),
        pl.BlockSpec((1, HALO, d), lambda bi, i: (bi, jnp.maximum(i * halo_blocks - 1, 0), 0)),
        pl.BlockSpec((1, HALO, d),
                     lambda bi, i: (bi, jnp.minimum((i + 1) * halo_blocks, last_halo), 0)),
        pl.BlockSpec((1, 1, d), lambda bi, i: (mod_row(bi), 0, 3)),
        pl.BlockSpec((1, 1, d), lambda bi, i: (mod_row(bi), 0, 4)),
        pl.BlockSpec((1, 1, d), lambda bi, i: (mod_row(bi), 0, 5)),
        pl.BlockSpec((1, d), lambda bi, i: (0, 0)),
        pl.BlockSpec((1, d), lambda bi, i: (0, 0)),
    ] + [resident(a) for a in weights]
    weight_bytes = sum(_nbytes(a.shape, a.dtype) for a in weights)
    blocks = 2 * _nbytes((tm, d), F32) + 2 * _nbytes((HALO, d), F32)
    scratch = (_nbytes((rows, d), BF16) + _nbytes((tm, d), F32)
               + _nbytes((2, rows, 2 * FFN_CHUNK), F32) + _nbytes((rows, 2 * tail), F32)
               + weight_bytes)
    temps = 4 * _nbytes((tm, 2 * FFN_CHUNK), F32) + _nbytes((tm, d), F32)
    return pl.pallas_call(
        functools.partial(_ffn_kernel, tm=tm, n_main=n_main),
        grid=(b, t // tm),
        in_specs=in_specs,
        out_specs=pl.BlockSpec((1, tm, d), lambda bi, i: (bi, i, 0)),
        out_shape=jax.ShapeDtypeStruct((b, t, d), F32),
        scratch_shapes=[
            pltpu.VMEM((rows, d), BF16),
            pltpu.VMEM((tm, d), F32),
            pltpu.VMEM((2, rows, 2 * FFN_CHUNK), F32),
            pltpu.VMEM((rows, 2 * tail), F32),
        ],
        compiler_params=pltpu.CompilerParams(
            dimension_semantics=("parallel", "parallel"),
            vmem_limit_bytes=_vmem_limit(blocks, scratch, temps)),
        name="conv_ffn",
    )(x, x, x, mods, mods, mods, pre_g, post_g, *weights)


def _rope_tables(t):
    rows = t // GRID_W
    row = jnp.repeat(jnp.arange(rows), GRID_W).astype(F32)
    col = jnp.tile(jnp.arange(GRID_W), rows).astype(F32)
    n_freq = HEAD_DIM // 4
    freqs = ROPE_BASE ** (-jnp.arange(n_freq, dtype=F32) / n_freq)
    ang = jnp.concatenate([row[:, None] * freqs, col[:, None] * freqs], axis=-1)
    cos, sin = jnp.cos(ang), jnp.sin(ang)
    return (jnp.concatenate([cos, cos, cos, cos], axis=-1),
            jnp.concatenate([-sin, sin, -sin, sin], axis=-1))


def _dup_heads(w, n_heads):
    d = w.shape[0]
    w = w.reshape(d, n_heads, 1, HEAD_DIM)
    return jnp.broadcast_to(w, (d, n_heads, 2, HEAD_DIM)).reshape(d, n_heads * PAIR)


def _sink_row(sink, tq):
    return jnp.repeat(sink.astype(F32) * LOG2E, tq, axis=1)[:, None, :]


def kernel(x, c, ctx, c_ctx, ada_w, ada_b, mix_pre_g, mix_post_g, ffn_pre_g, ffn_post_g,
           ab_w_in, ab_w_out, pool_w, pool_scale, sink_logit,
           c_w_qkv, c_w_out, c_q_g, c_k_g,
           ffn_w_up, ffn_conv_w, ffn_conv_b, ffn_w_down):
    batch, t, d = x.shape
    n_ctx = ctx.shape[1]
    depth = ada_w.shape[0]
    pool_ch = pool_scale.shape[1]
    b_kv = sink_logit.shape[1]
    b_q = b_kv * GQA_GROUP * HEAD_DIM
    c_q = c_w_out.shape[1]
    c_kv = c_q // HEAD_DIM // GQA_GROUP
    d_ff = ffn_w_down.shape[1]

    ctx_row = batch
    n_rows = -(-(batch + 1) // SUBLANES) * SUBLANES
    cc = jnp.zeros((n_rows, d), F32).at[:batch].set(c).at[ctx_row].set(c_ctx)
    mods_all = _ada_params(cc, ada_w, ada_b).reshape(depth, n_rows, 1, N_MOD * d)

    rope_tabs = _rope_tables(t)
    ones_bd = jnp.kron(jnp.ones((2, 1), F32),
                       jnp.kron(jnp.eye(2, dtype=F32), jnp.ones((HEAD_DIM, HEAD_DIM), F32))
                       ).astype(BF16)
    latent_row = lambda bi: bi
    context_row = lambda bi: ctx_row
    win_tq = _row_tile(t, Q_TILE)
    ctx_tq = _row_tile(n_ctx, Q_TILE)

    def qkv_weights(w, n_lead, n_kv_heads):
        k_end = n_lead + n_kv_heads * HEAD_DIM
        return jnp.concatenate([w[:, :n_lead], _dup_heads(w[:, n_lead:k_end], n_kv_heads),
                                w[:, k_end:]], axis=1).astype(BF16)

    xc = ctx
    for l in range(depth):
        need_ctx = l < depth - 1
        i = l // 2
        mods = mods_all[l]
        pre_g, post_g = mix_pre_g[l][None], mix_post_g[l][None]
        if l % 2 == 0:
            w_in = qkv_weights(ab_w_in[i], pool_ch + b_q, b_kv)
            widths = dict(n_u=pool_ch, n_q=b_q, n_kv=b_kv * PAIR, n_v=b_kv * HEAD_DIM)
            u, q, k2, vt = _inproj(x, mods, latent_row, pre_g, w_in, rope_tabs=rope_tabs,
                                   key_chunk=BAND_CHUNK, **widths)
            uc, qc, kc2, vtc = _inproj(xc, mods, context_row, pre_g, w_in, **widths)
            w_out = ab_w_out[i].astype(BF16)
            w_outs = [w_out[:pool_ch], w_out[pool_ch:]]
            pw, ps = pool_w[i].astype(BF16), pool_scale[i][None]
            a = _window_attention(q, k2, vt, kc2, vtc, _sink_row(sink_logit[i], win_tq),
                                  tq=win_tq)
            x = _outproj([_pool_mix(u, pw, ps), a], w_outs, x, mods, latent_row, post_g)
            if need_ctx:
                ac = _flash_attention(qc, kc2, vtc, sink_row=_sink_row(sink_logit[i], ctx_tq))
                xc = _outproj([_pool_mix(uc, pw, ps), ac], w_outs, xc, mods, context_row, post_g)
        else:
            w_in = qkv_weights(c_w_qkv[i], c_q, c_kv)
            gains = (jnp.tile(c_q_g[i], 2)[None], jnp.tile(c_k_g[i], 2)[None], ones_bd)
            widths = dict(n_u=0, n_q=c_q, n_kv=c_kv * PAIR, n_v=c_kv * HEAD_DIM)
            q, k2, vt = _inproj(x, mods, latent_row, pre_g, w_in, rope_tabs=rope_tabs,
                                qk_gains=gains, key_chunk=_flash_chunk(t), **widths)
            qc, kc2, vtc = _inproj(xc, mods, context_row, pre_g, w_in, qk_gains=gains, **widths)
            w_outs = [c_w_out[i].astype(BF16)]
            a = _flash_attention(q, kc2, vtc, k2, vt)
            x = _outproj([a], w_outs, x, mods, latent_row, post_g)
            if need_ctx:
                ac = _flash_attention(qc, kc2, vtc)
                xc = _outproj([ac], w_outs, xc, mods, context_row, post_g)

        ffn_args = (ffn_pre_g[l][None], ffn_post_g[l][None],
                    _ffn_weights(ffn_w_up[l], ffn_conv_w[l], ffn_conv_b[l], ffn_w_down[l]))
        x = _conv_ffn(x, mods, latent_row, *ffn_args)
        if need_ctx:
            xc = _conv_ffn(xc, mods, context_row, *ffn_args)
    return x
```

```python
import functools

import jax
import jax.numpy as jnp
from jax import lax
from jax.experimental import pallas as pl
from jax.experimental.pallas import tpu as pltpu

F32 = jnp.float32
BF16 = jnp.bfloat16

HEAD_DIM = 64
GQA_GROUP = 4
POOL_WINDOWS = (2, 4, 8, 16)
POOL_GC = 128
WINDOW = 128
GRID_W = 64
ROPE_BASE = 10000.0
EPS = 1e-6
NEG = -1e30
N_MOD = 6

LANES = 128
SUBLANES = 8
VMEM_LIMIT_CAP = 56 * 1024 * 1024
VMEM_LIMIT_FLOOR = 32 * 1024 * 1024
VMEM_MARGIN_DIV = 4
ROW_TILE = 512
PROJ_ROW_TILE = 1024
Q_TILE = 256

PAIR = 2 * HEAD_DIM
GROUP_COLS = GQA_GROUP * HEAD_DIM
HALO = SUBLANES
KEY_CHUNK = 256
FLASH_CHUNK = 512
BAND_CHUNK = WINDOW
LOG2E = 1.4426950408889634
Q_SCALE = HEAD_DIM ** -0.5 * LOG2E
ONES_ROWS = 16
VT_ROWS = HEAD_DIM + ONES_ROWS
FFN_CHUNK = 256


def _vmem_limit(block_bytes, scratch_bytes=0, temp_bytes=0):
    need = 2 * block_bytes + scratch_bytes + temp_bytes
    return int(min(max(need + need // VMEM_MARGIN_DIV, VMEM_LIMIT_FLOOR), VMEM_LIMIT_CAP))


def _nbytes(shape, dtype):
    n = 1
    for s in shape:
        n *= s
    return n * jnp.dtype(dtype).itemsize


def _row_tile(t, want):
    tile = min(t, want)
    assert t % tile == 0
    return tile


def _ada_kernel(c_ref, w_ref, b_ref, o_ref):
    c = c_ref[...]
    a = (c * jax.nn.sigmoid(c)).astype(BF16)
    o_ref[0] = jnp.dot(a, w_ref[0].astype(BF16), preferred_element_type=F32) + b_ref[0]


def _ada_params(cc, ada_w, ada_b):
    depth, d, n = ada_w.shape
    rows = cc.shape[0]
    tn = n // N_MOD
    blocks = _nbytes((rows, d), F32) + _nbytes((d, tn), F32) + _nbytes((rows, tn), F32)
    return pl.pallas_call(
        _ada_kernel,
        grid=(depth, n // tn),
        in_specs=[
            pl.BlockSpec((rows, d), lambda l, j: (0, 0)),
            pl.BlockSpec((1, d, tn), lambda l, j: (l, 0, j)),
            pl.BlockSpec((1, 1, tn), lambda l, j: (l, 0, j)),
        ],
        out_specs=pl.BlockSpec((1, rows, tn), lambda l, j: (l, 0, j)),
        out_shape=jax.ShapeDtypeStruct((depth, rows, n), F32),
        compiler_params=pltpu.CompilerParams(
            dimension_semantics=("parallel", "parallel"),
            vmem_limit_bytes=_vmem_limit(blocks, temp_bytes=_nbytes((d, tn), BF16))),
        name="ada_params",
    )(cc, ada_w, ada_b.reshape(depth, 1, n))


def _rms_norm(x, g):
    return x * lax.rsqrt(jnp.mean(x * x, axis=-1, keepdims=True) + EPS) * g


def _pre_mod(x, g, shift, scale):
    return _rms_norm(x, g) * (1.0 + scale) + shift


def _inproj_kernel(*refs, n_u, n_q, n_kv, n_v, key_chunk, rope, qk_norm, tm):
    x_ref, sh_ref, sc_ref, g_ref, w_ref = refs[:5]
    pos = 5
    if rope:
        cos_ref, sin_ref = refs[pos:pos + 2]
        pos += 2
    if qk_norm:
        qg_ref, kg_ref, ones_ref = refs[pos:pos + 3]
        pos += 3
    out_refs = refs[pos:]

    h = _pre_mod(x_ref[0], g_ref[...], sh_ref[0], sc_ref[0]).astype(BF16)

    if rope:
        lane = lax.broadcasted_iota(jnp.int32, (tm, PAIR), 1)
        first_half = (lane & (HEAD_DIM - 1)) < HEAD_DIM // 2
        cos = cos_ref[...]
        sin = sin_ref[...]

    def head_epilogue(a, gain_ref, scale):
        if qk_norm:
            a2 = a * a
            hi = a2.astype(BF16)
            lo = (a2 - hi.astype(F32)).astype(BF16)
            ss = jnp.dot(jnp.concatenate([hi, lo], axis=1), ones_ref[...],
                         preferred_element_type=F32)
            a = a * lax.rsqrt(ss * (1.0 / HEAD_DIM) + EPS) * gain_ref[...]
        if rope:
            partner = jnp.where(first_half,
                                pltpu.roll(a, PAIR - HEAD_DIM // 2, 1),
                                pltpu.roll(a, HEAD_DIM // 2, 1))
            a = a * cos + partner * sin
        if scale != 1.0:
            a = a * scale
        return a.astype(BF16)

    col = 0
    out_idx = 0
    if n_u:
        out_refs[out_idx][0] = jnp.dot(h, w_ref[:, col:col + n_u], preferred_element_type=F32)
        col += n_u
        out_idx += 1

    acc = jnp.dot(h, w_ref[:, col:col + n_q], preferred_element_type=F32)
    for c in range(n_q // PAIR):
        out_refs[out_idx][0, :, c * PAIR:(c + 1) * PAIR] = head_epilogue(
            acc[:, c * PAIR:(c + 1) * PAIR], qg_ref if qk_norm else None, Q_SCALE)
    col += n_q
    out_idx += 1

    acc = jnp.dot(h, w_ref[:, col:col + n_kv], preferred_element_type=F32)
    for c in range(n_kv // PAIR):
        out_refs[out_idx][0, :, c * PAIR:(c + 1) * PAIR] = head_epilogue(
            acc[:, c * PAIR:(c + 1) * PAIR], kg_ref if qk_norm else None, 1.0)
    col += n_kv
    out_idx += 1

    acc = jnp.dot(h, w_ref[:, col:col + n_v], preferred_element_type=F32)
    ones = jnp.ones((ONES_ROWS, key_chunk), F32)
    for cc in range(tm // key_chunk):
        at = acc[cc * key_chunk:(cc + 1) * key_chunk, :].T
        pieces = []
        for j in range(n_v // HEAD_DIM):
            pieces += [at[j * HEAD_DIM:(j + 1) * HEAD_DIM], ones]
        out_refs[out_idx][0, cc] = jnp.concatenate(pieces, axis=0).astype(BF16)


def _inproj(x, mods, mod_row, pre_g, w, *, n_u, n_q, n_kv, n_v, key_chunk=KEY_CHUNK,
            rope_tabs=None, qk_gains=None):
    b, t, d = x.shape
    n = w.shape[1]
    tm = _row_tile(t, PROJ_ROW_TILE)
    rope = rope_tabs is not None
    qk_norm = qk_gains is not None
    vt_rows = n_v // HEAD_DIM * VT_ROWS
    assert tm % key_chunk == 0

    in_specs = [
        pl.BlockSpec((1, tm, d), lambda bi, i: (bi, i, 0)),
        pl.BlockSpec((1, 1, d), lambda bi, i: (mod_row(bi), 0, 0)),
        pl.BlockSpec((1, 1, d), lambda bi, i: (mod_row(bi), 0, 1)),
        pl.BlockSpec((1, d), lambda bi, i: (0, 0)),
        pl.BlockSpec((d, n), lambda bi, i: (0, 0)),
    ]
    args = [x, mods, mods, pre_g, w]
    if rope:
        in_specs += [pl.BlockSpec((tm, PAIR), lambda bi, i: (i, 0))] * 2
        args += list(rope_tabs)
    if qk_norm:
        in_specs += [pl.BlockSpec((1, PAIR), lambda bi, i: (0, 0))] * 2
        in_specs += [pl.BlockSpec((2 * PAIR, PAIR), lambda bi, i: (0, 0))]
        args += list(qk_gains)

    out_shape, out_specs = [], []
    for width, dtype in ((n_u, F32), (n_q, BF16), (n_kv, BF16)):
        if width:
            out_shape.append(jax.ShapeDtypeStruct((b, t, width), dtype))
            out_specs.append(pl.BlockSpec((1, tm, width), lambda bi, i: (bi, i, 0)))
    out_shape.append(jax.ShapeDtypeStruct((b, t // key_chunk, vt_rows, key_chunk), BF16))
    out_specs.append(pl.BlockSpec((1, tm // key_chunk, vt_rows, key_chunk),
                                  lambda bi, i: (bi, i, 0, 0)))

    blocks = (_nbytes((tm, d), F32) + _nbytes((d, n), BF16) + _nbytes((tm, n_u), F32)
              + _nbytes((tm, n_q + n_kv + 2 * n_v), BF16) + 2 * _nbytes((tm, PAIR), F32))
    temps = _nbytes((tm, d), F32) * 2 + _nbytes((tm, max(n_q, n_kv, n_u)), F32) * 2
    return pl.pallas_call(
        functools.partial(_inproj_kernel, n_u=n_u, n_q=n_q, n_kv=n_kv, n_v=n_v,
                          key_chunk=key_chunk, rope=rope, qk_norm=qk_norm, tm=tm),
        grid=(b, t // tm),
        in_specs=in_specs,
        out_specs=out_specs,
        out_shape=out_shape,
        compiler_params=pltpu.CompilerParams(
            dimension_semantics=("parallel", "parallel"),
            vmem_limit_bytes=_vmem_limit(blocks, temp_bytes=temps)),
        name="mixer_inproj",
    )(*args)


def _stack_heads_transposed(q_ref, qt_ref, tq, rows=slice(None)):
    sub = lax.broadcasted_iota(jnp.int32, (PAIR, tq), 0)
    top = sub < HEAD_DIM
    for p in range(GROUP_COLS // PAIR):
        pair_t = q_ref[0, rows, p * PAIR:(p + 1) * PAIR].astype(F32).T
        qt_ref[:, (2 * p) * tq:(2 * p + 1) * tq] = jnp.where(top, pair_t, 0.0).astype(BF16)
        qt_ref[:, (2 * p + 1) * tq:(2 * p + 2) * tq] = jnp.where(top, 0.0, pair_t).astype(BF16)


def _store_heads(o_t, o_ref, tq, rows=slice(None)):
    for p in range(GROUP_COLS // PAIR):
        pair_t = jnp.concatenate([o_t[:, (2 * p) * tq:(2 * p + 1) * tq],
                                  o_t[:, (2 * p + 1) * tq:(2 * p + 2) * tq]], axis=0)
        o_ref[0, rows, p * PAIR:(p + 1) * PAIR] = pair_t.T.astype(o_ref.dtype)


def _flash_kernel(*refs, tq, n_tiles, n_lat, chunk, has_sink):
    refs = list(refs)
    q_ref, kc_ref, vtc_ref = refs[:3]
    del refs[:3]
    if n_lat:
        k_ref, vt_ref = refs[:2]
        del refs[:2]
    if has_sink:
        sink_ref = refs.pop(0)
    o_ref, qt_ref, sc_ref = refs[:3]
    del refs[:3]
    if n_lat:
        s_ref = refs.pop(0)
    m_ref, acc_ref = refs

    def tile_rows(i):
        return pl.ds(pl.multiple_of(i * tq, tq), tq)

    def scores(c):
        start = pl.multiple_of(c * chunk, chunk)
        return jnp.dot(k_ref[0, pl.ds(start, chunk), :], qt_ref[...],
                       preferred_element_type=F32)

    def begin_tile(i):
        _stack_heads_transposed(q_ref, qt_ref, tq, tile_rows(i))
        sc_ref[...] = jnp.dot(kc_ref[0], qt_ref[...], preferred_element_type=F32)
        if n_lat:
            s_ref[0] = scores(0)

    def update(s, vt):
        m_prev = m_ref[...]
        m_new = jnp.maximum(m_prev, jnp.max(s, axis=0, keepdims=True))
        alpha = jnp.exp2(m_prev - m_new)
        p = jnp.exp2(s - m_new)
        acc_ref[...] = alpha * acc_ref[...] + jnp.dot(vt, p.astype(BF16),
                                                     preferred_element_type=F32)
        m_ref[...] = m_new

    def run_tile(i, next_tile):
        if has_sink:
            m_ref[...] = sink_ref[0]
            den_rows = lax.broadcasted_iota(jnp.int32, acc_ref.shape, 0) >= HEAD_DIM
            acc_ref[...] = jnp.where(den_rows, 1.0, 0.0)
        else:
            m_ref[...] = jnp.full_like(m_ref, NEG)
            acc_ref[...] = jnp.zeros_like(acc_ref)
        update(sc_ref[...], vtc_ref[0, 0])
        if n_lat:
            def body(pair, carry):
                c = 2 * pair
                s_ref[1] = scores(c + 1)
                update(s_ref[0], vt_ref[0, c])
                s_ref[0] = scores(c + 2)
                update(s_ref[1], vt_ref[0, c + 1])
                return carry
            lax.fori_loop(0, n_lat // 2 - 1, body, 0, unroll=True)
            s_ref[1] = scores(n_lat - 1)
            update(s_ref[0], vt_ref[0, n_lat - 2])
        if next_tile is not None:
            begin_tile(next_tile)
        if n_lat:
            update(s_ref[1], vt_ref[0, n_lat - 1])
        _store_heads(acc_ref[0:HEAD_DIM, :] / acc_ref[HEAD_DIM:HEAD_DIM + 1, :], o_ref, tq,
                     tile_rows(i))

    begin_tile(0)
    if n_tiles == 1:
        run_tile(0, None)
    else:
        def tile_body(i, carry):
            run_tile(i, jnp.minimum(i + 1, n_tiles - 1))
            return carry
        lax.fori_loop(0, n_tiles, tile_body, 0, unroll=2 if n_tiles % 2 == 0 else 1)


def _flash_chunk(t):
    if t % FLASH_CHUNK == 0 and (t // FLASH_CHUNK) % 2 == 0:
        return FLASH_CHUNK
    return FLASH_CHUNK // 2


def _flash_attention(q, kc2, vtc, k2=None, vt=None, sink_row=None, *, tq_want=Q_TILE):
    b, t, hq = q.shape
    n_ctx = kc2.shape[1]
    n_groups = hq // GROUP_COLS
    tq = _row_tile(t, tq_want)
    n_tiles = t // tq
    cols = GQA_GROUP * tq
    has_sink = sink_row is not None
    n_lat, chunk = (vt.shape[1], vt.shape[3]) if vt is not None else (0, 0)
    assert n_lat % 2 == 0 and vtc.shape[1] == 1

    in_specs = [
        pl.BlockSpec((1, t, GROUP_COLS), lambda bi, j: (bi, 0, j)),
        pl.BlockSpec((1, n_ctx, PAIR), lambda bi, j: (bi, 0, j)),
        pl.BlockSpec((1, 1, VT_ROWS, n_ctx), lambda bi, j: (bi, 0, j, 0)),
    ]
    args = [q, kc2, vtc]
    scratch_shapes = [pltpu.VMEM((PAIR, cols), BF16), pltpu.VMEM((n_ctx, cols), F32)]
    if n_lat:
        assert k2.shape[1] == n_lat * chunk
        in_specs += [
            pl.BlockSpec((1, n_lat * chunk, PAIR), lambda bi, j: (bi, 0, j)),
            pl.BlockSpec((1, n_lat, VT_ROWS, chunk), lambda bi, j: (bi, 0, j, 0)),
        ]
        args += [k2, vt]
        scratch_shapes.append(pltpu.VMEM((2, chunk, cols), F32))
    if has_sink:
        in_specs.append(pl.BlockSpec((1, 1, cols), lambda bi, j: (j, 0, 0)))
        args.append(sink_row)
    scratch_shapes += [pltpu.VMEM((1, cols), F32), pltpu.VMEM((VT_ROWS, cols), F32)]
    blocks = (2 * _nbytes((t, GROUP_COLS), BF16)
              + _nbytes((n_ctx + n_lat * chunk, PAIR + VT_ROWS), BF16))
    scratch = (_nbytes((PAIR, cols), BF16) + _nbytes((n_ctx + 2 * chunk, cols), F32)
               + _nbytes((VT_ROWS + SUBLANES, cols), F32))
    temps = 2 * _nbytes((max(chunk, n_ctx), cols), F32)
    return pl.pallas_call(
        functools.partial(_flash_kernel, tq=tq, n_tiles=n_tiles, n_lat=n_lat, chunk=chunk,
                          has_sink=has_sink),
        grid=(b, n_groups),
        in_specs=in_specs,
        out_specs=pl.BlockSpec((1, t, GROUP_COLS), lambda bi, j: (bi, 0, j)),
        out_shape=jax.ShapeDtypeStruct((b, t, hq), BF16),
        scratch_shapes=scratch_shapes,
        compiler_params=pltpu.CompilerParams(
            dimension_semantics=("parallel", "parallel"),
            vmem_limit_bytes=_vmem_limit(blocks, scratch, temps)),
        name="flash_attention",
    )(*args)


def _window_kernel(q_ref, k_ref, vt_ref, kc_ref, vtc_ref, bias_ref, sink_ref, o_ref,
                   qt_ref, sb_ref, sc_ref, *, tq, t):
    span = tq + 2 * WINDOW
    n_tiles = t // tq

    def window_start(i):
        return pl.multiple_of(jnp.clip(i * tq - WINDOW, 0, t - span), WINDOW)

    def issue(i, slot):
        start = window_start(i)
        bias = bias_ref[(i * tq - start) // WINDOW]
        _stack_heads_transposed(q_ref, qt_ref, tq, pl.ds(pl.multiple_of(i * tq, tq), tq))
        qt = qt_ref[...]
        s_band = jnp.dot(k_ref[0, pl.ds(start, span), :], qt, preferred_element_type=F32)
        sb_ref[slot] = s_band + jnp.concatenate([bias] * GQA_GROUP, axis=1)
        sc_ref[slot] = jnp.dot(kc_ref[0], qt, preferred_element_type=F32)

    def finish(i, slot):
        s_band, s_ctx = sb_ref[slot], sc_ref[slot]
        sink = sink_ref[0]
        m = jnp.maximum(jnp.maximum(jnp.max(s_band, axis=0, keepdims=True),
                                    jnp.max(s_ctx, axis=0, keepdims=True)), sink)
        p_band = jnp.exp2(s_band - m).astype(BF16)
        p_ctx = jnp.exp2(s_ctx - m).astype(BF16)
        first = window_start(i) // BAND_CHUNK
        vt_band = jnp.concatenate([vt_ref[0, first + c] for c in range(span // BAND_CHUNK)],
                                  axis=1)
        acc = (jnp.dot(vt_band, p_band, preferred_element_type=F32)
               + jnp.dot(vtc_ref[0, 0], p_ctx, preferred_element_type=F32))
        den = acc[HEAD_DIM:HEAD_DIM + 1, :] + jnp.exp2(sink - m)
        _store_heads(acc[0:HEAD_DIM, :] / den, o_ref, tq,
                     pl.ds(pl.multiple_of(i * tq, tq), tq))

    issue(0, 0)

    def body(pair, carry):
        i = 2 * pair
        issue(i + 1, 1)
        finish(i, 0)
        issue(i + 2, 0)
        finish(i + 1, 1)
        return carry
    lax.fori_loop(0, n_tiles // 2 - 1, body, 0)
    issue(n_tiles - 1, 1)
    finish(n_tiles - 2, 0)
    finish(n_tiles - 1, 1)


def _band_bias(tq):
    span = tq + 2 * WINDOW
    key = jnp.arange(span)[None, :, None]
    qry = jnp.arange(tq)[None, None, :]
    back = (jnp.arange(3) * WINDOW)[:, None, None]
    delta = qry + back - key
    return jnp.where(jnp.abs(delta) <= WINDOW, 0.0, NEG).astype(F32)


def _window_attention(q, k2, vt, kc2, vtc, sink_row, *, tq):
    b, t, hq = q.shape
    n_ctx = kc2.shape[1]
    n_groups = hq // GROUP_COLS
    cols = GQA_GROUP * tq
    span = tq + 2 * WINDOW
    n_band = t // BAND_CHUNK
    assert t % (2 * tq) == 0 and t >= span and tq % WINDOW == 0 and vtc.shape[1] == 1
    blocks = (2 * _nbytes((t, GROUP_COLS), BF16) + _nbytes((t, PAIR + VT_ROWS), BF16)
              + _nbytes((n_ctx, PAIR + VT_ROWS), BF16) + _nbytes((3, span, tq), F32))
    scratch = _nbytes((PAIR, cols), BF16) + 2 * _nbytes((span + n_ctx, cols), F32)
    temps = 2 * _nbytes((span + n_ctx, cols), F32)
    return pl.pallas_call(
        functools.partial(_window_kernel, tq=tq, t=t),
        grid=(b, n_groups),
        in_specs=[
            pl.BlockSpec((1, t, GROUP_COLS), lambda bi, j: (bi, 0, j)),
            pl.BlockSpec((1, t, PAIR), lambda bi, j: (bi, 0, j)),
            pl.BlockSpec((1, n_band, VT_ROWS, BAND_CHUNK), lambda bi, j: (bi, 0, j, 0)),
            pl.BlockSpec((1, n_ctx, PAIR), lambda bi, j: (bi, 0, j)),
            pl.BlockSpec((1, 1, VT_ROWS, n_ctx), lambda bi, j: (bi, 0, j, 0)),
            pl.BlockSpec((3, span, tq), lambda bi, j: (0, 0, 0)),
            pl.BlockSpec((1, 1, cols), lambda bi, j: (j, 0, 0)),
        ],
        out_specs=pl.BlockSpec((1, t, GROUP_COLS), lambda bi, j: (bi, 0, j)),
        out_shape=jax.ShapeDtypeStruct((b, t, hq), BF16),
        scratch_shapes=[
            pltpu.VMEM((PAIR, cols), BF16),
            pltpu.VMEM((2, span, cols), F32),
            pltpu.VMEM((2, n_ctx, cols), F32),
        ],
        compiler_params=pltpu.CompilerParams(
            dimension_semantics=("parallel", "parallel"),
            vmem_limit_bytes=_vmem_limit(blocks, scratch, temps)),
        name="window_attention",
    )(q, k2, vt, kc2, vtc, _band_bias(tq), sink_row)


def _pool_kernel(u_ref, w_ref, scale_ref, o_ref, pad_ref, *, t):
    reach = max(POOL_WINDOWS) // 2
    assert reach <= HALO
    pos = lax.broadcasted_iota(jnp.int32, (t, POOL_GC), 0)
    zeros = jnp.zeros((HALO, POOL_GC), F32)
    pad_ref[0:HALO, :] = zeros
    pad_ref[HALO + t:2 * HALO + t, :] = zeros
    for g, win in enumerate(POOL_WINDOWS):
        half = win // 2
        u = u_ref[0, :, g * POOL_GC:(g + 1) * POOL_GC]
        pad_ref[HALO:HALO + t, :] = u
        total = pad_ref[pl.ds(HALO - half, t), :]
        for off in range(1 - half, half):
            total = total + pad_ref[pl.ds(HALO + off, t), :]
        cnt = (jnp.minimum(pos + half, t) - jnp.maximum(pos - half, 0)).astype(F32)
        d = (total / cnt - u).astype(BF16)
        y = jnp.dot(d, w_ref[g], preferred_element_type=F32)
        o_ref[0, :, g * POOL_GC:(g + 1) * POOL_GC] = (
            y * scale_ref[:, g * POOL_GC:(g + 1) * POOL_GC]).astype(BF16)


def _pool_mix(u, pool_w, pool_scale):
    b, t, ch = u.shape
    n_g = len(POOL_WINDOWS)
    assert ch == n_g * POOL_GC
    blocks = _nbytes((t, ch), F32) + _nbytes((t, ch), BF16) + _nbytes((n_g, POOL_GC, POOL_GC), BF16)
    scratch = _nbytes((t + 2 * HALO, POOL_GC), F32)
    return pl.pallas_call(
        functools.partial(_pool_kernel, t=t),
        grid=(b,),
        in_specs=[
            pl.BlockSpec((1, t, ch), lambda bi: (bi, 0, 0)),
            pl.BlockSpec((n_g, POOL_GC, POOL_GC), lambda bi: (0, 0, 0)),
            pl.BlockSpec((1, ch), lambda bi: (0, 0)),
        ],
        out_specs=pl.BlockSpec((1, t, ch), lambda bi: (bi, 0, 0)),
        out_shape=jax.ShapeDtypeStruct((b, t, ch), BF16),
        scratch_shapes=[pltpu.VMEM((t + 2 * HALO, POOL_GC), F32)],
        compiler_params=pltpu.CompilerParams(
            dimension_semantics=("parallel",),
            vmem_limit_bytes=_vmem_limit(blocks, scratch, 6 * _nbytes((t, POOL_GC), F32))),
        name="pool_mix",
    )(u, pool_w, pool_scale)


def _outproj_kernel(*refs, n_in):
    a_refs = refs[:n_in]
    w_refs = refs[n_in:2 * n_in]
    x_ref, gate_ref, g_ref, o_ref = refs[2 * n_in:]
    y = jnp.dot(a_refs[0][0], w_refs[0][...], preferred_element_type=F32)
    for a_ref, w_ref in zip(a_refs[1:], w_refs[1:]):
        y = y + jnp.dot(a_ref[0], w_ref[...], preferred_element_type=F32)
    o_ref[0] = x_ref[0] + gate_ref[0] * _rms_norm(y, g_ref[...])


def _outproj(acts, weights, x, mods, mod_row, post_g):
    b, t, d = x.shape
    tm = _row_tile(t, PROJ_ROW_TILE)
    n_in = len(acts)
    in_specs = [pl.BlockSpec((1, tm, a.shape[2]), lambda bi, i: (bi, i, 0)) for a in acts]
    in_specs += [pl.BlockSpec(w.shape, lambda bi, i: (0, 0)) for w in weights]
    in_specs += [
        pl.BlockSpec((1, tm, d), lambda bi, i: (bi, i, 0)),
        pl.BlockSpec((1, 1, d), lambda bi, i: (mod_row(bi), 0, 2)),
        pl.BlockSpec((1, d), lambda bi, i: (0, 0)),
    ]
    blocks = (sum(_nbytes((tm, a.shape[2]), BF16) for a in acts)
              + sum(_nbytes(w.shape, BF16) for w in weights) + 2 * _nbytes((tm, d), F32))
    return pl.pallas_call(
        functools.partial(_outproj_kernel, n_in=n_in),
        grid=(b, t // tm),
        in_specs=in_specs,
        out_specs=pl.BlockSpec((1, tm, d), lambda bi, i: (bi, i, 0)),
        out_shape=jax.ShapeDtypeStruct((b, t, d), F32),
        compiler_params=pltpu.CompilerParams(
            dimension_semantics=("parallel", "parallel"),
            vmem_limit_bytes=_vmem_limit(blocks, temp_bytes=3 * _nbytes((tm, d), F32))),
        name="mixer_outproj",
    )(*acts, *weights, x, mods, post_g)


def _ffn_kernel(x_ref, xp_ref, xn_ref, sh_ref, sc_ref, gate_ref, pre_g_ref, post_g_ref,
                wu_ref, wut_ref, cw_ref, cwt_ref, cb_ref, cbt_ref, wd_ref, wdt_ref, o_ref,
                h_ref, acc_ref, u_ref, ut_ref, *, tm, n_main):
    i = pl.program_id(1)
    g, sh, sc = pre_g_ref[...], sh_ref[0], sc_ref[0]
    h_prev = _pre_mod(xp_ref[0], g, sh, sc)
    h_next = _pre_mod(xn_ref[0], g, sh, sc)
    h_prev = jnp.where(i == 0, 0.0, h_prev)
    h_next = jnp.where(i == pl.num_programs(1) - 1, 0.0, h_next)
    h_ref[0:HALO, :] = h_prev.astype(BF16)
    h_ref[HALO:HALO + tm, :] = _pre_mod(x_ref[0], g, sh, sc).astype(BF16)
    h_ref[HALO + tm:2 * HALO + tm, :] = h_next.astype(BF16)

    def up(c):
        return jnp.dot(h_ref[...], wu_ref[c], preferred_element_type=F32)

    def conv_down(u, cw, cb, wd):
        w = wd.shape[0]
        uv = u[...]
        n_rows = uv.shape[0]
        conv = (pltpu.roll(uv, 1, 0)[HALO:HALO + tm] * cw[0:1, :]
                + uv[HALO:HALO + tm] * cw[1:2, :]
                + pltpu.roll(uv, n_rows - 1, 0)[HALO:HALO + tm] * cw[2:3, :]
                + cb)
        gated = jax.nn.gelu(conv[:, :w]) * conv[:, w:]
        return jnp.dot(gated.astype(BF16), wd, preferred_element_type=F32)

    def consume(slot, c):
        return conv_down(u_ref.at[slot], cw_ref[c], cb_ref[c], wd_ref[c])

    u_ref[0] = up(0)
    acc_ref[...] = jnp.zeros_like(acc_ref)

    def body(pair, carry):
        c = 2 * pair
        u_ref[1] = up(c + 1)
        acc_ref[...] += consume(0, c)
        u_ref[0] = up(c + 2)
        acc_ref[...] += consume(1, c + 1)
        return carry
    lax.fori_loop(0, (n_main - 1) // 2, body, 0, unroll=True)
    last = n_main - 1
    if n_main % 2 == 0:
        u_ref[1] = up(last)
        acc_ref[...] += consume(0, last - 1)
    ut_ref[...] = jnp.dot(h_ref[...], wut_ref[...], preferred_element_type=F32)
    acc_ref[...] += consume(last % 2, last)
    y = acc_ref[...] + conv_down(ut_ref, cwt_ref[...], cbt_ref[...], wdt_ref[...])
    o_ref[0] = x_ref[0] + gate_ref[0] * _rms_norm(y, post_g_ref[...])


def _ffn_weights(w_up, conv_w, conv_b, w_down):
    d_ff = w_down.shape[0]
    n_main = d_ff // FFN_CHUNK
    split = n_main * FFN_CHUNK

    def pair_cols(a):
        r = a.shape[0]
        gelu, gate = a[:, :d_ff], a[:, d_ff:]
        main = jnp.concatenate([gelu[:, :split].reshape(r, n_main, FFN_CHUNK),
                                gate[:, :split].reshape(r, n_main, FFN_CHUNK)], axis=2)
        tail = jnp.concatenate([gelu[:, split:], gate[:, split:]], axis=1)
        return jnp.swapaxes(main, 0, 1), tail

    wu, wut = pair_cols(w_up.astype(BF16))
    cw, cwt = pair_cols(conv_w)
    cb, cbt = pair_cols(conv_b[None])
    wd = w_down.astype(BF16)
    return (wu, wut, cw, cwt, cb, cbt, wd[:split].reshape(n_main, FFN_CHUNK, -1), wd[split:])


def _conv_ffn(x, mods, mod_row, pre_g, post_g, weights, *, tm_want=ROW_TILE):
    b, t, d = x.shape
    tm = _row_tile(t, tm_want)
    n_main = weights[0].shape[0]
    tail = weights[-1].shape[0]
    assert tm % HALO == 0 and n_main >= 2 and tail > 0
    halo_blocks = tm // HALO
    last_halo = t // HALO - 1
    rows = tm + 2 * HALO

    def resident(a):
        zeros = (0,) * a.ndim
        return pl.BlockSpec(a.shape, lambda bi, i: zeros, pipeline_mode=pl.Buffered(1))

    in_specs = [
        pl.BlockSpec((1, tm, d), lambda bi, i: (bi, i, 0)),
        pl.BlockSpec((1, HALO, d), lambda bi, i: (bi, jnp.maximum(i * halo_blocks - 1, 0), 0)),
        pl.BlockSpec((1, HALO, d),
                     lambda bi, i: (bi, jnp.minimum((i + 1) * halo_blocks, last_halo), 0)),
        pl.BlockSpec((1, 1, d), lambda bi, i: (mod_row(bi), 0, 3)),
        pl.BlockSpec((1, 1, d), lambda bi, i: (mod_row(bi), 0, 4)),
        pl.BlockSpec((1, 1, d), lambda bi, i: (mod_row(bi), 0, 5)),
        pl.BlockSpec((1, d), lambda bi, i: (0, 0)),
        pl.BlockSpec((1, d), lambda bi, i: (0, 0)),
    ] + [resident(a) for a in weights]
    weight_bytes = sum(_nbytes(a.shape, a.dtype) for a in weights)
    blocks = 2 * _nbytes((tm, d), F32) + 2 * _nbytes((HALO, d), F32)
    scratch = (_nbytes((rows, d), BF16) + _nbytes((tm, d), F32)
               + _nbytes((2, rows, 2 * FFN_CHUNK), F32) + _nbytes((rows, 2 * tail), F32)
               + weight_bytes)
    temps = 4 * _nbytes((tm, 2 * FFN_CHUNK), F32) + _nbytes((tm, d), F32)
    return pl.pallas_call(
        functools.partial(_ffn_kernel, tm=tm, n_main=n_main),
        grid=(b, t // tm),
        in_specs=in_specs,
        out_specs=pl.BlockSpec((1, tm, d), lambda bi, i: (bi, i, 0)),
        out_shape=jax.ShapeDtypeStruct((b, t, d), F32),
        scratch_shapes=[
            pltpu.VMEM((rows, d), BF16),
            pltpu.VMEM((tm, d), F32),
            pltpu.VMEM((2, rows, 2 * FFN_CHUNK), F32),
            pltpu.VMEM((rows, 2 * tail), F32),
        ],
        compiler_params=pltpu.CompilerParams(
            dimension_semantics=("parallel", "parallel"),
            vmem_limit_bytes=_vmem_limit(blocks, scratch, temps)),
        name="conv_ffn",
    )(x, x, x, mods, mods, mods, pre_g, post_g, *weights)


def _rope_tables(t):
    rows = t // GRID_W
    row = jnp.repeat(jnp.arange(rows), GRID_W).astype(F32)
    col = jnp.tile(jnp.arange(GRID_W), rows).astype(F32)
    n_freq = HEAD_DIM // 4
    freqs = ROPE_BASE ** (-jnp.arange(n_freq, dtype=F32) / n_freq)
    ang = jnp.concatenate([row[:, None] * freqs, col[:, None] * freqs], axis=-1)
    cos, sin = jnp.cos(ang), jnp.sin(ang)
    return (jnp.concatenate([cos, cos, cos, cos], axis=-1),
            jnp.concatenate([-sin, sin, -sin, sin], axis=-1))


def _dup_heads(w, n_heads):
    d = w.shape[0]
    w = w.reshape(d, n_heads, 1, HEAD_DIM)
    return jnp.broadcast_to(w, (d, n_heads, 2, HEAD_DIM)).reshape(d, n_heads * PAIR)


def _sink_row(sink, tq):
    return jnp.repeat(sink.astype(F32) * LOG2E, tq, axis=1)[:, None, :]


def kernel(x, c, ctx, c_ctx, ada_w, ada_b, mix_pre_g, mix_post_g, ffn_pre_g, ffn_post_g,
           ab_w_in, ab_w_out, pool_w, pool_scale, sink_logit,
           c_w_qkv, c_w_out, c_q_g, c_k_g,
           ffn_w_up, ffn_conv_w, ffn_conv_b, ffn_w_down):
    batch, t, d = x.shape
    n_ctx = ctx.shape[1]
    depth = ada_w.shape[0]
    pool_ch = pool_scale.shape[1]
    b_kv = sink_logit.shape[1]
    b_q = b_kv * GQA_GROUP * HEAD_DIM
    c_q = c_w_out.shape[1]
    c_kv = c_q // HEAD_DIM // GQA_GROUP
    d_ff = ffn_w_down.shape[1]

    ctx_row = batch
    n_rows = -(-(batch + 1) // SUBLANES) * SUBLANES
    cc = jnp.zeros((n_rows, d), F32).at[:batch].set(c).at[ctx_row].set(c_ctx)
    mods_all = _ada_params(cc, ada_w, ada_b).reshape(depth, n_rows, 1, N_MOD * d)

    rope_tabs = _rope_tables(t)
    ones_bd = jnp.kron(jnp.ones((2, 1), F32),
                       jnp.kron(jnp.eye(2, dtype=F32), jnp.ones((HEAD_DIM, HEAD_DIM), F32))
                       ).astype(BF16)
    latent_row = lambda bi: bi
    context_row = lambda bi: ctx_row
    win_tq = _row_tile(t, Q_TILE)
    ctx_tq = _row_tile(n_ctx, Q_TILE)

    def qkv_weights(w, n_lead, n_kv_heads):
        k_end = n_lead + n_kv_heads * HEAD_DIM
        return jnp.concatenate([w[:, :n_lead], _dup_heads(w[:, n_lead:k_end], n_kv_heads),
                                w[:, k_end:]], axis=1).astype(BF16)

    xc = ctx
    for l in range(depth):
        need_ctx = l < depth - 1
        i = l // 2
        mods = mods_all[l]
        pre_g, post_g = mix_pre_g[l][None], mix_post_g[l][None]
        if l % 2 == 0:
            w_in = qkv_weights(ab_w_in[i], pool_ch + b_q, b_kv)
            widths = dict(n_u=pool_ch, n_q=b_q, n_kv=b_kv * PAIR, n_v=b_kv * HEAD_DIM)
            u, q, k2, vt = _inproj(x, mods, latent_row, pre_g, w_in, rope_tabs=rope_tabs,
                                   key_chunk=BAND_CHUNK, **widths)
            uc, qc, kc2, vtc = _inproj(xc, mods, context_row, pre_g, w_in, **widths)
            w_out = ab_w_out[i].astype(BF16)
            w_outs = [w_out[:pool_ch], w_out[pool_ch:]]
            pw, ps = pool_w[i].astype(BF16), pool_scale[i][None]
            a = _window_attention(q, k2, vt, kc2, vtc, _sink_row(sink_logit[i], win_tq),
                                  tq=win_tq)
            x = _outproj([_pool_mix(u, pw, ps), a], w_outs, x, mods, latent_row, post_g)
            if need_ctx:
                ac = _flash_attention(qc, kc2, vtc, sink_row=_sink_row(sink_logit[i], ctx_tq))
                xc = _outproj([_pool_mix(uc, pw, ps), ac], w_outs, xc, mods, context_row, post_g)
        else:
            w_in = qkv_weights(c_w_qkv[i], c_q, c_kv)
            gains = (jnp.tile(c_q_g[i], 2)[None], jnp.tile(c_k_g[i], 2)[None], ones_bd)
            widths = dict(n_u=0, n_q=c_q, n_kv=c_kv * PAIR, n_v=c_kv * HEAD_DIM)
            q, k2, vt = _inproj(x, mods, latent_row, pre_g, w_in, rope_tabs=rope_tabs,
                                qk_gains=gains, key_chunk=_flash_chunk(t), **widths)
            qc, kc2, vtc = _inproj(xc, mods, context_row, pre_g, w_in, qk_gains=gains, **widths)
            w_outs = [c_w_out[i].astype(BF16)]
            a = _flash_attention(q, kc2, vtc, k2, vt)
            x = _outproj([a], w_outs, x, mods, latent_row, post_g)
            if need_ctx:
                ac = _flash_attention(qc, kc2, vtc)
                xc = _outproj([ac], w_outs, xc, mods, context_row, post_g)

        ffn_args = (ffn_pre_g[l][None], ffn_post_g[l][None],
                    _ffn_weights(ffn_w_up[l], ffn_conv_w[l], ffn_conv_b[l], ffn_w_down[l]))
        x = _conv_ffn(x, mods, latent_row, *ffn_args)
        if need_ctx:
            xc = _conv_ffn(xc, mods, context_row, *ffn_args)
    return x
```

```python
import functools

import jax
import jax.numpy as jnp
from jax import lax
from jax.experimental import pallas as pl
from jax.experimental.pallas import tpu as pltpu

F32 = jnp.float32
BF16 = jnp.bfloat16

HEAD_DIM = 64
GQA_GROUP = 4
POOL_WINDOWS = (2, 4, 8, 16)
POOL_GC = 128
WINDOW = 128
GRID_W = 64
ROPE_BASE = 10000.0
EPS = 1e-6
NEG = -1e30
N_MOD = 6

LANES = 128
SUBLANES = 8
VMEM_LIMIT_CAP = 56 * 1024 * 1024
VMEM_LIMIT_FLOOR = 32 * 1024 * 1024
VMEM_MARGIN_DIV = 4
ROW_TILE = 512
PROJ_ROW_TILE = 1024
Q_TILE = 256

PAIR = 2 * HEAD_DIM
GROUP_COLS = GQA_GROUP * HEAD_DIM
HALO = SUBLANES
KEY_CHUNK = 256
FLASH_CHUNK = 512
BAND_CHUNK = WINDOW
LOG2E = 1.4426950408889634
Q_SCALE = HEAD_DIM ** -0.5 * LOG2E
ONES_ROWS = 16
VT_ROWS = HEAD_DIM + ONES_ROWS
FFN_CHUNK = 256


def _vmem_limit(block_bytes, scratch_bytes=0, temp_bytes=0):
    need = 2 * block_bytes + scratch_bytes + temp_bytes
    return int(min(max(need + need // VMEM_MARGIN_DIV, VMEM_LIMIT_FLOOR), VMEM_LIMIT_CAP))


def _nbytes(shape, dtype):
    n = 1
    for s in shape:
        n *= s
    return n * jnp.dtype(dtype).itemsize


def _row_tile(t, want):
    tile = min(t, want)
    assert t % tile == 0
    return tile


def _ada_kernel(c_ref, w_ref, b_ref, o_ref):
    c = c_ref[...]
    a = (c * jax.nn.sigmoid(c)).astype(BF16)
    o_ref[0] = jnp.dot(a, w_ref[0].astype(BF16), preferred_element_type=F32) + b_ref[0]


def _ada_params(cc, ada_w, ada_b):
    depth, d, n = ada_w.shape
    rows = cc.shape[0]
    tn = n // N_MOD
    blocks = _nbytes((rows, d), F32) + _nbytes((d, tn), F32) + _nbytes((rows, tn), F32)
    return pl.pallas_call(
        _ada_kernel,
        grid=(depth, n // tn),
        in_specs=[
            pl.BlockSpec((rows, d), lambda l, j: (0, 0)),
            pl.BlockSpec((1, d, tn), lambda l, j: (l, 0, j)),
            pl.BlockSpec((1, 1, tn), lambda l, j: (l, 0, j)),
        ],
        out_specs=pl.BlockSpec((1, rows, tn), lambda l, j: (l, 0, j)),
        out_shape=jax.ShapeDtypeStruct((depth, rows, n), F32),
        compiler_params=pltpu.CompilerParams(
            dimension_semantics=("parallel", "parallel"),
            vmem_limit_bytes=_vmem_limit(blocks, temp_bytes=_nbytes((d, tn), BF16))),
        name="ada_params",
    )(cc, ada_w, ada_b.reshape(depth, 1, n))


def _rms_norm(x, g):
    return x * lax.rsqrt(jnp.mean(x * x, axis=-1, keepdims=True) + EPS) * g


def _pre_mod(x, g, shift, scale):
    return _rms_norm(x, g) * (1.0 + scale) + shift


def _inproj_kernel(*refs, n_u, n_q, n_kv, n_v, key_chunk, rope, qk_norm, tm):
    x_ref, sh_ref, sc_ref, g_ref, w_ref = refs[:5]
    pos = 5
    if rope:
        cos_ref, sin_ref = refs[pos:pos + 2]
        pos += 2
    if qk_norm:
        qg_ref, kg_ref, ones_ref = refs[pos:pos + 3]
        pos += 3
    out_refs = refs[pos:]

    h = _pre_mod(x_ref[0], g_ref[...], sh_ref[0], sc_ref[0]).astype(BF16)

    if rope:
        lane = lax.broadcasted_iota(jnp.int32, (tm, PAIR), 1)
        first_half = (lane & (HEAD_DIM - 1)) < HEAD_DIM // 2
        cos = cos_ref[...]
        sin = sin_ref[...]

    def head_epilogue(a, gain_ref, scale):
        if qk_norm:
            a2 = a * a
            hi = a2.astype(BF16)
            lo = (a2 - hi.astype(F32)).astype(BF16)
            ss = jnp.dot(jnp.concatenate([hi, lo], axis=1), ones_ref[...],
                         preferred_element_type=F32)
            a = a * lax.rsqrt(ss * (1.0 / HEAD_DIM) + EPS) * gain_ref[...]
        if rope:
            partner = jnp.where(first_half,
                                pltpu.roll(a, PAIR - HEAD_DIM // 2, 1),
                                pltpu.roll(a, HEAD_DIM // 2, 1))
            a = a * cos + partner * sin
        if scale != 1.0:
            a = a * scale
        return a.astype(BF16)

    col = 0
    out_idx = 0
    if n_u:
        out_refs[out_idx][0] = jnp.dot(h, w_ref[:, col:col + n_u], preferred_element_type=F32)
        col += n_u
        out_idx += 1

    acc = jnp.dot(h, w_ref[:, col:col + n_q], preferred_element_type=F32)
    for c in range(n_q // PAIR):
        out_refs[out_idx][0, :, c * PAIR:(c + 1) * PAIR] = head_epilogue(
            acc[:, c * PAIR:(c + 1) * PAIR], qg_ref if qk_norm else None, Q_SCALE)
    col += n_q
    out_idx += 1

    acc = jnp.dot(h, w_ref[:, col:col + n_kv], preferred_element_type=F32)
    for c in range(n_kv // PAIR):
        out_refs[out_idx][0, :, c * PAIR:(c + 1) * PAIR] = head_epilogue(
            acc[:, c * PAIR:(c + 1) * PAIR], kg_ref if qk_norm else None, 1.0)
    col += n_kv
    out_idx += 1

    acc = jnp.dot(h, w_ref[:, col:col + n_v], preferred_element_type=F32)
    ones = jnp.ones((ONES_ROWS, key_chunk), F32)
    for cc in range(tm // key_chunk):
        at = acc[cc * key_chunk:(cc + 1) * key_chunk, :].T
        pieces = []
        for j in range(n_v // HEAD_DIM):
            pieces += [at[j * HEAD_DIM:(j + 1) * HEAD_DIM], ones]
        out_refs[out_idx][0, cc] = jnp.concatenate(pieces, axis=0).astype(BF16)


def _inproj(x, mods, mod_row, pre_g, w, *, n_u, n_q, n_kv, n_v, key_chunk=KEY_CHUNK,
            rope_tabs=None, qk_gains=None):
    b, t, d = x.shape
    n = w.shape[1]
    tm = _row_tile(t, PROJ_ROW_TILE)
    rope = rope_tabs is not None
    qk_norm = qk_gains is not None
    vt_rows = n_v // HEAD_DIM * VT_ROWS
    assert tm % key_chunk == 0

    in_specs = [
        pl.BlockSpec((1, tm, d), lambda bi, i: (bi, i, 0)),
        pl.BlockSpec((1, 1, d), lambda bi, i: (mod_row(bi), 0, 0)),
        pl.BlockSpec((1, 1, d), lambda bi, i: (mod_row(bi), 0, 1)),
        pl.BlockSpec((1, d), lambda bi, i: (0, 0)),
        pl.BlockSpec((d, n), lambda bi, i: (0, 0)),
    ]
    args = [x, mods, mods, pre_g, w]
    if rope:
        in_specs += [pl.BlockSpec((tm, PAIR), lambda bi, i: (i, 0))] * 2
        args += list(rope_tabs)
    if qk_norm:
        in_specs += [pl.BlockSpec((1, PAIR), lambda bi, i: (0, 0))] * 2
        in_specs += [pl.BlockSpec((2 * PAIR, PAIR), lambda bi, i: (0, 0))]
        args += list(qk_gains)

    out_shape, out_specs = [], []
    for width, dtype in ((n_u, F32), (n_q, BF16), (n_kv, BF16)):
        if width:
            out_shape.append(jax.ShapeDtypeStruct((b, t, width), dtype))
            out_specs.append(pl.BlockSpec((1, tm, width), lambda bi, i: (bi, i, 0)))
    out_shape.append(jax.ShapeDtypeStruct((b, t // key_chunk, vt_rows, key_chunk), BF16))
    out_specs.append(pl.BlockSpec((1, tm // key_chunk, vt_rows, key_chunk),
                                  lambda bi, i: (bi, i, 0, 0)))

    blocks = (_nbytes((tm, d), F32) + _nbytes((d, n), BF16) + _nbytes((tm, n_u), F32)
              + _nbytes((tm, n_q + n_kv + 2 * n_v), BF16) + 2 * _nbytes((tm, PAIR), F32))
    temps = _nbytes((tm, d), F32) * 2 + _nbytes((tm, max(n_q, n_kv, n_u)), F32) * 2
    return pl.pallas_call(
        functools.partial(_inproj_kernel, n_u=n_u, n_q=n_q, n_kv=n_kv, n_v=n_v,
                          key_chunk=key_chunk, rope=rope, qk_norm=qk_norm, tm=tm),
        grid=(b, t // tm),
        in_specs=in_specs,
        out_specs=out_specs,
        out_shape=out_shape,
        compiler_params=pltpu.CompilerParams(
            dimension_semantics=("parallel", "parallel"),
            vmem_limit_bytes=_vmem_limit(blocks, temp_bytes=temps)),
        name="mixer_inproj",
    )(*args)


def _stack_heads_transposed(q_ref, qt_ref, tq, rows=slice(None)):
    sub = lax.broadcasted_iota(jnp.int32, (PAIR, tq), 0)
    top = sub < HEAD_DIM
    for p in range(GROUP_COLS // PAIR):
        pair_t = q_ref[0, rows, p * PAIR:(p + 1) * PAIR].astype(F32).T
        qt_ref[:, (2 * p) * tq:(2 * p + 1) * tq] = jnp.where(top, pair_t, 0.0).astype(BF16)
        qt_ref[:, (2 * p + 1) * tq:(2 * p + 2) * tq] = jnp.where(top, 0.0, pair_t).astype(BF16)


def _store_heads(o_t, o_ref, tq, rows=slice(None)):
    for p in range(GROUP_COLS // PAIR):
        pair_t = jnp.concatenate([o_t[:, (2 * p) * tq:(2 * p + 1) * tq],
                                  o_t[:, (2 * p + 1) * tq:(2 * p + 2) * tq]], axis=0)
        o_ref[0, rows, p * PAIR:(p + 1) * PAIR] = pair_t.T.astype(o_ref.dtype)


def _flash_kernel(*refs, tq, n_tiles, n_lat, chunk, has_sink):
    refs = list(refs)
    q_ref, kc_ref, vtc_ref = refs[:3]
    del refs[:3]
    if n_lat:
        k_ref, vt_ref = refs[:2]
        del refs[:2]
    if has_sink:
        sink_ref = refs.pop(0)
    o_ref, qt_ref, sc_ref = refs[:3]
    del refs[:3]
    if n_lat:
        s_ref = refs.pop(0)
    m_ref, acc_ref = refs

    def tile_rows(i):
        return pl.ds(pl.multiple_of(i * tq, tq), tq)

    def scores(c):
        start = pl.multiple_of(c * chunk, chunk)
        return jnp.dot(k_ref[0, pl.ds(start, chunk), :], qt_ref[...],
                       preferred_element_type=F32)

    def begin_tile(i):
        _stack_heads_transposed(q_ref, qt_ref, tq, tile_rows(i))
        sc_ref[...] = jnp.dot(kc_ref[0], qt_ref[...], preferred_element_type=F32)
        if n_lat:
            s_ref[0] = scores(0)

    def update(s, vt):
        m_prev = m_ref[...]
        m_new = jnp.maximum(m_prev, jnp.max(s, axis=0, keepdims=True))
        alpha = jnp.exp2(m_prev - m_new)
        p = jnp.exp2(s - m_new)
        acc_ref[...] = alpha * acc_ref[...] + jnp.dot(vt, p.astype(BF16),
                                                     preferred_element_type=F32)
        m_ref[...] = m_new

    def run_tile(i, next_tile):
        if has_sink:
            m_ref[...] = sink_ref[0]
            den_rows = lax.broadcasted_iota(jnp.int32, acc_ref.shape, 0) >= HEAD_DIM
            acc_ref[...] = jnp.where(den_rows, 1.0, 0.0)
        else:
            m_ref[...] = jnp.full_like(m_ref, NEG)
            acc_ref[...] = jnp.zeros_like(acc_ref)
        update(sc_ref[...], vtc_ref[0, 0])
        if n_lat:
            def body(pair, carry):
                c = 2 * pair
                s_ref[1] = scores(c + 1)
                update(s_ref[0], vt_ref[0, c])
                s_ref[0] = scores(c + 2)
                update(s_ref[1], vt_ref[0, c + 1])
                return carry
            lax.fori_loop(0, n_lat // 2 - 1, body, 0, unroll=True)
            s_ref[1] = scores(n_lat - 1)
            update(s_ref[0], vt_ref[0, n_lat - 2])
        if next_tile is not None:
            begin_tile(next_tile)
        if n_lat:
            update(s_ref[1], vt_ref[0, n_lat - 1])
        _store_heads(acc_ref[0:HEAD_DIM, :] / acc_ref[HEAD_DIM:HEAD_DIM + 1, :], o_ref, tq,
                     tile_rows(i))

    begin_tile(0)
    if n_tiles == 1:
        run_tile(0, None)
    else:
        def tile_body(i, carry):
            run_tile(i, jnp.minimum(i + 1, n_tiles - 1))
            return carry
        lax.fori_loop(0, n_tiles, tile_body, 0, unroll=4 if n_tiles % 4 == 0 else 1)


def _flash_chunk(t):
    if t % FLASH_CHUNK == 0 and (t // FLASH_CHUNK) % 2 == 0:
        return FLASH_CHUNK
    return FLASH_CHUNK // 2


def _flash_attention(q, kc2, vtc, k2=None, vt=None, sink_row=None, *, tq_want=Q_TILE):
    b, t, hq = q.shape
    n_ctx = kc2.shape[1]
    n_groups = hq // GROUP_COLS
    tq = _row_tile(t, tq_want)
    n_tiles = t // tq
    cols = GQA_GROUP * tq
    has_sink = sink_row is not None
    n_lat, chunk = (vt.shape[1], vt.shape[3]) if vt is not None else (0, 0)
    assert n_lat % 2 == 0 and vtc.shape[1] == 1

    in_specs = [
        pl.BlockSpec((1, t, GROUP_COLS), lambda bi, j: (bi, 0, j)),
        pl.BlockSpec((1, n_ctx, PAIR), lambda bi, j: (bi, 0, j)),
        pl.BlockSpec((1, 1, VT_ROWS, n_ctx), lambda bi, j: (bi, 0, j, 0)),
    ]
    args = [q, kc2, vtc]
    scratch_shapes = [pltpu.VMEM((PAIR, cols), BF16), pltpu.VMEM((n_ctx, cols), F32)]
    if n_lat:
        assert k2.shape[1] == n_lat * chunk
        in_specs += [
            pl.BlockSpec((1, n_lat * chunk, PAIR), lambda bi, j: (bi, 0, j)),
            pl.BlockSpec((1, n_lat, VT_ROWS, chunk), lambda bi, j: (bi, 0, j, 0)),
        ]
        args += [k2, vt]
        scratch_shapes.append(pltpu.VMEM((2, chunk, cols), F32))
    if has_sink:
        in_specs.append(pl.BlockSpec((1, 1, cols), lambda bi, j: (j, 0, 0)))
        args.append(sink_row)
    scratch_shapes += [pltpu.VMEM((1, cols), F32), pltpu.VMEM((VT_ROWS, cols), F32)]
    blocks = (2 * _nbytes((t, GROUP_COLS), BF16)
              + _nbytes((n_ctx + n_lat * chunk, PAIR + VT_ROWS), BF16))
    scratch = (_nbytes((PAIR, cols), BF16) + _nbytes((n_ctx + 2 * chunk, cols), F32)
               + _nbytes((VT_ROWS + SUBLANES, cols), F32))
    temps = 2 * _nbytes((max(chunk, n_ctx), cols), F32)
    return pl.pallas_call(
        functools.partial(_flash_kernel, tq=tq, n_tiles=n_tiles, n_lat=n_lat, chunk=chunk,
                          has_sink=has_sink),
        grid=(b, n_groups),
        in_specs=in_specs,
        out_specs=pl.BlockSpec((1, t, GROUP_COLS), lambda bi, j: (bi, 0, j)),
        out_shape=jax.ShapeDtypeStruct((b, t, hq), BF16),
        scratch_shapes=scratch_shapes,
        compiler_params=pltpu.CompilerParams(
            dimension_semantics=("parallel", "parallel"),
            vmem_limit_bytes=_vmem_limit(blocks, scratch, temps)),
        name="flash_attention",
    )(*args)


def _window_kernel(q_ref, k_ref, vt_ref, kc_ref, vtc_ref, bias_ref, sink_ref, o_ref,
                   qt_ref, sb_ref, sc_ref, *, tq, t):
    span = tq + 2 * WINDOW
    n_tiles = t // tq

    def window_start(i):
        return pl.multiple_of(jnp.clip(i * tq - WINDOW, 0, t - span), WINDOW)

    def issue(i, slot):
        start = window_start(i)
        bias = bias_ref[(i * tq - start) // WINDOW]
        _stack_heads_transposed(q_ref, qt_ref, tq, pl.ds(pl.multiple_of(i * tq, tq), tq))
        qt = qt_ref[...]
        s_band = jnp.dot(k_ref[0, pl.ds(start, span), :], qt, preferred_element_type=F32)
        sb_ref[slot] = s_band + jnp.concatenate([bias] * GQA_GROUP, axis=1)
        sc_ref[slot] = jnp.dot(kc_ref[0], qt, preferred_element_type=F32)

    def finish(i, slot):
        s_band, s_ctx = sb_ref[slot], sc_ref[slot]
        sink = sink_ref[0]
        m = jnp.maximum(jnp.maximum(jnp.max(s_band, axis=0, keepdims=True),
                                    jnp.max(s_ctx, axis=0, keepdims=True)), sink)
        p_band = jnp.exp2(s_band - m).astype(BF16)
        p_ctx = jnp.exp2(s_ctx - m).astype(BF16)
        first = window_start(i) // BAND_CHUNK
        vt_band = jnp.concatenate([vt_ref[0, first + c] for c in range(span // BAND_CHUNK)],
                                  axis=1)
        acc = (jnp.dot(vt_band, p_band, preferred_element_type=F32)
               + jnp.dot(vtc_ref[0, 0], p_ctx, preferred_element_type=F32))
        den = acc[HEAD_DIM:HEAD_DIM + 1, :] + jnp.exp2(sink - m)
        _store_heads(acc[0:HEAD_DIM, :] / den, o_ref, tq,
                     pl.ds(pl.multiple_of(i * tq, tq), tq))

    issue(0, 0)

    def body(pair, carry):
        i = 2 * pair
        issue(i + 1, 1)
        finish(i, 0)
        issue(i + 2, 0)
        finish(i + 1, 1)
        return carry
    lax.fori_loop(0, n_tiles // 2 - 1, body, 0)
    issue(n_tiles - 1, 1)
    finish(n_tiles - 2, 0)
    finish(n_tiles - 1, 1)


def _band_bias(tq):
    span = tq + 2 * WINDOW
    key = jnp.arange(span)[None, :, None]
    qry = jnp.arange(tq)[None, None, :]
    back = (jnp.arange(3) * WINDOW)[:, None, None]
    delta = qry + back - key
    return jnp.where(jnp.abs(delta) <= WINDOW, 0.0, NEG).astype(F32)


def _window_attention(q, k2, vt, kc2, vtc, sink_row, *, tq):
    b, t, hq = q.shape
    n_ctx = kc2.shape[1]
    n_groups = hq // GROUP_COLS
    cols = GQA_GROUP * tq
    span = tq + 2 * WINDOW
    n_band = t // BAND_CHUNK
    assert t % (2 * tq) == 0 and t >= span and tq % WINDOW == 0 and vtc.shape[1] == 1
    blocks = (2 * _nbytes((t, GROUP_COLS), BF16) + _nbytes((t, PAIR + VT_ROWS), BF16)
              + _nbytes((n_ctx, PAIR + VT_ROWS), BF16) + _nbytes((3, span, tq), F32))
    scratch = _nbytes((PAIR, cols), BF16) + 2 * _nbytes((span + n_ctx, cols), F32)
    temps = 2 * _nbytes((span + n_ctx, cols), F32)
    return pl.pallas_call(
        functools.partial(_window_kernel, tq=tq, t=t),
        grid=(b, n_groups),
        in_specs=[
            pl.BlockSpec((1, t, GROUP_COLS), lambda bi, j: (bi, 0, j)),
            pl.BlockSpec((1, t, PAIR), lambda bi, j: (bi, 0, j)),
            pl.BlockSpec((1, n_band, VT_ROWS, BAND_CHUNK), lambda bi, j: (bi, 0, j, 0)),
            pl.BlockSpec((1, n_ctx, PAIR), lambda bi, j: (bi, 0, j)),
            pl.BlockSpec((1, 1, VT_ROWS, n_ctx), lambda bi, j: (bi, 0, j, 0)),
            pl.BlockSpec((3, span, tq), lambda bi, j: (0, 0, 0)),
            pl.BlockSpec((1, 1, cols), lambda bi, j: (j, 0, 0)),
        ],
        out_specs=pl.BlockSpec((1, t, GROUP_COLS), lambda bi, j: (bi, 0, j)),
        out_shape=jax.ShapeDtypeStruct((b, t, hq), BF16),
        scratch_shapes=[
            pltpu.VMEM((PAIR, cols), BF16),
            pltpu.VMEM((2, span, cols), F32),
            pltpu.VMEM((2, n_ctx, cols), F32),
        ],
        compiler_params=pltpu.CompilerParams(
            dimension_semantics=("parallel", "parallel"),
            vmem_limit_bytes=_vmem_limit(blocks, scratch, temps)),
        name="window_attention",
    )(q, k2, vt, kc2, vtc, _band_bias(tq), sink_row)


def _pool_kernel(u_ref, w_ref, scale_ref, o_ref, pad_ref, *, t):
    reach = max(POOL_WINDOWS) // 2
    assert reach <= HALO
    pos = lax.broadcasted_iota(jnp.int32, (t, POOL_GC), 0)
    zeros = jnp.zeros((HALO, POOL_GC), F32)
    pad_ref[0:HALO, :] = zeros
    pad_ref[HALO + t:2 * HALO + t, :] = zeros
    for g, win in enumerate(POOL_WINDOWS):
        half = win // 2
        u = u_ref[0, :, g * POOL_GC:(g + 1) * POOL_GC]
        pad_ref[HALO:HALO + t, :] = u
        total = pad_ref[pl.ds(HALO - half, t), :]
        for off in range(1 - half, half):
            total = total + pad_ref[pl.ds(HALO + off, t), :]
        cnt = (jnp.minimum(pos + half, t) - jnp.maximum(pos - half, 0)).astype(F32)
        d = (total / cnt - u).astype(BF16)
        y = jnp.dot(d, w_ref[g], preferred_element_type=F32)
        o_ref[0, :, g * POOL_GC:(g + 1) * POOL_GC] = (
            y * scale_ref[:, g * POOL_GC:(g + 1) * POOL_GC]).astype(BF16)


def _pool_mix(u, pool_w, pool_scale):
    b, t, ch = u.shape
    n_g = len(POOL_WINDOWS)
    assert ch == n_g * POOL_GC
    blocks = _nbytes((t, ch), F32) + _nbytes((t, ch), BF16) + _nbytes((n_g, POOL_GC, POOL_GC), BF16)
    scratch = _nbytes((t + 2 * HALO, POOL_GC), F32)
    return pl.pallas_call(
        functools.partial(_pool_kernel, t=t),
        grid=(b,),
        in_specs=[
            pl.BlockSpec((1, t, ch), lambda bi: (bi, 0, 0)),
            pl.BlockSpec((n_g, POOL_GC, POOL_GC), lambda bi: (0, 0, 0)),
            pl.BlockSpec((1, ch), lambda bi: (0, 0)),
        ],
        out_specs=pl.BlockSpec((1, t, ch), lambda bi: (bi, 0, 0)),
        out_shape=jax.ShapeDtypeStruct((b, t, ch), BF16),
        scratch_shapes=[pltpu.VMEM((t + 2 * HALO, POOL_GC), F32)],
        compiler_params=pltpu.CompilerParams(
            dimension_semantics=("parallel",),
            vmem_limit_bytes=_vmem_limit(blocks, scratch, 6 * _nbytes((t, POOL_GC), F32))),
        name="pool_mix",
    )(u, pool_w, pool_scale)


def _outproj_kernel(*refs, n_in):
    a_refs = refs[:n_in]
    w_refs = refs[n_in:2 * n_in]
    x_ref, gate_ref, g_ref, o_ref = refs[2 * n_in:]
    y = jnp.dot(a_refs[0][0], w_refs[0][...], preferred_element_type=F32)
    for a_ref, w_ref in zip(a_refs[1:], w_refs[1:]):
        y = y + jnp.dot(a_ref[0], w_ref[...], preferred_element_type=F32)
    o_ref[0] = x_ref[0] + gate_ref[0] * _rms_norm(y, g_ref[...])


def _outproj(acts, weights, x, mods, mod_row, post_g):
    b, t, d = x.shape
    tm = _row_tile(t, PROJ_ROW_TILE)
    n_in = len(acts)
    in_specs = [pl.BlockSpec((1, tm, a.shape[2]), lambda bi, i: (bi, i, 0)) for a in acts]
    in_specs += [pl.BlockSpec(w.shape, lambda bi, i: (0, 0)) for w in weights]
    in_specs += [
        pl.BlockSpec((1, tm, d), lambda bi, i: (bi, i, 0)),
        pl.BlockSpec((1, 1, d), lambda bi, i: (mod_row(bi), 0, 2)),
        pl.BlockSpec((1, d), lambda bi, i: (0, 0)),
    ]
    blocks = (sum(_nbytes((tm, a.shape[2]), BF16) for a in acts)
              + sum(_nbytes(w.shape, BF16) for w in weights) + 2 * _nbytes((tm, d), F32))
    return pl.pallas_call(
        functools.partial(_outproj_kernel, n_in=n_in),
        grid=(b, t // tm),
        in_specs=in_specs,
        out_specs=pl.BlockSpec((1, tm, d), lambda bi, i: (bi, i, 0)),
        out_shape=jax.ShapeDtypeStruct((b, t, d), F32),
        compiler_params=pltpu.CompilerParams(
            dimension_semantics=("parallel", "parallel"),
            vmem_limit_bytes=_vmem_limit(blocks, temp_bytes=3 * _nbytes((tm, d), F32))),
        name="mixer_outproj",
    )(*acts, *weights, x, mods, post_g)


def _ffn_kernel(x_ref, xp_ref, xn_ref, sh_ref, sc_ref, gate_ref, pre_g_ref, post_g_ref,
                wu_ref, wut_ref, cw_ref, cwt_ref, cb_ref, cbt_ref, wd_ref, wdt_ref, o_ref,
                h_ref, acc_ref, u_ref, ut_ref, *, tm, n_main):
    i = pl.program_id(1)
    g, sh, sc = pre_g_ref[...], sh_ref[0], sc_ref[0]
    h_prev = _pre_mod(xp_ref[0], g, sh, sc)
    h_next = _pre_mod(xn_ref[0], g, sh, sc)
    h_prev = jnp.where(i == 0, 0.0, h_prev)
    h_next = jnp.where(i == pl.num_programs(1) - 1, 0.0, h_next)
    h_ref[0:HALO, :] = h_prev.astype(BF16)
    h_ref[HALO:HALO + tm, :] = _pre_mod(x_ref[0], g, sh, sc).astype(BF16)
    h_ref[HALO + tm:2 * HALO + tm, :] = h_next.astype(BF16)

    def up(c):
        return jnp.dot(h_ref[...], wu_ref[c], preferred_element_type=F32)

    def conv_down(u, cw, cb, wd):
        w = wd.shape[0]
        uv = u[...]
        n_rows = uv.shape[0]
        conv = (pltpu.roll(uv, 1, 0)[HALO:HALO + tm] * cw[0:1, :]
                + uv[HALO:HALO + tm] * cw[1:2, :]
                + pltpu.roll(uv, n_rows - 1, 0)[HALO:HALO + tm] * cw[2:3, :]
                + cb)
        gated = jax.nn.gelu(conv[:, :w]) * conv[:, w:]
        return jnp.dot(gated.astype(BF16), wd, preferred_element_type=F32)

    def consume(slot, c):
        return conv_down(u_ref.at[slot], cw_ref[c], cb_ref[c], wd_ref[c])

    u_ref[0] = up(0)
    acc_ref[...] = jnp.zeros_like(acc_ref)

    def body(pair, carry):
        c = 2 * pair
        u_ref[1] = up(c + 1)
        acc_ref[...] += consume(0, c)
        u_ref[0] = up(c + 2)
        acc_ref[...] += consume(1, c + 1)
        return carry
    lax.fori_loop(0, (n_main - 1) // 2, body, 0, unroll=True)
    last = n_main - 1
    if n_main % 2 == 0:
        u_ref[1] = up(last)
        acc_ref[...] += consume(0, last - 1)
    ut_ref[...] = jnp.dot(h_ref[...], wut_ref[...], preferred_element_type=F32)
    acc_ref[...] += consume(last % 2, last)
    y = acc_ref[...] + conv_down(ut_ref, cwt_ref[...], cbt_ref[...], wdt_ref[...])
    o_ref[0] = x_ref[0] + gate_ref[0] * _rms_norm(y, post_g_ref[...])


def _ffn_weights(w_up, conv_w, conv_b, w_down):
    d_ff = w_down.shape[0]
    n_main = d_ff // FFN_CHUNK
    split = n_main * FFN_CHUNK

    def pair_cols(a):
        r = a.shape[0]
        gelu, gate = a[:, :d_ff], a[:, d_ff:]
        main = jnp.concatenate([gelu[:, :split].reshape(r, n_main, FFN_CHUNK),
                                gate[:, :split].reshape(r, n_main, FFN_CHUNK)], axis=2)
        tail = jnp.concatenate([gelu[:, split:], gate[:, split:]], axis=1)
        return jnp.swapaxes(main, 0, 1), tail

    wu, wut = pair_cols(w_up.astype(BF16))
    cw, cwt = pair_cols(conv_w)
    cb, cbt = pair_cols(conv_b[None])
    wd = w_down.astype(BF16)
    return (wu, wut, cw, cwt, cb, cbt, wd[:split].reshape(n_main, FFN_CHUNK, -1), wd[split:])


def _conv_ffn(x, mods, mod_row, pre_g, post_g, weights, *, tm_want=ROW_TILE):
    b, t, d = x.shape
    tm = _row_tile(t, tm_want)
    n_main = weights[0].shape[0]
    tail = weights[-1].shape[0]
    assert tm % HALO == 0 and n_main >= 2 and tail > 0
    halo_blocks = tm // HALO
    last_halo = t // HALO - 1
    rows = tm + 2 * HALO

    def resident(a):
        zeros = (0,) * a.ndim
        return pl.BlockSpec(a.shape, lambda bi, i: zeros, pipeline_mode=pl.Buffered(1))

    in_specs = [
        pl.BlockSpec((1, tm, d), lambda bi, i: (bi, i, 0)),
        pl.BlockSpec((1, HALO, d), lambda bi, i: (bi, jnp.maximum(i * halo_blocks - 1, 0), 0)),
        pl.BlockSpec((1, HALO, d),
                     lambda bi, i: (bi, jnp.minimum((i + 1) * halo_blocks, last_halo), 0)),
        pl.BlockSpec((1, 1, d), lambda bi, i: (mod_row(bi), 0, 3)),
        pl.BlockSpec((1, 1, d), lambda bi, i: (mod_row(bi), 0, 4)),
        pl.BlockSpec((1, 1, d), lambda bi, i: (mod_row(bi), 0, 5)),
        pl.BlockSpec((1, d), lambda bi, i: (0, 0)),
        pl.BlockSpec((1, d), lambda bi, i: (0, 0)),
    ] + [resident(a) for a in weights]
    weight_bytes = sum(_nbytes(a.shape, a.dtype) for a in weights)
    blocks = 2 * _nbytes((tm, d), F32) + 2 * _nbytes((HALO, d), F32)
    scratch = (_nbytes((rows, d), BF16) + _nbytes((tm, d), F32)
               + _nbytes((2, rows, 2 * FFN_CHUNK), F32) + _nbytes((rows, 2 * tail), F32)
               + weight_bytes)
    temps = 4 * _nbytes((tm, 2 * FFN_CHUNK), F32) + _nbytes((tm, d), F32)
    return pl.pallas_call(
        functools.partial(_ffn_kernel, tm=tm, n_main=n_main),
        grid=(b, t // tm),
        in_specs=in_specs,
        out_specs=pl.BlockSpec((1, tm, d), lambda bi, i: (bi, i, 0)),
        out_shape=jax.ShapeDtypeStruct((b, t, d), F32),
        scratch_shapes=[
            pltpu.VMEM((rows, d), BF16),
            pltpu.VMEM((tm, d), F32),
            pltpu.VMEM((2, rows, 2 * FFN_CHUNK), F32),
            pltpu.VMEM((rows, 2 * tail), F32),
        ],
        compiler_params=pltpu.CompilerParams(
            dimension_semantics=("parallel", "parallel"),
            vmem_limit_bytes=_vmem_limit(blocks, scratch, temps)),
        name="conv_ffn",
    )(x, x, x, mods, mods, mods, pre_g, post_g, *weights)


def _rope_tables(t):
    rows = t // GRID_W
    row = jnp.repeat(jnp.arange(rows), GRID_W).astype(F32)
    col = jnp.tile(jnp.arange(GRID_W), rows).astype(F32)
    n_freq = HEAD_DIM // 4
    freqs = ROPE_BASE ** (-jnp.arange(n_freq, dtype=F32) / n_freq)
    ang = jnp.concatenate([row[:, None] * freqs, col[:, None] * freqs], axis=-1)
    cos, sin = jnp.cos(ang), jnp.sin(ang)
    return (jnp.concatenate([cos, cos, cos, cos], axis=-1),
            jnp.concatenate([-sin, sin, -sin, sin], axis=-1))


def _dup_heads(w, n_heads):
    d = w.shape[0]
    w = w.reshape(d, n_heads, 1, HEAD_DIM)
    return jnp.broadcast_to(w, (d, n_heads, 2, HEAD_DIM)).reshape(d, n_heads * PAIR)


def _sink_row(sink, tq):
    return jnp.repeat(sink.astype(F32) * LOG2E, tq, axis=1)[:, None, :]


def kernel(x, c, ctx, c_ctx, ada_w, ada_b, mix_pre_g, mix_post_g, ffn_pre_g, ffn_post_g,
           ab_w_in, ab_w_out, pool_w, pool_scale, sink_logit,
           c_w_qkv, c_w_out, c_q_g, c_k_g,
           ffn_w_up, ffn_conv_w, ffn_conv_b, ffn_w_down):
    batch, t, d = x.shape
    n_ctx = ctx.shape[1]
    depth = ada_w.shape[0]
    pool_ch = pool_scale.shape[1]
    b_kv = sink_logit.shape[1]
    b_q = b_kv * GQA_GROUP * HEAD_DIM
    c_q = c_w_out.shape[1]
    c_kv = c_q // HEAD_DIM // GQA_GROUP
    d_ff = ffn_w_down.shape[1]

    ctx_row = batch
    n_rows = -(-(batch + 1) // SUBLANES) * SUBLANES
    cc = jnp.zeros((n_rows, d), F32).at[:batch].set(c).at[ctx_row].set(c_ctx)
    mods_all = _ada_params(cc, ada_w, ada_b).reshape(depth, n_rows, 1, N_MOD * d)

    rope_tabs = _rope_tables(t)
    ones_bd = jnp.kron(jnp.ones((2, 1), F32),
                       jnp.kron(jnp.eye(2, dtype=F32), jnp.ones((HEAD_DIM, HEAD_DIM), F32))
                       ).astype(BF16)
    latent_row = lambda bi: bi
    context_row = lambda bi: ctx_row
    win_tq = _row_tile(t, Q_TILE)
    ctx_tq = _row_tile(n_ctx, Q_TILE)

    def qkv_weights(w, n_lead, n_kv_heads):
        k_end = n_lead + n_kv_heads * HEAD_DIM
        return jnp.concatenate([w[:, :n_lead], _dup_heads(w[:, n_lead:k_end], n_kv_heads),
                                w[:, k_end:]], axis=1).astype(BF16)

    xc = ctx
    for l in range(depth):
        need_ctx = l < depth - 1
        i = l // 2
        mods = mods_all[l]
        pre_g, post_g = mix_pre_g[l][None], mix_post_g[l][None]
        if l % 2 == 0:
            w_in = qkv_weights(ab_w_in[i], pool_ch + b_q, b_kv)
            widths = dict(n_u=pool_ch, n_q=b_q, n_kv=b_kv * PAIR, n_v=b_kv * HEAD_DIM)
            u, q, k2, vt = _inproj(x, mods, latent_row, pre_g, w_in, rope_tabs=rope_tabs,
                                   key_chunk=BAND_CHUNK, **widths)
            uc, qc, kc2, vtc = _inproj(xc, mods, context_row, pre_g, w_in, **widths)
            w_out = ab_w_out[i].astype(BF16)
            w_outs = [w_out[:pool_ch], w_out[pool_ch:]]
            pw, ps = pool_w[i].astype(BF16), pool_scale[i][None]
            a = _window_attention(q, k2, vt, kc2, vtc, _sink_row(sink_logit[i], win_tq),
                                  tq=win_tq)
            x = _outproj([_pool_mix(u, pw, ps), a], w_outs, x, mods, latent_row, post_g)
            if need_ctx:
                ac = _flash_attention(qc, kc2, vtc, sink_row=_sink_row(sink_logit[i], ctx_tq))
                xc = _outproj([_pool_mix(uc, pw, ps), ac], w_outs, xc, mods, context_row, post_g)
        else:
            w_in = qkv_weights(c_w_qkv[i], c_q, c_kv)
            gains = (jnp.tile(c_q_g[i], 2)[None], jnp.tile(c_k_g[i], 2)[None], ones_bd)
            widths = dict(n_u=0, n_q=c_q, n_kv=c_kv * PAIR, n_v=c_kv * HEAD_DIM)
            q, k2, vt = _inproj(x, mods, latent_row, pre_g, w_in, rope_tabs=rope_tabs,
                                qk_gains=gains, key_chunk=_flash_chunk(t), **widths)
            qc, kc2, vtc = _inproj(xc, mods, context_row, pre_g, w_in, qk_gains=gains, **widths)
            w_outs = [c_w_out[i].astype(BF16)]
            a = _flash_attention(q, kc2, vtc, k2, vt)
            x = _outproj([a], w_outs, x, mods, latent_row, post_g)
            if need_ctx:
                ac = _flash_attention(qc, kc2, vtc)
                xc = _outproj([ac], w_outs, xc, mods, context_row, post_g)

        ffn_args = (ffn_pre_g[l][None], ffn_post_g[l][None],
                    _ffn_weights(ffn_w_up[l], ffn_conv_w[l], ffn_conv_b[l], ffn_w_down[l]))
        x = _conv_ffn(x, mods, latent_row, *ffn_args)
        if need_ctx:
            xc = _conv_ffn(xc, mods, context_row, *ffn_args)
    return x
```

```python
import functools

import jax
import jax.numpy as jnp
from jax import lax
from jax.experimental import pallas as pl
from jax.experimental.pallas import tpu as pltpu

F32 = jnp.float32
BF16 = jnp.bfloat16

HEAD_DIM = 64
GQA_GROUP = 4
POOL_WINDOWS = (2, 4, 8, 16)
POOL_GC = 128
WINDOW = 128
GRID_W = 64
ROPE_BASE = 10000.0
EPS = 1e-6
NEG = -1e30
N_MOD = 6

LANES = 128
SUBLANES = 8
VMEM_LIMIT_CAP = 56 * 1024 * 1024
VMEM_LIMIT_FLOOR = 32 * 1024 * 1024
VMEM_MARGIN_DIV = 4
ROW_TILE = 512
PROJ_ROW_TILE = 1024
Q_TILE = 256

PAIR = 2 * HEAD_DIM
GROUP_COLS = GQA_GROUP * HEAD_DIM
HALO = SUBLANES
KEY_CHUNK = 256
FLASH_CHUNK = 512
BAND_CHUNK = WINDOW
LOG2E = 1.4426950408889634
Q_SCALE = HEAD_DIM ** -0.5 * LOG2E
ONES_ROWS = 16
VT_ROWS = HEAD_DIM + ONES_ROWS
FFN_CHUNK = 256


def _vmem_limit(block_bytes, scratch_bytes=0, temp_bytes=0):
    need = 2 * block_bytes + scratch_bytes + temp_bytes
    return int(min(max(need + need // VMEM_MARGIN_DIV, VMEM_LIMIT_FLOOR), VMEM_LIMIT_CAP))


def _nbytes(shape, dtype):
    n = 1
    for s in shape:
        n *= s
    return n * jnp.dtype(dtype).itemsize


def _row_tile(t, want):
    tile = min(t, want)
    assert t % tile == 0
    return tile


def _ada_kernel(c_ref, w_ref, b_ref, o_ref):
    c = c_ref[...]
    a = (c * jax.nn.sigmoid(c)).astype(BF16)
    o_ref[0] = jnp.dot(a, w_ref[0].astype(BF16), preferred_element_type=F32) + b_ref[0]


def _ada_params(cc, ada_w, ada_b):
    depth, d, n = ada_w.shape
    rows = cc.shape[0]
    tn = n // N_MOD
    blocks = _nbytes((rows, d), F32) + _nbytes((d, tn), F32) + _nbytes((rows, tn), F32)
    return pl.pallas_call(
        _ada_kernel,
        grid=(depth, n // tn),
        in_specs=[
            pl.BlockSpec((rows, d), lambda l, j: (0, 0)),
            pl.BlockSpec((1, d, tn), lambda l, j: (l, 0, j)),
            pl.BlockSpec((1, 1, tn), lambda l, j: (l, 0, j)),
        ],
        out_specs=pl.BlockSpec((1, rows, tn), lambda l, j: (l, 0, j)),
        out_shape=jax.ShapeDtypeStruct((depth, rows, n), F32),
        compiler_params=pltpu.CompilerParams(
            dimension_semantics=("parallel", "parallel"),
            vmem_limit_bytes=_vmem_limit(blocks, temp_bytes=_nbytes((d, tn), BF16))),
        name="ada_params",
    )(cc, ada_w, ada_b.reshape(depth, 1, n))


def _rms_norm(x, g):
    return x * lax.rsqrt(jnp.mean(x * x, axis=-1, keepdims=True) + EPS) * g


def _pre_mod(x, g, shift, scale):
    return _rms_norm(x, g) * (1.0 + scale) + shift


def _inproj_kernel(*refs, n_u, n_q, n_kv, n_v, key_chunk, rope, qk_norm, tm):
    x_ref, sh_ref, sc_ref, g_ref, w_ref = refs[:5]
    pos = 5
    if rope:
        cos_ref, sin_ref = refs[pos:pos + 2]
        pos += 2
    if qk_norm:
        qg_ref, kg_ref, ones_ref = refs[pos:pos + 3]
        pos += 3
    out_refs = refs[pos:]

    h = _pre_mod(x_ref[0], g_ref[...], sh_ref[0], sc_ref[0]).astype(BF16)

    if rope:
        lane = lax.broadcasted_iota(jnp.int32, (tm, PAIR), 1)
        first_half = (lane & (HEAD_DIM - 1)) < HEAD_DIM // 2
        cos = cos_ref[...]
        sin = sin_ref[...]

    def head_epilogue(a, gain_ref, scale):
        if qk_norm:
            a2 = a * a
            hi = a2.astype(BF16)
            lo = (a2 - hi.astype(F32)).astype(BF16)
            ss = jnp.dot(jnp.concatenate([hi, lo], axis=1), ones_ref[...],
                         preferred_element_type=F32)
            a = a * lax.rsqrt(ss * (1.0 / HEAD_DIM) + EPS) * gain_ref[...]
        if rope:
            partner = jnp.where(first_half,
                                pltpu.roll(a, PAIR - HEAD_DIM // 2, 1),
                                pltpu.roll(a, HEAD_DIM // 2, 1))
            a = a * cos + partner * sin
        if scale != 1.0:
            a = a * scale
        return a.astype(BF16)

    col = 0
    out_idx = 0
    if n_u:
        out_refs[out_idx][0] = jnp.dot(h, w_ref[:, col:col + n_u], preferred_element_type=F32)
        col += n_u
        out_idx += 1

    acc = jnp.dot(h, w_ref[:, col:col + n_q], preferred_element_type=F32)
    for c in range(n_q // PAIR):
        out_refs[out_idx][0, :, c * PAIR:(c + 1) * PAIR] = head_epilogue(
            acc[:, c * PAIR:(c + 1) * PAIR], qg_ref if qk_norm else None, Q_SCALE)
    col += n_q
    out_idx += 1

    acc = jnp.dot(h, w_ref[:, col:col + n_kv], preferred_element_type=F32)
    for c in range(n_kv // PAIR):
        out_refs[out_idx][0, :, c * PAIR:(c + 1) * PAIR] = head_epilogue(
            acc[:, c * PAIR:(c + 1) * PAIR], kg_ref if qk_norm else None, 1.0)
    col += n_kv
    out_idx += 1

    acc = jnp.dot(h, w_ref[:, col:col + n_v], preferred_element_type=F32)
    ones = jnp.ones((ONES_ROWS, key_chunk), F32)
    for cc in range(tm // key_chunk):
        at = acc[cc * key_chunk:(cc + 1) * key_chunk, :].T
        pieces = []
        for j in range(n_v // HEAD_DIM):
            pieces += [at[j * HEAD_DIM:(j + 1) * HEAD_DIM], ones]
        out_refs[out_idx][0, cc] = jnp.concatenate(pieces, axis=0).astype(BF16)


def _inproj(x, mods, mod_row, pre_g, w, *, n_u, n_q, n_kv, n_v, key_chunk=KEY_CHUNK,
            rope_tabs=None, qk_gains=None):
    b, t, d = x.shape
    n = w.shape[1]
    tm = _row_tile(t, PROJ_ROW_TILE)
    rope = rope_tabs is not None
    qk_norm = qk_gains is not None
    vt_rows = n_v // HEAD_DIM * VT_ROWS
    assert tm % key_chunk == 0

    in_specs = [
        pl.BlockSpec((1, tm, d), lambda bi, i: (bi, i, 0)),
        pl.BlockSpec((1, 1, d), lambda bi, i: (mod_row(bi), 0, 0)),
        pl.BlockSpec((1, 1, d), lambda bi, i: (mod_row(bi), 0, 1)),
        pl.BlockSpec((1, d), lambda bi, i: (0, 0)),
        pl.BlockSpec((d, n), lambda bi, i: (0, 0)),
    ]
    args = [x, mods, mods, pre_g, w]
    if rope:
        in_specs += [pl.BlockSpec((tm, PAIR), lambda bi, i: (i, 0))] * 2
        args += list(rope_tabs)
    if qk_norm:
        in_specs += [pl.BlockSpec((1, PAIR), lambda bi, i: (0, 0))] * 2
        in_specs += [pl.BlockSpec((2 * PAIR, PAIR), lambda bi, i: (0, 0))]
        args += list(qk_gains)

    out_shape, out_specs = [], []
    for width, dtype in ((n_u, F32), (n_q, BF16), (n_kv, BF16)):
        if width:
            out_shape.append(jax.ShapeDtypeStruct((b, t, width), dtype))
            out_specs.append(pl.BlockSpec((1, tm, width), lambda bi, i: (bi, i, 0)))
    out_shape.append(jax.ShapeDtypeStruct((b, t // key_chunk, vt_rows, key_chunk), BF16))
    out_specs.append(pl.BlockSpec((1, tm // key_chunk, vt_rows, key_chunk),
                                  lambda bi, i: (bi, i, 0, 0)))

    blocks = (_nbytes((tm, d), F32) + _nbytes((d, n), BF16) + _nbytes((tm, n_u), F32)
              + _nbytes((tm, n_q + n_kv + 2 * n_v), BF16) + 2 * _nbytes((tm, PAIR), F32))
    temps = _nbytes((tm, d), F32) * 2 + _nbytes((tm, max(n_q, n_kv, n_u)), F32) * 2
    return pl.pallas_call(
        functools.partial(_inproj_kernel, n_u=n_u, n_q=n_q, n_kv=n_kv, n_v=n_v,
                          key_chunk=key_chunk, rope=rope, qk_norm=qk_norm, tm=tm),
        grid=(b, t // tm),
        in_specs=in_specs,
        out_specs=out_specs,
        out_shape=out_shape,
        compiler_params=pltpu.CompilerParams(
            dimension_semantics=("parallel", "parallel"),
            vmem_limit_bytes=_vmem_limit(blocks, temp_bytes=temps)),
        name="mixer_inproj",
    )(*args)


def _stack_heads_transposed(q_ref, qt_ref, tq, rows=slice(None)):
    sub = lax.broadcasted_iota(jnp.int32, (PAIR, tq), 0)
    top = sub < HEAD_DIM
    for p in range(GROUP_COLS // PAIR):
        pair_t = q_ref[0, rows, p * PAIR:(p + 1) * PAIR].astype(F32).T
        qt_ref[:, (2 * p) * tq:(2 * p + 1) * tq] = jnp.where(top, pair_t, 0.0).astype(BF16)
        qt_ref[:, (2 * p + 1) * tq:(2 * p + 2) * tq] = jnp.where(top, 0.0, pair_t).astype(BF16)


def _store_heads(o_t, o_ref, tq, rows=slice(None)):
    for p in range(GROUP_COLS // PAIR):
        pair_t = jnp.concatenate([o_t[:, (2 * p) * tq:(2 * p + 1) * tq],
                                  o_t[:, (2 * p + 1) * tq:(2 * p + 2) * tq]], axis=0)
        o_ref[0, rows, p * PAIR:(p + 1) * PAIR] = pair_t.T.astype(o_ref.dtype)


def _flash_kernel(*refs, tq, n_tiles, n_lat, chunk, has_sink):
    refs = list(refs)
    q_ref, kc_ref, vtc_ref = refs[:3]
    del refs[:3]
    if n_lat:
        k_ref, vt_ref = refs[:2]
        del refs[:2]
    if has_sink:
        sink_ref = refs.pop(0)
    o_ref, qt_ref, sc_ref = refs[:3]
    del refs[:3]
    if n_lat:
        s_ref = refs.pop(0)
    m_ref, acc_ref = refs

    def tile_rows(i):
        return pl.ds(pl.multiple_of(i * tq, tq), tq)

    def scores(c):
        start = pl.multiple_of(c * chunk, chunk)
        return jnp.dot(k_ref[0, pl.ds(start, chunk), :], qt_ref[...],
                       preferred_element_type=F32)

    def begin_tile(i):
        _stack_heads_transposed(q_ref, qt_ref, tq, tile_rows(i))
        sc_ref[...] = jnp.dot(kc_ref[0], qt_ref[...], preferred_element_type=F32)
        if n_lat:
            s_ref[0] = scores(0)

    def update(s, vt):
        m_prev = m_ref[...]
        m_new = jnp.maximum(m_prev, jnp.max(s, axis=0, keepdims=True))
        alpha = jnp.exp2(m_prev - m_new)
        p = jnp.exp2(s - m_new)
        acc_ref[...] = alpha * acc_ref[...] + jnp.dot(vt, p.astype(BF16),
                                                     preferred_element_type=F32)
        m_ref[...] = m_new

    def run_tile(i, next_tile):
        if has_sink:
            m_ref[...] = sink_ref[0]
            den_rows = lax.broadcasted_iota(jnp.int32, acc_ref.shape, 0) >= HEAD_DIM
            acc_ref[...] = jnp.where(den_rows, 1.0, 0.0)
        else:
            m_ref[...] = jnp.full_like(m_ref, NEG)
            acc_ref[...] = jnp.zeros_like(acc_ref)
        update(sc_ref[...], vtc_ref[0, 0])
        if n_lat:
            def body(pair, carry):
                c = 2 * pair
                s_ref[1] = scores(c + 1)
                update(s_ref[0], vt_ref[0, c])
                s_ref[0] = scores(c + 2)
                update(s_ref[1], vt_ref[0, c + 1])
                return carry
            lax.fori_loop(0, n_lat // 2 - 1, body, 0, unroll=True)
            s_ref[1] = scores(n_lat - 1)
            update(s_ref[0], vt_ref[0, n_lat - 2])
        if next_tile is not None:
            begin_tile(next_tile)
        if n_lat:
            update(s_ref[1], vt_ref[0, n_lat - 1])
        _store_heads(acc_ref[0:HEAD_DIM, :] / acc_ref[HEAD_DIM:HEAD_DIM + 1, :], o_ref, tq,
                     tile_rows(i))

    begin_tile(0)
    if n_tiles == 1:
        run_tile(0, None)
    else:
        def tile_body(i, carry):
            run_tile(i, jnp.minimum(i + 1, n_tiles - 1))
            return carry
        lax.fori_loop(0, n_tiles, tile_body, 0, unroll=4 if n_tiles % 4 == 0 else 1)


def _flash_chunk(t):
    if t % FLASH_CHUNK == 0 and (t // FLASH_CHUNK) % 2 == 0:
        return FLASH_CHUNK
    return FLASH_CHUNK // 2


def _flash_attention(q, kc2, vtc, k2=None, vt=None, sink_row=None, *, tq_want=Q_TILE):
    b, t, hq = q.shape
    n_ctx = kc2.shape[1]
    n_groups = hq // GROUP_COLS
    tq = _row_tile(t, tq_want)
    n_tiles = t // tq
    cols = GQA_GROUP * tq
    has_sink = sink_row is not None
    n_lat, chunk = (vt.shape[1], vt.shape[3]) if vt is not None else (0, 0)
    assert n_lat % 2 == 0 and vtc.shape[1] == 1

    in_specs = [
        pl.BlockSpec((1, t, GROUP_COLS), lambda bi, j: (bi, 0, j)),
        pl.BlockSpec((1, n_ctx, PAIR), lambda bi, j: (bi, 0, j)),
        pl.BlockSpec((1, 1, VT_ROWS, n_ctx), lambda bi, j: (bi, 0, j, 0)),
    ]
    args = [q, kc2, vtc]
    scratch_shapes = [pltpu.VMEM((PAIR, cols), BF16), pltpu.VMEM((n_ctx, cols), F32)]
    if n_lat:
        assert k2.shape[1] == n_lat * chunk
        in_specs += [
            pl.BlockSpec((1, n_lat * chunk, PAIR), lambda bi, j: (bi, 0, j)),
            pl.BlockSpec((1, n_lat, VT_ROWS, chunk), lambda bi, j: (bi, 0, j, 0)),
        ]
        args += [k2, vt]
        scratch_shapes.append(pltpu.VMEM((2, chunk, cols), F32))
    if has_sink:
        in_specs.append(pl.BlockSpec((1, 1, cols), lambda bi, j: (j, 0, 0)))
        args.append(sink_row)
    scratch_shapes += [pltpu.VMEM((1, cols), F32), pltpu.VMEM((VT_ROWS, cols), F32)]
    blocks = (2 * _nbytes((t, GROUP_COLS), BF16)
              + _nbytes((n_ctx + n_lat * chunk, PAIR + VT_ROWS), BF16))
    scratch = (_nbytes((PAIR, cols), BF16) + _nbytes((n_ctx + 2 * chunk, cols), F32)
               + _nbytes((VT_ROWS + SUBLANES, cols), F32))
    temps = 2 * _nbytes((max(chunk, n_ctx), cols), F32)
    return pl.pallas_call(
        functools.partial(_flash_kernel, tq=tq, n_tiles=n_tiles, n_lat=n_lat, chunk=chunk,
                          has_sink=has_sink),
        grid=(b, n_groups),
        in_specs=in_specs,
        out_specs=pl.BlockSpec((1, t, GROUP_COLS), lambda bi, j: (bi, 0, j)),
        out_shape=jax.ShapeDtypeStruct((b, t, hq), BF16),
        scratch_shapes=scratch_shapes,
        compiler_params=pltpu.CompilerParams(
            dimension_semantics=("parallel", "parallel"),
            vmem_limit_bytes=_vmem_limit(blocks, scratch, temps)),
        name="flash_attention",
    )(*args)


def _window_kernel(q_ref, k_ref, vt_ref, kc_ref, vtc_ref, bias_ref, sink_ref, o_ref,
                   qt_ref, sb_ref, sc_ref, *, tq, t):
    span = tq + 2 * WINDOW
    n_tiles = t // tq

    def window_start(i):
        return pl.multiple_of(jnp.clip(i * tq - WINDOW, 0, t - span), WINDOW)

    def issue(i, slot):
        start = window_start(i)
        bias = bias_ref[(i * tq - start) // WINDOW]
        _stack_heads_transposed(q_ref, qt_ref, tq, pl.ds(pl.multiple_of(i * tq, tq), tq))
        qt = qt_ref[...]
        s_band = jnp.dot(k_ref[0, pl.ds(start, span), :], qt, preferred_element_type=F32)
        sb_ref[slot] = s_band + jnp.concatenate([bias] * GQA_GROUP, axis=1)
        sc_ref[slot] = jnp.dot(kc_ref[0], qt, preferred_element_type=F32)

    def finish(i, slot):
        s_band, s_ctx = sb_ref[slot], sc_ref[slot]
        sink = sink_ref[0]
        m = jnp.maximum(jnp.maximum(jnp.max(s_band, axis=0, keepdims=True),
                                    jnp.max(s_ctx, axis=0, keepdims=True)), sink)
        p_band = jnp.exp2(s_band - m).astype(BF16)
        p_ctx = jnp.exp2(s_ctx - m).astype(BF16)
        first = window_start(i) // BAND_CHUNK
        vt_band = jnp.concatenate([vt_ref[0, first + c] for c in range(span // BAND_CHUNK)],
                                  axis=1)
        acc = (jnp.dot(vt_band, p_band, preferred_element_type=F32)
               + jnp.dot(vtc_ref[0, 0], p_ctx, preferred_element_type=F32))
        den = acc[HEAD_DIM:HEAD_DIM + 1, :] + jnp.exp2(sink - m)
        _store_heads(acc[0:HEAD_DIM, :] / den, o_ref, tq,
                     pl.ds(pl.multiple_of(i * tq, tq), tq))

    issue(0, 0)

    def body(pair, carry):
        i = 2 * pair
        issue(i + 1, 1)
        finish(i, 0)
        issue(i + 2, 0)
        finish(i + 1, 1)
        return carry
    lax.fori_loop(0, n_tiles // 2 - 1, body, 0, unroll=2)
    issue(n_tiles - 1, 1)
    finish(n_tiles - 2, 0)
    finish(n_tiles - 1, 1)


def _band_bias(tq):
    span = tq + 2 * WINDOW
    key = jnp.arange(span)[None, :, None]
    qry = jnp.arange(tq)[None, None, :]
    back = (jnp.arange(3) * WINDOW)[:, None, None]
    delta = qry + back - key
    return jnp.where(jnp.abs(delta) <= WINDOW, 0.0, NEG).astype(F32)


def _window_attention(q, k2, vt, kc2, vtc, sink_row, *, tq):
    b, t, hq = q.shape
    n_ctx = kc2.shape[1]
    n_groups = hq // GROUP_COLS
    cols = GQA_GROUP * tq
    span = tq + 2 * WINDOW
    n_band = t // BAND_CHUNK
    assert t % (2 * tq) == 0 and t >= span and tq % WINDOW == 0 and vtc.shape[1] == 1
    blocks = (2 * _nbytes((t, GROUP_COLS), BF16) + _nbytes((t, PAIR + VT_ROWS), BF16)
              + _nbytes((n_ctx, PAIR + VT_ROWS), BF16) + _nbytes((3, span, tq), F32))
    scratch = _nbytes((PAIR, cols), BF16) + 2 * _nbytes((span + n_ctx, cols), F32)
    temps = 2 * _nbytes((span + n_ctx, cols), F32)
    return pl.pallas_call(
        functools.partial(_window_kernel, tq=tq, t=t),
        grid=(b, n_groups),
        in_specs=[
            pl.BlockSpec((1, t, GROUP_COLS), lambda bi, j: (bi, 0, j)),
            pl.BlockSpec((1, t, PAIR), lambda bi, j: (bi, 0, j)),
            pl.BlockSpec((1, n_band, VT_ROWS, BAND_CHUNK), lambda bi, j: (bi, 0, j, 0)),
            pl.BlockSpec((1, n_ctx, PAIR), lambda bi, j: (bi, 0, j)),
            pl.BlockSpec((1, 1, VT_ROWS, n_ctx), lambda bi, j: (bi, 0, j, 0)),
            pl.BlockSpec((3, span, tq), lambda bi, j: (0, 0, 0)),
            pl.BlockSpec((1, 1, cols), lambda bi, j: (j, 0, 0)),
        ],
        out_specs=pl.BlockSpec((1, t, GROUP_COLS), lambda bi, j: (bi, 0, j)),
        out_shape=jax.ShapeDtypeStruct((b, t, hq), BF16),
        scratch_shapes=[
            pltpu.VMEM((PAIR, cols), BF16),
            pltpu.VMEM((2, span, cols), F32),
            pltpu.VMEM((2, n_ctx, cols), F32),
        ],
        compiler_params=pltpu.CompilerParams(
            dimension_semantics=("parallel", "parallel"),
            vmem_limit_bytes=_vmem_limit(blocks, scratch, temps)),
        name="window_attention",
    )(q, k2, vt, kc2, vtc, _band_bias(tq), sink_row)


def _pool_kernel(u_ref, w_ref, scale_ref, o_ref, pad_ref, *, t):
    reach = max(POOL_WINDOWS) // 2
    assert reach <= HALO
    pos = lax.broadcasted_iota(jnp.int32, (t, POOL_GC), 0)
    zeros = jnp.zeros((HALO, POOL_GC), F32)
    pad_ref[0:HALO, :] = zeros
    pad_ref[HALO + t:2 * HALO + t, :] = zeros
    for g, win in enumerate(POOL_WINDOWS):
        half = win // 2
        u = u_ref[0, :, g * POOL_GC:(g + 1) * POOL_GC]
        pad_ref[HALO:HALO + t, :] = u
        total = pad_ref[pl.ds(HALO - half, t), :]
        for off in range(1 - half, half):
            total = total + pad_ref[pl.ds(HALO + off, t), :]
        cnt = (jnp.minimum(pos + half, t) - jnp.maximum(pos - half, 0)).astype(F32)
        d = (total / cnt - u).astype(BF16)
        y = jnp.dot(d, w_ref[g], preferred_element_type=F32)
        o_ref[0, :, g * POOL_GC:(g + 1) * POOL_GC] = (
            y * scale_ref[:, g * POOL_GC:(g + 1) * POOL_GC]).astype(BF16)


def _pool_mix(u, pool_w, pool_scale):
    b, t, ch = u.shape
    n_g = len(POOL_WINDOWS)
    assert ch == n_g * POOL_GC
    blocks = _nbytes((t, ch), F32) + _nbytes((t, ch), BF16) + _nbytes((n_g, POOL_GC, POOL_GC), BF16)
    scratch = _nbytes((t + 2 * HALO, POOL_GC), F32)
    return pl.pallas_call(
        functools.partial(_pool_kernel, t=t),
        grid=(b,),
        in_specs=[
            pl.BlockSpec((1, t, ch), lambda bi: (bi, 0, 0)),
            pl.BlockSpec((n_g, POOL_GC, POOL_GC), lambda bi: (0, 0, 0)),
            pl.BlockSpec((1, ch), lambda bi: (0, 0)),
        ],
        out_specs=pl.BlockSpec((1, t, ch), lambda bi: (bi, 0, 0)),
        out_shape=jax.ShapeDtypeStruct((b, t, ch), BF16),
        scratch_shapes=[pltpu.VMEM((t + 2 * HALO, POOL_GC), F32)],
        compiler_params=pltpu.CompilerParams(
            dimension_semantics=("parallel",),
            vmem_limit_bytes=_vmem_limit(blocks, scratch, 6 * _nbytes((t, POOL_GC), F32))),
        name="pool_mix",
    )(u, pool_w, pool_scale)


def _outproj_kernel(*refs, n_in):
    a_refs = refs[:n_in]
    w_refs = refs[n_in:2 * n_in]
    x_ref, gate_ref, g_ref, o_ref = refs[2 * n_in:]
    y = jnp.dot(a_refs[0][0], w_refs[0][...], preferred_element_type=F32)
    for a_ref, w_ref in zip(a_refs[1:], w_refs[1:]):
        y = y + jnp.dot(a_ref[0], w_ref[...], preferred_element_type=F32)
    o_ref[0] = x_ref[0] + gate_ref[0] * _rms_norm(y, g_ref[...])


def _outproj(acts, weights, x, mods, mod_row, post_g):
    b, t, d = x.shape
    tm = _row_tile(t, PROJ_ROW_TILE)
    n_in = len(acts)
    in_specs = [pl.BlockSpec((1, tm, a.shape[2]), lambda bi, i: (bi, i, 0)) for a in acts]
    in_specs += [pl.BlockSpec(w.shape, lambda bi, i: (0, 0)) for w in weights]
    in_specs += [
        pl.BlockSpec((1, tm, d), lambda bi, i: (bi, i, 0)),
        pl.BlockSpec((1, 1, d), lambda bi, i: (mod_row(bi), 0, 2)),
        pl.BlockSpec((1, d), lambda bi, i: (0, 0)),
    ]
    blocks = (sum(_nbytes((tm, a.shape[2]), BF16) for a in acts)
              + sum(_nbytes(w.shape, BF16) for w in weights) + 2 * _nbytes((tm, d), F32))
    return pl.pallas_call(
        functools.partial(_outproj_kernel, n_in=n_in),
        grid=(b, t // tm),
        in_specs=in_specs,
        out_specs=pl.BlockSpec((1, tm, d), lambda bi, i: (bi, i, 0)),
        out_shape=jax.ShapeDtypeStruct((b, t, d), F32),
        compiler_params=pltpu.CompilerParams(
            dimension_semantics=("parallel", "parallel"),
            vmem_limit_bytes=_vmem_limit(blocks, temp_bytes=3 * _nbytes((tm, d), F32))),
        name="mixer_outproj",
    )(*acts, *weights, x, mods, post_g)


def _ffn_kernel(x_ref, xp_ref, xn_ref, sh_ref, sc_ref, gate_ref, pre_g_ref, post_g_ref,
                wu_ref, wut_ref, cw_ref, cwt_ref, cb_ref, cbt_ref, wd_ref, wdt_ref, o_ref,
                h_ref, acc_ref, u_ref, ut_ref, *, tm, n_main):
    i = pl.program_id(1)
    g, sh, sc = pre_g_ref[...], sh_ref[0], sc_ref[0]
    h_prev = _pre_mod(xp_ref[0], g, sh, sc)
    h_next = _pre_mod(xn_ref[0], g, sh, sc)
    h_prev = jnp.where(i == 0, 0.0, h_prev)
    h_next = jnp.where(i == pl.num_programs(1) - 1, 0.0, h_next)
    h_ref[0:HALO, :] = h_prev.astype(BF16)
    h_ref[HALO:HALO + tm, :] = _pre_mod(x_ref[0], g, sh, sc).astype(BF16)
    h_ref[HALO + tm:2 * HALO + tm, :] = h_next.astype(BF16)

    def up(c):
        return jnp.dot(h_ref[...], wu_ref[c], preferred_element_type=F32)

    def conv_down(u, cw, cb, wd):
        w = wd.shape[0]
        uv = u[...]
        n_rows = uv.shape[0]
        conv = (pltpu.roll(uv, 1, 0)[HALO:HALO + tm] * cw[0:1, :]
                + uv[HALO:HALO + tm] * cw[1:2, :]
                + pltpu.roll(uv, n_rows - 1, 0)[HALO:HALO + tm] * cw[2:3, :]
                + cb)
        gated = jax.nn.gelu(conv[:, :w]) * conv[:, w:]
        return jnp.dot(gated.astype(BF16), wd, preferred_element_type=F32)

    def consume(slot, c):
        return conv_down(u_ref.at[slot], cw_ref[c], cb_ref[c], wd_ref[c])

    u_ref[0] = up(0)
    acc_ref[...] = jnp.zeros_like(acc_ref)

    def body(pair, carry):
        c = 2 * pair
        u_ref[1] = up(c + 1)
        acc_ref[...] += consume(0, c)
        u_ref[0] = up(c + 2)
        acc_ref[...] += consume(1, c + 1)
        return carry
    lax.fori_loop(0, (n_main - 1) // 2, body, 0, unroll=True)
    last = n_main - 1
    if n_main % 2 == 0:
        u_ref[1] = up(last)
        acc_ref[...] += consume(0, last - 1)
    ut_ref[...] = jnp.dot(h_ref[...], wut_ref[...], preferred_element_type=F32)
    acc_ref[...] += consume(last % 2, last)
    y = acc_ref[...] + conv_down(ut_ref, cwt_ref[...], cbt_ref[...], wdt_ref[...])
    o_ref[0] = x_ref[0] + gate_ref[0] * _rms_norm(y, post_g_ref[...])


def _ffn_weights(w_up, conv_w, conv_b, w_down):
    d_ff = w_down.shape[0]
    n_main = d_ff // FFN_CHUNK
    split = n_main * FFN_CHUNK

    def pair_cols(a):
        r = a.shape[0]
        gelu, gate = a[:, :d_ff], a[:, d_ff:]
        main = jnp.concatenate([gelu[:, :split].reshape(r, n_main, FFN_CHUNK),
                                gate[:, :split].reshape(r, n_main, FFN_CHUNK)], axis=2)
        tail = jnp.concatenate([gelu[:, split:], gate[:, split:]], axis=1)
        return jnp.swapaxes(main, 0, 1), tail

    wu, wut = pair_cols(w_up.astype(BF16))
    cw, cwt = pair_cols(conv_w)
    cb, cbt = pair_cols(conv_b[None])
    wd = w_down.astype(BF16)
    return (wu, wut, cw, cwt, cb, cbt, wd[:split].reshape(n_main, FFN_CHUNK, -1), wd[split:])


def _conv_ffn(x, mods, mod_row, pre_g, post_g, weights, *, tm_want=ROW_TILE):
    b, t, d = x.shape
    tm = _row_tile(t, tm_want)
    n_main = weights[0].shape[0]
    tail = weights[-1].shape[0]
    assert tm % HALO == 0 and n_main >= 2 and tail > 0
    halo_blocks = tm // HALO
    last_halo = t // HALO - 1
    rows = tm + 2 * HALO

    def resident(a):
        zeros = (0,) * a.ndim
        return pl.BlockSpec(a.shape, lambda bi, i: zeros, pipeline_mode=pl.Buffered(1))

    in_specs = [
        pl.BlockSpec((1, tm, d), lambda bi, i: (bi, i, 0)),
        pl.BlockSpec((1, HALO, d), lambda bi, i: (bi, jnp.maximum(i * halo_blocks - 1, 0), 0)),
        pl.BlockSpec((1, HALO, d),
                     lambda bi, i: (bi, jnp.minimum((i + 1) * halo_blocks, last_halo), 0)),
        pl.BlockSpec((1, 1, d), lambda bi, i: (mod_row(bi), 0, 3)),
        pl.BlockSpec((1, 1, d), lambda bi, i: (mod_row(bi), 0, 4)),
        pl.BlockSpec((1, 1, d), lambda bi, i: (mod_row(bi), 0, 5)),
        pl.BlockSpec((1, d), lambda bi, i: (0, 0)),
        pl.BlockSpec((1, d), lambda bi, i: (0, 0)),
    ] + [resident(a) for a in weights]
    weight_bytes = sum(_nbytes(a.shape, a.dtype) for a in weights)
    blocks = 2 * _nbytes((tm, d), F32) + 2 * _nbytes((HALO, d), F32)
    scratch = (_nbytes((rows, d), BF16) + _nbytes((tm, d), F32)
               + _nbytes((2, rows, 2 * FFN_CHUNK), F32) + _nbytes((rows, 2 * tail), F32)
               + weight_bytes)
    temps = 4 * _nbytes((tm, 2 * FFN_CHUNK), F32) + _nbytes((tm, d), F32)
    return pl.pallas_call(
        functools.partial(_ffn_kernel, tm=tm, n_main=n_main),
        grid=(b, t // tm),
        in_specs=in_specs,
        out_specs=pl.BlockSpec((1, tm, d), lambda bi, i: (bi, i, 0)),
        out_shape=jax.ShapeDtypeStruct((b, t, d), F32),
        scratch_shapes=[
            pltpu.VMEM((rows, d), BF16),
            pltpu.VMEM((tm, d), F32),
            pltpu.VMEM((2, rows, 2 * FFN_CHUNK), F32),
            pltpu.VMEM((rows, 2 * tail), F32),
        ],
        compiler_params=pltpu.CompilerParams(
            dimension_semantics=("parallel", "parallel"),
            vmem_limit_bytes=_vmem_limit(blocks, scratch, temps)),
        name="conv_ffn",
    )(x, x, x, mods, mods, mods, pre_g, post_g, *weights)


def _rope_tables(t):
    rows = t // GRID_W
    row = jnp.repeat(jnp.arange(rows), GRID_W).astype(F32)
    col = jnp.tile(jnp.arange(GRID_W), rows).astype(F32)
    n_freq = HEAD_DIM // 4
    freqs = ROPE_BASE ** (-jnp.arange(n_freq, dtype=F32) / n_freq)
    ang = jnp.concatenate([row[:, None] * freqs, col[:, None] * freqs], axis=-1)
    cos, sin = jnp.cos(ang), jnp.sin(ang)
    return (jnp.concatenate([cos, cos, cos, cos], axis=-1),
            jnp.concatenate([-sin, sin, -sin, sin], axis=-1))


def _dup_heads(w, n_heads):
    d = w.shape[0]
    w = w.reshape(d, n_heads, 1, HEAD_DIM)
    return jnp.broadcast_to(w, (d, n_heads, 2, HEAD_DIM)).reshape(d, n_heads * PAIR)


def _sink_row(sink, tq):
    return jnp.repeat(sink.astype(F32) * LOG2E, tq, axis=1)[:, None, :]


def kernel(x, c, ctx, c_ctx, ada_w, ada_b, mix_pre_g, mix_post_g, ffn_pre_g, ffn_post_g,
           ab_w_in, ab_w_out, pool_w, pool_scale, sink_logit,
           c_w_qkv, c_w_out, c_q_g, c_k_g,
           ffn_w_up, ffn_conv_w, ffn_conv_b, ffn_w_down):
    batch, t, d = x.shape
    n_ctx = ctx.shape[1]
    depth = ada_w.shape[0]
    pool_ch = pool_scale.shape[1]
    b_kv = sink_logit.shape[1]
    b_q = b_kv * GQA_GROUP * HEAD_DIM
    c_q = c_w_out.shape[1]
    c_kv = c_q // HEAD_DIM // GQA_GROUP
    d_ff = ffn_w_down.shape[1]

    ctx_row = batch
    n_rows = -(-(batch + 1) // SUBLANES) * SUBLANES
    cc = jnp.zeros((n_rows, d), F32).at[:batch].set(c).at[ctx_row].set(c_ctx)
    mods_all = _ada_params(cc, ada_w, ada_b).reshape(depth, n_rows, 1, N_MOD * d)

    rope_tabs = _rope_tables(t)
    ones_bd = jnp.kron(jnp.ones((2, 1), F32),
                       jnp.kron(jnp.eye(2, dtype=F32), jnp.ones((HEAD_DIM, HEAD_DIM), F32))
                       ).astype(BF16)
    latent_row = lambda bi: bi
    context_row = lambda bi: ctx_row
    win_tq = _row_tile(t, Q_TILE)
    ctx_tq = _row_tile(n_ctx, Q_TILE)

    def qkv_weights(w, n_lead, n_kv_heads):
        k_end = n_lead + n_kv_heads * HEAD_DIM
        return jnp.concatenate([w[:, :n_lead], _dup_heads(w[:, n_lead:k_end], n_kv_heads),
                                w[:, k_end:]], axis=1).astype(BF16)

    xc = ctx
    for l in range(depth):
        need_ctx = l < depth - 1
        i = l // 2
        mods = mods_all[l]
        pre_g, post_g = mix_pre_g[l][None], mix_post_g[l][None]
        if l % 2 == 0:
            w_in = qkv_weights(ab_w_in[i], pool_ch + b_q, b_kv)
            widths = dict(n_u=pool_ch, n_q=b_q, n_kv=b_kv * PAIR, n_v=b_kv * HEAD_DIM)
            u, q, k2, vt = _inproj(x, mods, latent_row, pre_g, w_in, rope_tabs=rope_tabs,
                                   key_chunk=BAND_CHUNK, **widths)
            uc, qc, kc2, vtc = _inproj(xc, mods, context_row, pre_g, w_in, **widths)
            w_out = ab_w_out[i].astype(BF16)
            w_outs = [w_out[:pool_ch], w_out[pool_ch:]]
            pw, ps = pool_w[i].astype(BF16), pool_scale[i][None]
            a = _window_attention(q, k2, vt, kc2, vtc, _sink_row(sink_logit[i], win_tq),
                                  tq=win_tq)
            x = _outproj([_pool_mix(u, pw, ps), a], w_outs, x, mods, latent_row, post_g)
            if need_ctx:
                ac = _flash_attention(qc, kc2, vtc, sink_row=_sink_row(sink_logit[i], ctx_tq))
                xc = _outproj([_pool_mix(uc, pw, ps), ac], w_outs, xc, mods, context_row, post_g)
        else:
            w_in = qkv_weights(c_w_qkv[i], c_q, c_kv)
            gains = (jnp.tile(c_q_g[i], 2)[None], jnp.tile(c_k_g[i], 2)[None], ones_bd)
            widths = dict(n_u=0, n_q=c_q, n_kv=c_kv * PAIR, n_v=c_kv * HEAD_DIM)
            q, k2, vt = _inproj(x, mods, latent_row, pre_g, w_in, rope_tabs=rope_tabs,
                                qk_gains=gains, key_chunk=_flash_chunk(t), **widths)
            qc, kc2, vtc = _inproj(xc, mods, context_row, pre_g, w_in, qk_gains=gains, **widths)
            w_outs = [c_w_out[i].astype(BF16)]
            a = _flash_attention(q, kc2, vtc, k2, vt)
            x = _outproj([a], w_outs, x, mods, latent_row, post_g)
            if need_ctx:
                ac = _flash_attention(qc, kc2, vtc)
                xc = _outproj([ac], w_outs, xc, mods, context_row, post_g)

        ffn_args = (ffn_pre_g[l][None], ffn_post_g[l][None],
                    _ffn_weights(ffn_w_up[l], ffn_conv_w[l], ffn_conv_b[l], ffn_w_down[l]))
        x = _conv_ffn(x, mods, latent_row, *ffn_args)
        if need_ctx:
            xc = _conv_ffn(xc, mods, context_row, *ffn_args)
    return x
```

```python
import functools

import jax
import jax.numpy as jnp
from jax import lax
from jax.experimental import pallas as pl
from jax.experimental.pallas import tpu as pltpu

F32 = jnp.float32
BF16 = jnp.bfloat16

HEAD_DIM = 64
GQA_GROUP = 4
POOL_WINDOWS = (2, 4, 8, 16)
POOL_GC = 128
WINDOW = 128
GRID_W = 64
ROPE_BASE = 10000.0
EPS = 1e-6
NEG = -1e30
N_MOD = 6

LANES = 128
SUBLANES = 8
VMEM_LIMIT_CAP = 56 * 1024 * 1024
VMEM_LIMIT_FLOOR = 32 * 1024 * 1024
VMEM_MARGIN_DIV = 4
ROW_TILE = 512
PROJ_ROW_TILE = 1024
STREAM_BUFFERS = 3
Q_TILE = 256

PAIR = 2 * HEAD_DIM
GROUP_COLS = GQA_GROUP * HEAD_DIM
HALO = SUBLANES
KEY_CHUNK = 256
FLASH_CHUNK = 512
BAND_CHUNK = WINDOW
LOG2E = 1.4426950408889634
Q_SCALE = HEAD_DIM ** -0.5 * LOG2E
ONES_ROWS = 16
VT_ROWS = HEAD_DIM + ONES_ROWS
FFN_CHUNK = 256


def _vmem_limit(block_bytes, scratch_bytes=0, temp_bytes=0):
    need = 2 * block_bytes + scratch_bytes + temp_bytes
    return int(min(max(need + need // VMEM_MARGIN_DIV, VMEM_LIMIT_FLOOR), VMEM_LIMIT_CAP))


def _nbytes(shape, dtype):
    n = 1
    for s in shape:
        n *= s
    return n * jnp.dtype(dtype).itemsize


def _row_tile(t, want):
    tile = min(t, want)
    assert t % tile == 0
    return tile


def _ada_kernel(c_ref, w_ref, b_ref, o_ref):
    c = c_ref[...]
    a = (c * jax.nn.sigmoid(c)).astype(BF16)
    o_ref[0] = jnp.dot(a, w_ref[0].astype(BF16), preferred_element_type=F32) + b_ref[0]


def _ada_params(cc, ada_w, ada_b):
    depth, d, n = ada_w.shape
    rows = cc.shape[0]
    tn = n // N_MOD
    blocks = _nbytes((rows, d), F32) + _nbytes((d, tn), F32) + _nbytes((rows, tn), F32)
    return pl.pallas_call(
        _ada_kernel,
        grid=(depth, n // tn),
        in_specs=[
            pl.BlockSpec((rows, d), lambda l, j: (0, 0)),
            pl.BlockSpec((1, d, tn), lambda l, j: (l, 0, j)),
            pl.BlockSpec((1, 1, tn), lambda l, j: (l, 0, j)),
        ],
        out_specs=pl.BlockSpec((1, rows, tn), lambda l, j: (l, 0, j)),
        out_shape=jax.ShapeDtypeStruct((depth, rows, n), F32),
        compiler_params=pltpu.CompilerParams(
            dimension_semantics=("parallel", "parallel"),
            vmem_limit_bytes=_vmem_limit(blocks, temp_bytes=_nbytes((d, tn), BF16))),
        name="ada_params",
    )(cc, ada_w, ada_b.reshape(depth, 1, n))


def _rms_norm(x, g):
    return x * lax.rsqrt(jnp.mean(x * x, axis=-1, keepdims=True) + EPS) * g


def _pre_mod(x, g, shift, scale):
    return _rms_norm(x, g) * (1.0 + scale) + shift


def _inproj_kernel(*refs, n_u, n_q, n_kv, n_v, key_chunk, rope, qk_norm, tm):
    x_ref, sh_ref, sc_ref, g_ref, w_ref = refs[:5]
    pos = 5
    if rope:
        cos_ref, sin_ref = refs[pos:pos + 2]
        pos += 2
    if qk_norm:
        qg_ref, kg_ref, ones_ref = refs[pos:pos + 3]
        pos += 3
    out_refs = refs[pos:]

    h = _pre_mod(x_ref[0], g_ref[...], sh_ref[0], sc_ref[0]).astype(BF16)

    if rope:
        lane = lax.broadcasted_iota(jnp.int32, (tm, PAIR), 1)
        first_half = (lane & (HEAD_DIM - 1)) < HEAD_DIM // 2
        cos = cos_ref[...]
        sin = sin_ref[...]

    def head_epilogue(a, gain_ref, scale):
        if qk_norm:
            a2 = a * a
            hi = a2.astype(BF16)
            lo = (a2 - hi.astype(F32)).astype(BF16)
            ss = jnp.dot(jnp.concatenate([hi, lo], axis=1), ones_ref[...],
                         preferred_element_type=F32)
            a = a * lax.rsqrt(ss * (1.0 / HEAD_DIM) + EPS) * gain_ref[...]
        if rope:
            partner = jnp.where(first_half,
                                pltpu.roll(a, PAIR - HEAD_DIM // 2, 1),
                                pltpu.roll(a, HEAD_DIM // 2, 1))
            a = a * cos + partner * sin
        if scale != 1.0:
            a = a * scale
        return a.astype(BF16)

    col = 0
    out_idx = 0
    if n_u:
        out_refs[out_idx][0] = jnp.dot(h, w_ref[:, col:col + n_u], preferred_element_type=F32)
        col += n_u
        out_idx += 1

    acc = jnp.dot(h, w_ref[:, col:col + n_q], preferred_element_type=F32)
    for c in range(n_q // PAIR):
        out_refs[out_idx][0, :, c * PAIR:(c + 1) * PAIR] = head_epilogue(
            acc[:, c * PAIR:(c + 1) * PAIR], qg_ref if qk_norm else None, Q_SCALE)
    col += n_q
    out_idx += 1

    acc = jnp.dot(h, w_ref[:, col:col + n_kv], preferred_element_type=F32)
    for c in range(n_kv // PAIR):
        out_refs[out_idx][0, :, c * PAIR:(c + 1) * PAIR] = head_epilogue(
            acc[:, c * PAIR:(c + 1) * PAIR], kg_ref if qk_norm else None, 1.0)
    col += n_kv
    out_idx += 1

    acc = jnp.dot(h, w_ref[:, col:col + n_v], preferred_element_type=F32)
    ones = jnp.ones((ONES_ROWS, key_chunk), F32)
    for cc in range(tm // key_chunk):
        at = acc[cc * key_chunk:(cc + 1) * key_chunk, :].T
        pieces = []
        for j in range(n_v // HEAD_DIM):
            pieces += [at[j * HEAD_DIM:(j + 1) * HEAD_DIM], ones]
        out_refs[out_idx][0, cc] = jnp.concatenate(pieces, axis=0).astype(BF16)


def _inproj(x, mods, mod_row, pre_g, w, *, n_u, n_q, n_kv, n_v, key_chunk=KEY_CHUNK,
            rope_tabs=None, qk_gains=None):
    b, t, d = x.shape
    n = w.shape[1]
    tm = _row_tile(t, PROJ_ROW_TILE)
    rope = rope_tabs is not None
    qk_norm = qk_gains is not None
    vt_rows = n_v // HEAD_DIM * VT_ROWS
    assert tm % key_chunk == 0

    in_specs = [
        pl.BlockSpec((1, tm, d), lambda bi, i: (bi, i, 0)),
        pl.BlockSpec((1, 1, d), lambda bi, i: (mod_row(bi), 0, 0)),
        pl.BlockSpec((1, 1, d), lambda bi, i: (mod_row(bi), 0, 1)),
        pl.BlockSpec((1, d), lambda bi, i: (0, 0)),
        pl.BlockSpec((d, n), lambda bi, i: (0, 0)),
    ]
    args = [x, mods, mods, pre_g, w]
    if rope:
        in_specs += [pl.BlockSpec((tm, PAIR), lambda bi, i: (i, 0))] * 2
        args += list(rope_tabs)
    if qk_norm:
        in_specs += [pl.BlockSpec((1, PAIR), lambda bi, i: (0, 0))] * 2
        in_specs += [pl.BlockSpec((2 * PAIR, PAIR), lambda bi, i: (0, 0))]
        args += list(qk_gains)

    out_shape, out_specs = [], []
    for width, dtype in ((n_u, F32), (n_q, BF16), (n_kv, BF16)):
        if width:
            out_shape.append(jax.ShapeDtypeStruct((b, t, width), dtype))
            out_specs.append(pl.BlockSpec((1, tm, width), lambda bi, i: (bi, i, 0)))
    out_shape.append(jax.ShapeDtypeStruct((b, t // key_chunk, vt_rows, key_chunk), BF16))
    out_specs.append(pl.BlockSpec((1, tm // key_chunk, vt_rows, key_chunk),
                                  lambda bi, i: (bi, i, 0, 0)))

    blocks = (_nbytes((tm, d), F32) + _nbytes((d, n), BF16) + _nbytes((tm, n_u), F32)
              + _nbytes((tm, n_q + n_kv + 2 * n_v), BF16) + 2 * _nbytes((tm, PAIR), F32))
    temps = _nbytes((tm, d), F32) * 2 + _nbytes((tm, max(n_q, n_kv, n_u)), F32) * 2
    return pl.pallas_call(
        functools.partial(_inproj_kernel, n_u=n_u, n_q=n_q, n_kv=n_kv, n_v=n_v,
                          key_chunk=key_chunk, rope=rope, qk_norm=qk_norm, tm=tm),
        grid=(b, t // tm),
        in_specs=in_specs,
        out_specs=out_specs,
        out_shape=out_shape,
        compiler_params=pltpu.CompilerParams(
            dimension_semantics=("parallel", "parallel"),
            vmem_limit_bytes=_vmem_limit(blocks, temp_bytes=temps)),
        name="mixer_inproj",
    )(*args)


def _stack_heads_transposed(q_ref, qt_ref, tq, rows=slice(None)):
    sub = lax.broadcasted_iota(jnp.int32, (PAIR, tq), 0)
    top = sub < HEAD_DIM
    for p in range(GROUP_COLS // PAIR):
        pair_t = q_ref[0, rows, p * PAIR:(p + 1) * PAIR].astype(F32).T
        qt_ref[:, (2 * p) * tq:(2 * p + 1) * tq] = jnp.where(top, pair_t, 0.0).astype(BF16)
        qt_ref[:, (2 * p + 1) * tq:(2 * p + 2) * tq] = jnp.where(top, 0.0, pair_t).astype(BF16)


def _store_heads(o_t, o_ref, tq, rows=slice(None)):
    for p in range(GROUP_COLS // PAIR):
        pair_t = jnp.concatenate([o_t[:, (2 * p) * tq:(2 * p + 1) * tq],
                                  o_t[:, (2 * p + 1) * tq:(2 * p + 2) * tq]], axis=0)
        o_ref[0, rows, p * PAIR:(p + 1) * PAIR] = pair_t.T.astype(o_ref.dtype)


def _flash_kernel(*refs, tq, n_tiles, n_lat, chunk, has_sink):
    refs = list(refs)
    q_ref, kc_ref, vtc_ref = refs[:3]
    del refs[:3]
    if n_lat:
        k_ref, vt_ref = refs[:2]
        del refs[:2]
    if has_sink:
        sink_ref = refs.pop(0)
    o_ref, qt_ref, sc_ref = refs[:3]
    del refs[:3]
    if n_lat:
        s_ref = refs.pop(0)
    m_ref, acc_ref = refs

    def tile_rows(i):
        return pl.ds(pl.multiple_of(i * tq, tq), tq)

    def scores(c):
        start = pl.multiple_of(c * chunk, chunk)
        return jnp.dot(k_ref[0, pl.ds(start, chunk), :], qt_ref[...],
                       preferred_element_type=F32)

    def begin_tile(i):
        _stack_heads_transposed(q_ref, qt_ref, tq, tile_rows(i))
        sc_ref[...] = jnp.dot(kc_ref[0], qt_ref[...], preferred_element_type=F32)
        if n_lat:
            s_ref[0] = scores(0)

    def update(s, vt):
        m_prev = m_ref[...]
        m_new = jnp.maximum(m_prev, jnp.max(s, axis=0, keepdims=True))
        alpha = jnp.exp2(m_prev - m_new)
        p = jnp.exp2(s - m_new)
        acc_ref[...] = alpha * acc_ref[...] + jnp.dot(vt, p.astype(BF16),
                                                     preferred_element_type=F32)
        m_ref[...] = m_new

    def run_tile(i, next_tile):
        if has_sink:
            m_ref[...] = sink_ref[0]
            den_rows = lax.broadcasted_iota(jnp.int32, acc_ref.shape, 0) >= HEAD_DIM
            acc_ref[...] = jnp.where(den_rows, 1.0, 0.0)
        else:
            m_ref[...] = jnp.full_like(m_ref, NEG)
            acc_ref[...] = jnp.zeros_like(acc_ref)
        update(sc_ref[...], vtc_ref[0, 0])
        if n_lat:
            def body(pair, carry):
                c = 2 * pair
                s_ref[1] = scores(c + 1)
                update(s_ref[0], vt_ref[0, c])
                s_ref[0] = scores(c + 2)
                update(s_ref[1], vt_ref[0, c + 1])
                return carry
            lax.fori_loop(0, n_lat // 2 - 1, body, 0, unroll=True)
            s_ref[1] = scores(n_lat - 1)
            update(s_ref[0], vt_ref[0, n_lat - 2])
        if next_tile is not None:
            begin_tile(next_tile)
        if n_lat:
            update(s_ref[1], vt_ref[0, n_lat - 1])
        _store_heads(acc_ref[0:HEAD_DIM, :] / acc_ref[HEAD_DIM:HEAD_DIM + 1, :], o_ref, tq,
                     tile_rows(i))

    begin_tile(0)
    if n_tiles == 1:
        run_tile(0, None)
    else:
        def tile_body(i, carry):
            run_tile(i, jnp.minimum(i + 1, n_tiles - 1))
            return carry
        lax.fori_loop(0, n_tiles, tile_body, 0, unroll=4 if n_tiles % 4 == 0 else 1)


def _flash_chunk(t):
    if t % FLASH_CHUNK == 0 and (t // FLASH_CHUNK) % 2 == 0:
        return FLASH_CHUNK
    return FLASH_CHUNK // 2


def _flash_attention(q, kc2, vtc, k2=None, vt=None, sink_row=None, *, tq_want=Q_TILE):
    b, t, hq = q.shape
    n_ctx = kc2.shape[1]
    n_groups = hq // GROUP_COLS
    tq = _row_tile(t, tq_want)
    n_tiles = t // tq
    cols = GQA_GROUP * tq
    has_sink = sink_row is not None
    n_lat, chunk = (vt.shape[1], vt.shape[3]) if vt is not None else (0, 0)
    assert n_lat % 2 == 0 and vtc.shape[1] == 1

    in_specs = [
        pl.BlockSpec((1, t, GROUP_COLS), lambda bi, j: (bi, 0, j)),
        pl.BlockSpec((1, n_ctx, PAIR), lambda bi, j: (bi, 0, j)),
        pl.BlockSpec((1, 1, VT_ROWS, n_ctx), lambda bi, j: (bi, 0, j, 0)),
    ]
    args = [q, kc2, vtc]
    scratch_shapes = [pltpu.VMEM((PAIR, cols), BF16), pltpu.VMEM((n_ctx, cols), F32)]
    if n_lat:
        assert k2.shape[1] == n_lat * chunk
        in_specs += [
            pl.BlockSpec((1, n_lat * chunk, PAIR), lambda bi, j: (bi, 0, j)),
            pl.BlockSpec((1, n_lat, VT_ROWS, chunk), lambda bi, j: (bi, 0, j, 0)),
        ]
        args += [k2, vt]
        scratch_shapes.append(pltpu.VMEM((2, chunk, cols), F32))
    if has_sink:
        in_specs.append(pl.BlockSpec((1, 1, cols), lambda bi, j: (j, 0, 0)))
        args.append(sink_row)
    scratch_shapes += [pltpu.VMEM((1, cols), F32), pltpu.VMEM((VT_ROWS, cols), F32)]
    blocks = (2 * _nbytes((t, GROUP_COLS), BF16)
              + _nbytes((n_ctx + n_lat * chunk, PAIR + VT_ROWS), BF16))
    scratch = (_nbytes((PAIR, cols), BF16) + _nbytes((n_ctx + 2 * chunk, cols), F32)
               + _nbytes((VT_ROWS + SUBLANES, cols), F32))
    temps = 2 * _nbytes((max(chunk, n_ctx), cols), F32)
    return pl.pallas_call(
        functools.partial(_flash_kernel, tq=tq, n_tiles=n_tiles, n_lat=n_lat, chunk=chunk,
                          has_sink=has_sink),
        grid=(b, n_groups),
        in_specs=in_specs,
        out_specs=pl.BlockSpec((1, t, GROUP_COLS), lambda bi, j: (bi, 0, j)),
        out_shape=jax.ShapeDtypeStruct((b, t, hq), BF16),
        scratch_shapes=scratch_shapes,
        compiler_params=pltpu.CompilerParams(
            dimension_semantics=("parallel", "parallel"),
            vmem_limit_bytes=_vmem_limit(blocks, scratch, temps)),
        name="flash_attention",
    )(*args)


def _window_kernel(q_ref, k_ref, vt_ref, kc_ref, vtc_ref, bias_ref, sink_ref, o_ref,
                   qt_ref, sb_ref, sc_ref, *, tq, t):
    span = tq + 2 * WINDOW
    n_tiles = t // tq

    def window_start(i):
        return pl.multiple_of(jnp.clip(i * tq - WINDOW, 0, t - span), WINDOW)

    def issue(i, slot):
        start = window_start(i)
        bias = bias_ref[(i * tq - start) // WINDOW]
        _stack_heads_transposed(q_ref, qt_ref, tq, pl.ds(pl.multiple_of(i * tq, tq), tq))
        qt = qt_ref[...]
        s_band = jnp.dot(k_ref[0, pl.ds(start, span), :], qt, preferred_element_type=F32)
        sb_ref[slot] = s_band + jnp.concatenate([bias] * GQA_GROUP, axis=1)
        sc_ref[slot] = jnp.dot(kc_ref[0], qt, preferred_element_type=F32)

    def finish(i, slot):
        s_band, s_ctx = sb_ref[slot], sc_ref[slot]
        sink = sink_ref[0]
        m = jnp.maximum(jnp.maximum(jnp.max(s_band, axis=0, keepdims=True),
                                    jnp.max(s_ctx, axis=0, keepdims=True)), sink)
        p_band = jnp.exp2(s_band - m).astype(BF16)
        p_ctx = jnp.exp2(s_ctx - m).astype(BF16)
        first = window_start(i) // BAND_CHUNK
        vt_band = jnp.concatenate([vt_ref[0, first + c] for c in range(span // BAND_CHUNK)],
                                  axis=1)
        acc = (jnp.dot(vt_band, p_band, preferred_element_type=F32)
               + jnp.dot(vtc_ref[0, 0], p_ctx, preferred_element_type=F32))
        den = acc[HEAD_DIM:HEAD_DIM + 1, :] + jnp.exp2(sink - m)
        _store_heads(acc[0:HEAD_DIM, :] / den, o_ref, tq,
                     pl.ds(pl.multiple_of(i * tq, tq), tq))

    issue(0, 0)

    def body(pair, carry):
        i = 2 * pair
        issue(i + 1, 1)
        finish(i, 0)
        issue(i + 2, 0)
        finish(i + 1, 1)
        return carry
    lax.fori_loop(0, n_tiles // 2 - 1, body, 0, unroll=2)
    issue(n_tiles - 1, 1)
    finish(n_tiles - 2, 0)
    finish(n_tiles - 1, 1)


def _band_bias(tq):
    span = tq + 2 * WINDOW
    key = jnp.arange(span)[None, :, None]
    qry = jnp.arange(tq)[None, None, :]
    back = (jnp.arange(3) * WINDOW)[:, None, None]
    delta = qry + back - key
    return jnp.where(jnp.abs(delta) <= WINDOW, 0.0, NEG).astype(F32)


def _window_attention(q, k2, vt, kc2, vtc, sink_row, *, tq):
    b, t, hq = q.shape
    n_ctx = kc2.shape[1]
    n_groups = hq // GROUP_COLS
    cols = GQA_GROUP * tq
    span = tq + 2 * WINDOW
    n_band = t // BAND_CHUNK
    assert t % (2 * tq) == 0 and t >= span and tq % WINDOW == 0 and vtc.shape[1] == 1
    blocks = (2 * _nbytes((t, GROUP_COLS), BF16) + _nbytes((t, PAIR + VT_ROWS), BF16)
              + _nbytes((n_ctx, PAIR + VT_ROWS), BF16) + _nbytes((3, span, tq), F32))
    scratch = _nbytes((PAIR, cols), BF16) + 2 * _nbytes((span + n_ctx, cols), F32)
    temps = 2 * _nbytes((span + n_ctx, cols), F32)
    return pl.pallas_call(
        functools.partial(_window_kernel, tq=tq, t=t),
        grid=(b, n_groups),
        in_specs=[
            pl.BlockSpec((1, t, GROUP_COLS), lambda bi, j: (bi, 0, j)),
            pl.BlockSpec((1, t, PAIR), lambda bi, j: (bi, 0, j)),
            pl.BlockSpec((1, n_band, VT_ROWS, BAND_CHUNK), lambda bi, j: (bi, 0, j, 0)),
            pl.BlockSpec((1, n_ctx, PAIR), lambda bi, j: (bi, 0, j)),
            pl.BlockSpec((1, 1, VT_ROWS, n_ctx), lambda bi, j: (bi, 0, j, 0)),
            pl.BlockSpec((3, span, tq), lambda bi, j: (0, 0, 0)),
            pl.BlockSpec((1, 1, cols), lambda bi, j: (j, 0, 0)),
        ],
        out_specs=pl.BlockSpec((1, t, GROUP_COLS), lambda bi, j: (bi, 0, j)),
        out_shape=jax.ShapeDtypeStruct((b, t, hq), BF16),
        scratch_shapes=[
            pltpu.VMEM((PAIR, cols), BF16),
            pltpu.VMEM((2, span, cols), F32),
            pltpu.VMEM((2, n_ctx, cols), F32),
        ],
        compiler_params=pltpu.CompilerParams(
            dimension_semantics=("parallel", "parallel"),
            vmem_limit_bytes=_vmem_limit(blocks, scratch, temps)),
        name="window_attention",
    )(q, k2, vt, kc2, vtc, _band_bias(tq), sink_row)


def _pool_kernel(u_ref, w_ref, scale_ref, o_ref, pad_ref, *, t):
    reach = max(POOL_WINDOWS) // 2
    assert reach <= HALO
    pos = lax.broadcasted_iota(jnp.int32, (t, POOL_GC), 0)
    zeros = jnp.zeros((HALO, POOL_GC), F32)
    pad_ref[0:HALO, :] = zeros
    pad_ref[HALO + t:2 * HALO + t, :] = zeros
    for g, win in enumerate(POOL_WINDOWS):
        half = win // 2
        u = u_ref[0, :, g * POOL_GC:(g + 1) * POOL_GC]
        pad_ref[HALO:HALO + t, :] = u
        total = pad_ref[pl.ds(HALO - half, t), :]
        for off in range(1 - half, half):
            total = total + pad_ref[pl.ds(HALO + off, t), :]
        cnt = (jnp.minimum(pos + half, t) - jnp.maximum(pos - half, 0)).astype(F32)
        d = (total / cnt - u).astype(BF16)
        y = jnp.dot(d, w_ref[g], preferred_element_type=F32)
        o_ref[0, :, g * POOL_GC:(g + 1) * POOL_GC] = (
            y * scale_ref[:, g * POOL_GC:(g + 1) * POOL_GC]).astype(BF16)


def _pool_mix(u, pool_w, pool_scale):
    b, t, ch = u.shape
    n_g = len(POOL_WINDOWS)
    assert ch == n_g * POOL_GC
    blocks = _nbytes((t, ch), F32) + _nbytes((t, ch), BF16) + _nbytes((n_g, POOL_GC, POOL_GC), BF16)
    scratch = _nbytes((t + 2 * HALO, POOL_GC), F32)
    return pl.pallas_call(
        functools.partial(_pool_kernel, t=t),
        grid=(b,),
        in_specs=[
            pl.BlockSpec((1, t, ch), lambda bi: (bi, 0, 0)),
            pl.BlockSpec((n_g, POOL_GC, POOL_GC), lambda bi: (0, 0, 0)),
            pl.BlockSpec((1, ch), lambda bi: (0, 0)),
        ],
        out_specs=pl.BlockSpec((1, t, ch), lambda bi: (bi, 0, 0)),
        out_shape=jax.ShapeDtypeStruct((b, t, ch), BF16),
        scratch_shapes=[pltpu.VMEM((t + 2 * HALO, POOL_GC), F32)],
        compiler_params=pltpu.CompilerParams(
            dimension_semantics=("parallel",),
            vmem_limit_bytes=_vmem_limit(blocks, scratch, 6 * _nbytes((t, POOL_GC), F32))),
        name="pool_mix",
    )(u, pool_w, pool_scale)


def _outproj_kernel(*refs, n_in):
    a_refs = refs[:n_in]
    w_refs = refs[n_in:2 * n_in]
    x_ref, gate_ref, g_ref, o_ref = refs[2 * n_in:]
    y = jnp.dot(a_refs[0][0], w_refs[0][...], preferred_element_type=F32)
    for a_ref, w_ref in zip(a_refs[1:], w_refs[1:]):
        y = y + jnp.dot(a_ref[0], w_ref[...], preferred_element_type=F32)
    o_ref[0] = x_ref[0] + gate_ref[0] * _rms_norm(y, g_ref[...])


def _outproj(acts, weights, x, mods, mod_row, post_g):
    b, t, d = x.shape
    tm = _row_tile(t, PROJ_ROW_TILE)
    n_in = len(acts)
    streamed = pl.Buffered(STREAM_BUFFERS)
    in_specs = [pl.BlockSpec((1, tm, a.shape[2]), lambda bi, i: (bi, i, 0),
                             pipeline_mode=streamed) for a in acts]
    in_specs += [pl.BlockSpec(w.shape, lambda bi, i: (0, 0)) for w in weights]
    in_specs += [
        pl.BlockSpec((1, tm, d), lambda bi, i: (bi, i, 0), pipeline_mode=streamed),
        pl.BlockSpec((1, 1, d), lambda bi, i: (mod_row(bi), 0, 2)),
        pl.BlockSpec((1, d), lambda bi, i: (0, 0)),
    ]
    blocks = (sum(_nbytes((tm, a.shape[2]), BF16) for a in acts) * STREAM_BUFFERS // 2
              + sum(_nbytes(w.shape, BF16) for w in weights)
              + (STREAM_BUFFERS + 2) * _nbytes((tm, d), F32) // 2)
    pipeline = pltpu.emit_pipeline(
        functools.partial(_outproj_kernel, n_in=n_in),
        grid=(b, t // tm),
        in_specs=in_specs,
        out_specs=[pl.BlockSpec((1, tm, d), lambda bi, i: (bi, i, 0))],
    )
    operands = (*acts, *weights, x, mods, post_g)
    return pl.pallas_call(
        lambda *refs: pipeline(*refs),
        in_specs=[pl.BlockSpec(memory_space=pl.ANY)] * len(operands),
        out_specs=pl.BlockSpec(memory_space=pl.ANY),
        out_shape=jax.ShapeDtypeStruct((b, t, d), F32),
        compiler_params=pltpu.CompilerParams(
            vmem_limit_bytes=_vmem_limit(blocks, temp_bytes=3 * _nbytes((tm, d), F32))),
        name="mixer_outproj",
    )(*operands)


def _ffn_kernel(x_ref, xp_ref, xn_ref, sh_ref, sc_ref, gate_ref, pre_g_ref, post_g_ref,
                wu_ref, wut_ref, cw_ref, cwt_ref, cb_ref, cbt_ref, wd_ref, wdt_ref, o_ref,
                h_ref, acc_ref, u_ref, ut_ref, *, tm, n_main):
    i = pl.program_id(1)
    g, sh, sc = pre_g_ref[...], sh_ref[0], sc_ref[0]
    h_prev = _pre_mod(xp_ref[0], g, sh, sc)
    h_next = _pre_mod(xn_ref[0], g, sh, sc)
    h_prev = jnp.where(i == 0, 0.0, h_prev)
    h_next = jnp.where(i == pl.num_programs(1) - 1, 0.0, h_next)
    h_ref[0:HALO, :] = h_prev.astype(BF16)
    h_ref[HALO:HALO + tm, :] = _pre_mod(x_ref[0], g, sh, sc).astype(BF16)
    h_ref[HALO + tm:2 * HALO + tm, :] = h_next.astype(BF16)

    def up(c):
        return jnp.dot(h_ref[...], wu_ref[c], preferred_element_type=F32)

    def conv_down(u, cw, cb, wd):
        w = wd.shape[0]
        uv = u[...]
        n_rows = uv.shape[0]
        conv = (pltpu.roll(uv, 1, 0)[HALO:HALO + tm] * cw[0:1, :]
                + uv[HALO:HALO + tm] * cw[1:2, :]
                + pltpu.roll(uv, n_rows - 1, 0)[HALO:HALO + tm] * cw[2:3, :]
                + cb)
        gated = jax.nn.gelu(conv[:, :w]) * conv[:, w:]
        return jnp.dot(gated.astype(BF16), wd, preferred_element_type=F32)

    def consume(slot, c):
        return conv_down(u_ref.at[slot], cw_ref[c], cb_ref[c], wd_ref[c])

    u_ref[0] = up(0)
    acc_ref[...] = jnp.zeros_like(acc_ref)

    def body(pair, carry):
        c = 2 * pair
        u_ref[1] = up(c + 1)
        acc_ref[...] += consume(0, c)
        u_ref[0] = up(c + 2)
        acc_ref[...] += consume(1, c + 1)
        return carry
    lax.fori_loop(0, (n_main - 1) // 2, body, 0, unroll=True)
    last = n_main - 1
    if n_main % 2 == 0:
        u_ref[1] = up(last)
        acc_ref[...] += consume(0, last - 1)
    ut_ref[...] = jnp.dot(h_ref[...], wut_ref[...], preferred_element_type=F32)
    acc_ref[...] += consume(last % 2, last)
    y = acc_ref[...] + conv_down(ut_ref, cwt_ref[...], cbt_ref[...], wdt_ref[...])
    o_ref[0] = x_ref[0] + gate_ref[0] * _rms_norm(y, post_g_ref[...])


def _ffn_weights(w_up, conv_w, conv_b, w_down):
    d_ff = w_down.shape[0]
    n_main = d_ff // FFN_CHUNK
    split = n_main * FFN_CHUNK

    def pair_cols(a):
        r = a.shape[0]
        gelu, gate = a[:, :d_ff], a[:, d_ff:]
        main = jnp.concatenate([gelu[:, :split].reshape(r, n_main, FFN_CHUNK),
                                gate[:, :split].reshape(r, n_main, FFN_CHUNK)], axis=2)
        tail = jnp.concatenate([gelu[:, split:], gate[:, split:]], axis=1)
        return jnp.swapaxes(main, 0, 1), tail

    wu, wut = pair_cols(w_up.astype(BF16))
    cw, cwt = pair_cols(conv_w)
    cb, cbt = pair_cols(conv_b[None])
    wd = w_down.astype(BF16)
    return (wu, wut, cw, cwt, cb, cbt, wd[:split].reshape(n_main, FFN_CHUNK, -1), wd[split:])


def _conv_ffn(x, mods, mod_row, pre_g, post_g, weights, *, tm_want=ROW_TILE):
    b, t, d = x.shape
    tm = _row_tile(t, tm_want)
    n_main = weights[0].shape[0]
    tail = weights[-1].shape[0]
    assert tm % HALO == 0 and n_main >= 2 and tail > 0
    halo_blocks = tm // HALO
    last_halo = t // HALO - 1
    rows = tm + 2 * HALO

    def resident(a):
        zeros = (0,) * a.ndim
        return pl.BlockSpec(a.shape, lambda bi, i: zeros, pipeline_mode=pl.Buffered(1))

    in_specs = [
        pl.BlockSpec((1, tm, d), lambda bi, i: (bi, i, 0)),
        pl.BlockSpec((1, HALO, d), lambda bi, i: (bi, jnp.maximum(i * halo_blocks - 1, 0), 0)),
        pl.BlockSpec((1, HALO, d),
                     lambda bi, i: (bi, jnp.minimum((i + 1) * halo_blocks, last_halo), 0)),
        pl.BlockSpec((1, 1, d), lambda bi, i: (mod_row(bi), 0, 3)),
        pl.BlockSpec((1, 1, d), lambda bi, i: (mod_row(bi), 0, 4)),
        pl.BlockSpec((1, 1, d), lambda bi, i: (mod_row(bi), 0, 5)),
        pl.BlockSpec((1, d), lambda bi, i: (0, 0)),
        pl.BlockSpec((1, d), lambda bi, i: (0, 0)),
    ] + [resident(a) for a in weights]
    weight_bytes = sum(_nbytes(a.shape, a.dtype) for a in weights)
    blocks = 2 * _nbytes((tm, d), F32) + 2 * _nbytes((HALO, d), F32)
    scratch = (_nbytes((rows, d), BF16) + _nbytes((tm, d), F32)
               + _nbytes((2, rows, 2 * FFN_CHUNK), F32) + _nbytes((rows, 2 * tail), F32)
               + weight_bytes)
    temps = 4 * _nbytes((tm, 2 * FFN_CHUNK), F32) + _nbytes((tm, d), F32)
    return pl.pallas_call(
        functools.partial(_ffn_kernel, tm=tm, n_main=n_main),
        grid=(b, t // tm),
        in_specs=in_specs,
        out_specs=pl.BlockSpec((1, tm, d), lambda bi, i: (bi, i, 0)),
        out_shape=jax.ShapeDtypeStruct((b, t, d), F32),
        scratch_shapes=[
            pltpu.VMEM((rows, d), BF16),
            pltpu.VMEM((tm, d), F32),
            pltpu.VMEM((2, rows, 2 * FFN_CHUNK), F32),
            pltpu.VMEM((rows, 2 * tail), F32),
        ],
        compiler_params=pltpu.CompilerParams(
            dimension_semantics=("parallel", "parallel"),
            vmem_limit_bytes=_vmem_limit(blocks, scratch, temps)),
        name="conv_ffn",
    )(x, x, x, mods, mods, mods, pre_g, post_g, *weights)


def _rope_tables(t):
    rows = t // GRID_W
    row = jnp.repeat(jnp.arange(rows), GRID_W).astype(F32)
    col = jnp.tile(jnp.arange(GRID_W), rows).astype(F32)
    n_freq = HEAD_DIM // 4
    freqs = ROPE_BASE ** (-jnp.arange(n_freq, dtype=F32) / n_freq)
    ang = jnp.concatenate([row[:, None] * freqs, col[:, None] * freqs], axis=-1)
    cos, sin = jnp.cos(ang), jnp.sin(ang)
    return (jnp.concatenate([cos, cos, cos, cos], axis=-1),
            jnp.concatenate([-sin, sin, -sin, sin], axis=-1))


def _dup_heads(w, n_heads):
    d = w.shape[0]
    w = w.reshape(d, n_heads, 1, HEAD_DIM)
    return jnp.broadcast_to(w, (d, n_heads, 2, HEAD_DIM)).reshape(d, n_heads * PAIR)


def _sink_row(sink, tq):
    return jnp.repeat(sink.astype(F32) * LOG2E, tq, axis=1)[:, None, :]


def kernel(x, c, ctx, c_ctx, ada_w, ada_b, mix_pre_g, mix_post_g, ffn_pre_g, ffn_post_g,
           ab_w_in, ab_w_out, pool_w, pool_scale, sink_logit,
           c_w_qkv, c_w_out, c_q_g, c_k_g,
           ffn_w_up, ffn_conv_w, ffn_conv_b, ffn_w_down):
    batch, t, d = x.shape
    n_ctx = ctx.shape[1]
    depth = ada_w.shape[0]
    pool_ch = pool_scale.shape[1]
    b_kv = sink_logit.shape[1]
    b_q = b_kv * GQA_GROUP * HEAD_DIM
    c_q = c_w_out.shape[1]
    c_kv = c_q // HEAD_DIM // GQA_GROUP
    d_ff = ffn_w_down.shape[1]

    ctx_row = batch
    n_rows = -(-(batch + 1) // SUBLANES) * SUBLANES
    cc = jnp.zeros((n_rows, d), F32).at[:batch].set(c).at[ctx_row].set(c_ctx)
    mods_all = _ada_params(cc, ada_w, ada_b).reshape(depth, n_rows, 1, N_MOD * d)

    rope_tabs = _rope_tables(t)
    ones_bd = jnp.kron(jnp.ones((2, 1), F32),
                       jnp.kron(jnp.eye(2, dtype=F32), jnp.ones((HEAD_DIM, HEAD_DIM), F32))
                       ).astype(BF16)
    latent_row = lambda bi: bi
    context_row = lambda bi: ctx_row
    win_tq = _row_tile(t, Q_TILE)
    ctx_tq = _row_tile(n_ctx, Q_TILE)

    def qkv_weights(w, n_lead, n_kv_heads):
        k_end = n_lead + n_kv_heads * HEAD_DIM
        return jnp.concatenate([w[:, :n_lead], _dup_heads(w[:, n_lead:k_end], n_kv_heads),
                                w[:, k_end:]], axis=1).astype(BF16)

    xc = ctx
    for l in range(depth):
        need_ctx = l < depth - 1
        i = l // 2
        mods = mods_all[l]
        pre_g, post_g = mix_pre_g[l][None], mix_post_g[l][None]
        if l % 2 == 0:
            w_in = qkv_weights(ab_w_in[i], pool_ch + b_q, b_kv)
            widths = dict(n_u=pool_ch, n_q=b_q, n_kv=b_kv * PAIR, n_v=b_kv * HEAD_DIM)
            u, q, k2, vt = _inproj(x, mods, latent_row, pre_g, w_in, rope_tabs=rope_tabs,
                                   key_chunk=BAND_CHUNK, **widths)
            uc, qc, kc2, vtc = _inproj(xc, mods, context_row, pre_g, w_in, **widths)
            w_out = ab_w_out[i].astype(BF16)
            w_outs = [w_out[:pool_ch], w_out[pool_ch:]]
            pw, ps = pool_w[i].astype(BF16), pool_scale[i][None]
            a = _window_attention(q, k2, vt, kc2, vtc, _sink_row(sink_logit[i], win_tq),
                                  tq=win_tq)
            x = _outproj([_pool_mix(u, pw, ps), a], w_outs, x, mods, latent_row, post_g)
            if need_ctx:
                ac = _flash_attention(qc, kc2, vtc, sink_row=_sink_row(sink_logit[i], ctx_tq))
                xc = _outproj([_pool_mix(uc, pw, ps), ac], w_outs, xc, mods, context_row, post_g)
        else:
            w_in = qkv_weights(c_w_qkv[i], c_q, c_kv)
            gains = (jnp.tile(c_q_g[i], 2)[None], jnp.tile(c_k_g[i], 2)[None], ones_bd)
            widths = dict(n_u=0, n_q=c_q, n_kv=c_kv * PAIR, n_v=c_kv * HEAD_DIM)
            q, k2, vt = _inproj(x, mods, latent_row, pre_g, w_in, rope_tabs=rope_tabs,
                                qk_gains=gains, key_chunk=_flash_chunk(t), **widths)
            qc, kc2, vtc = _inproj(xc, mods, context_row, pre_g, w_in, qk_gains=gains, **widths)
            w_outs = [c_w_out[i].astype(BF16)]
            a = _flash_attention(q, kc2, vtc, k2, vt)
            x = _outproj([a], w_outs, x, mods, latent_row, post_g)
            if need_ctx:
                ac = _flash_attention(qc, kc2, vtc)
                xc = _outproj([ac], w_outs, xc, mods, context_row, post_g)

        ffn_args = (ffn_pre_g[l][None], ffn_post_g[l][None],
                    _ffn_weights(ffn_w_up[l], ffn_conv_w[l], ffn_conv_b[l], ffn_w_down[l]))
        x = _conv_ffn(x, mods, latent_row, *ffn_args)
        if need_ctx:
            xc = _conv_ffn(xc, mods, context_row, *ffn_args)
    return x
```
